```python
import math
import jax
import jax.numpy as jnp
from jax import lax
import numpy as np

D_MODEL = 1024
BATCH = 16
SEQ = 256
DEPTH = 4
DEC_BATCH = 8
DEC_SEQ = 1024
PAST_LEN = 512

GRID_W = 64
ROPE_THETA = 10000.0
EPS = 1e-6
Q_BLOCK = 128

A_HEADS = 4
A_QK_DIM = 64
A_V_DIM = 2 * A_QK_DIM
B_HEADS = 4
B_NOPE = 64
B_ROPE = 32
B_QK_DIM = B_NOPE + B_ROPE
B_V_DIM = 64
B_Q_LORA = 256
B_KV_LORA = 128
C_CH = 256
C_CONV_WIDTH = 31
POOL_WINDOWS = (2, 4, 8, 16)
D_GROUP = 64
D_CH = D_GROUP * len(POOL_WINDOWS)
N_BRANCH = 4
FFN_DIM = 2816
N_EXPERTS = 8
TOP_K = 2
EXPERT_DIM = 3584
N_DENSE = (DEPTH + 1) // 2
N_MOE = DEPTH // 2

IN_SPLIT = (
    2 * A_HEADS * A_QK_DIM,
    2 * A_HEADS * A_QK_DIM,
    A_HEADS * A_V_DIM,
    B_Q_LORA,
    B_KV_LORA,
    B_ROPE,
    2 * C_CH,
    D_CH,
    N_BRANCH * D_MODEL,
)
IN_DIM = sum(IN_SPLIT)

kernel_name = 'hybrid_diffusion_prefix_trunk_step'


def split_columns(p, sizes):
    out, start = [], 0
    for s in sizes:
        out.append(p[..., start:start + s])
        start += s
    return out


def rms_norm(x, g):
    xf = x.astype(jnp.float32)
    y = xf * lax.rsqrt(jnp.mean(xf * xf, axis=-1, keepdims=True) + EPS)
    return (y * g.astype(jnp.float32)).astype(x.dtype)


def layer_norm(x, g, b):
    xf = x.astype(jnp.float32)
    mu = jnp.mean(xf, axis=-1, keepdims=True)
    xc = xf - mu
    y = xc * lax.rsqrt(jnp.mean(xc * xc, axis=-1, keepdims=True) + EPS)
    return (y * g.astype(jnp.float32) + b.astype(jnp.float32)).astype(x.dtype)


def axial_rope(x, rows):
    nf = x.shape[-1] // 4
    freqs = ROPE_THETA ** (-jnp.arange(nf, dtype=jnp.float32) / nf)
    row = jnp.repeat(jnp.arange(rows, dtype=jnp.float32), GRID_W)
    col = jnp.tile(jnp.arange(GRID_W, dtype=jnp.float32), rows)
    ang = jnp.concatenate([row[:, None] * freqs, col[:, None] * freqs], axis=-1)[None, :, None, :]
    cos, sin = jnp.cos(ang), jnp.sin(ang)
    xf = x.astype(jnp.float32)
    x1, x2 = xf[..., 0::2], xf[..., 1::2]
    return jnp.stack([x1 * cos - x2 * sin, x1 * sin + x2 * cos], axis=-1).reshape(x.shape).astype(x.dtype)


def rope_tail(x, rows):
    return jnp.concatenate([x[..., :B_NOPE], axial_rope(x[..., B_NOPE:], rows)], axis=-1)


def blocked_queries(fn, q):
    b, t = q.shape[:2]
    nb = t // Q_BLOCK
    qb = jnp.moveaxis(q.reshape(b, nb, Q_BLOCK, *q.shape[2:]), 1, 0)
    out = lax.map(fn, qb)
    return jnp.moveaxis(out, 0, 1).reshape(b, t, *out.shape[3:])


def diff_attention(q, k, v, lam):
    b, tk = q.shape[0], k.shape[1]
    scale = A_QK_DIM ** -0.5

    def block(qb):
        s = jnp.einsum('bqhd,bkhd->bhqk', qb, k, preferred_element_type=jnp.float32) * scale
        p = jax.nn.softmax(s, axis=-1).reshape(b, A_HEADS, 2, qb.shape[1], tk)
        w = p[:, :, 0] - lam * p[:, :, 1]
        return jnp.einsum('bhqk,bkhd->bqhd', w.astype(v.dtype), v)

    return blocked_queries(block, q)


def softmax_attention(q, k, v):
    scale = q.shape[-1] ** -0.5

    def block(qb):
        s = jnp.einsum('bqhd,bkhd->bhqk', qb, k, preferred_element_type=jnp.float32) * scale
        p = jax.nn.softmax(s, axis=-1)
        return jnp.einsum('bhqk,bkhd->bqhd', p.astype(v.dtype), v)

    return blocked_queries(block, q)


def mla_expand(ckv, kpe, lp):
    b, t = ckv.shape[:2]
    kv = (ckv @ lp['b_w_ukv']).reshape(b, t, B_HEADS, B_NOPE + B_V_DIM)
    k = jnp.concatenate([kv[..., :B_NOPE], jnp.broadcast_to(kpe[:, :, None, :], (b, t, B_HEADS, B_ROPE))], axis=-1)
    return rms_norm(k, lp['b_k_norm']), kv[..., B_NOPE:]


def centred_window_mean(x, w):
    b, t, g = x.shape
    left = w // 2
    right = w - 1 - left
    cs = jnp.concatenate([jnp.zeros((b, 1, g), jnp.float32), jnp.cumsum(x.astype(jnp.float32), axis=1)], axis=1)
    pos = jnp.arange(t)
    lo = jnp.clip(pos - left, 0, t)
    hi = jnp.clip(pos + right + 1, 0, t)
    s = jnp.take(cs, hi, axis=1) - jnp.take(cs, lo, axis=1)
    cnt = (hi - lo).astype(jnp.float32)[None, :, None]
    return (s / cnt).astype(x.dtype)


def mixer(m, lp, layer, ctx):
    b, t, _ = m.shape
    latent = ctx is not None
    a_q, a_k, a_v, b_q, b_kv, b_kpe, c_in, d_in, gate_logits = split_columns(m @ lp['w_in'], IN_SPLIT)

    qa = rms_norm(a_q.reshape(b, t, 2 * A_HEADS, A_QK_DIM), lp['a_q_norm'])
    ka = rms_norm(a_k.reshape(b, t, 2 * A_HEADS, A_QK_DIM), lp['a_k_norm'])
    va = a_v.reshape(b, t, A_HEADS, A_V_DIM)
    lam_init = 0.8 - 0.6 * math.exp(-0.3 * layer)
    lmb = lp['a_lambda'].astype(jnp.float32)
    lam = jnp.exp(jnp.sum(lmb[0] * lmb[1])) - jnp.exp(jnp.sum(lmb[2] * lmb[3])) + lam_init

    qb = (rms_norm(b_q, lp['b_q_lora_norm']) @ lp['b_w_uq']).reshape(b, t, B_HEADS, B_QK_DIM)
    qb = rms_norm(qb, lp['b_q_norm'])
    ckv = rms_norm(b_kv, lp['b_kv_lora_norm'])
    kb, vb = mla_expand(ckv, b_kpe, lp)

    if latent:
        rows = t // GRID_W
        qa = axial_rope(qa, rows)
        keys_a = jnp.concatenate([axial_rope(ka, rows), ctx[0]], axis=1)
        vals_a = jnp.concatenate([va, ctx[1]], axis=1)
        qb = rope_tail(qb, rows)
        kc, vc = mla_expand(ctx[2], ctx[3], lp)
        keys_b = jnp.concatenate([rope_tail(kb, rows), kc], axis=1)
        vals_b = jnp.concatenate([vb, vc], axis=1)
    else:
        keys_a, vals_a, keys_b, vals_b = ka, va, kb, vb

    oa = rms_norm(diff_attention(qa, keys_a, vals_a, lam), lp['a_sub_norm']) * (1.0 - lam_init)
    out_a = oa.reshape(b, t, A_HEADS * A_V_DIM) @ lp['a_w_o']
    ob = softmax_attention(qb, keys_b, vals_b)
    out_b = ob.reshape(b, t, B_HEADS * B_V_DIM) @ lp['b_w_o']

    glu = c_in[..., :C_CH] * jax.nn.sigmoid(c_in[..., C_CH:])
    conv = lax.conv_general_dilated(
        glu, lp['c_dw'][:, None, :], window_strides=(1,),
        padding=[(C_CONV_WIDTH // 2, C_CONV_WIDTH // 2)],
        dimension_numbers=('NWC', 'WIO', 'NWC'), feature_group_count=C_CH) + lp['c_dw_b']
    out_c = jax.nn.silu(layer_norm(conv, lp['c_ln_g'], lp['c_ln_b'])) @ lp['c_w_o']

    xg = d_in.reshape(b, t, len(POOL_WINDOWS), D_GROUP)
    pooled = jnp.stack([centred_window_mean(xg[:, :, i], w) for i, w in enumerate(POOL_WINDOWS)], axis=2)
    hd = jnp.einsum('btgc,gce->btge', pooled - xg, lp['d_w_group']).reshape(b, t, D_CH) * lp['d_scale']
    out_d = hd @ lp['d_w_o']

    gates = jax.nn.sigmoid(gate_logits.reshape(b, t, N_BRANCH, D_MODEL))
    merged = (gates[..., 0, :] * out_a + gates[..., 1, :] * out_b
              + gates[..., 2, :] * out_c + gates[..., 3, :] * out_d)
    y = merged @ lp['w_out']
    new_ctx = None if latent else (ka, va, ckv, b_kpe)
    return y, new_ctx


def swiglu(x, w1, w3, w2):
    return (jax.nn.silu(x @ w1) * (x @ w3)) @ w2


def moe_swiglu(x, router, w1, w3, w2):
    logits = (x @ router).astype(jnp.float32)
    top_val, top_idx = lax.top_k(logits, TOP_K)
    wts = jax.nn.softmax(top_val, axis=-1)
    comb = jnp.sum(jax.nn.one_hot(top_idx, N_EXPERTS, dtype=jnp.float32) * wts[..., None], axis=-2).astype(x.dtype)
    y = jnp.zeros_like(x)
    for e in range(N_EXPERTS):
        y = y + comb[..., e:e + 1] * swiglu(x, w1[e], w3[e], w2[e])
    return y


def trunk_layer(x, cvec, lp, layer, ctx):
    mod = (jax.nn.silu(cvec) @ lp['ada_w'] + lp['ada_b']).reshape(*cvec.shape[:-1], 1, 6, D_MODEL)
    sh1, sc1, g1, sh2, sc2, g2 = (mod[..., i, :] for i in range(6))
    y, new_ctx = mixer(rms_norm(x, lp['norm1_g']) * (1 + sc1) + sh1, lp, layer, ctx)
    x = x + g1 * y
    h = rms_norm(x, lp['norm2_g']) * (1 + sc2) + sh2
    if layer % 2 == 0:
        f = swiglu(h, lp['ffn_w1'], lp['ffn_w3'], lp['ffn_w2'])
    else:
        f = moe_swiglu(h, lp['moe_router'], lp['moe_w1'], lp['moe_w3'], lp['moe_w2'])
    return x + g2 * f, new_ctx


def setup_inputs(seed: int = 0) -> dict:
    key = jax.random.key(seed)
    ks = iter(jax.random.split(key, 64))

    def nrm(shape, scale):
        return scale * jax.random.normal(next(ks), shape, jnp.float32)

    def gain(shape, s=0.02):
        return 1.0 + s * jax.random.normal(next(ks), shape, jnp.float32)

    L = DEPTH
    return {
        'x_prompt': nrm((BATCH, SEQ, D_MODEL), 1.0),
        'x_sample': nrm((DEC_BATCH, DEC_SEQ, D_MODEL), 1.0),
        'cache_a_k': nrm((DEC_BATCH, DEPTH, PAST_LEN, 2 * A_HEADS, A_QK_DIM), 1.0),
        'cache_a_v': nrm((DEC_BATCH, DEPTH, PAST_LEN, A_HEADS, A_V_DIM), 1.0),
        'cache_b_ckv': nrm((DEC_BATCH, DEPTH, PAST_LEN, B_KV_LORA), 1.0),
        'cache_b_kpe': nrm((DEC_BATCH, DEPTH, PAST_LEN, B_ROPE), 1.0),
        'c': nrm((DEC_BATCH, D_MODEL), 1.0),
        'c_ctx': nrm((D_MODEL,), 1.0),
        'ada_w': nrm((L, D_MODEL, 6 * D_MODEL), 0.5 * D_MODEL ** -0.5),
        'ada_b': nrm((L, 6 * D_MODEL), 0.02),
        'norm1_g': gain((L, D_MODEL)),
        'norm2_g': gain((L, D_MODEL)),
        'w_in': nrm((L, D_MODEL, IN_DIM), D_MODEL ** -0.5),
        'a_q_norm': gain((L, A_QK_DIM)),
        'a_k_norm': gain((L, A_QK_DIM)),
        'a_lambda': nrm((L, 4, A_QK_DIM), 0.1),
        'a_sub_norm': gain((L, A_V_DIM)),
        'a_w_o': nrm((L, A_HEADS * A_V_DIM, D_MODEL), (A_HEADS * A_V_DIM) ** -0.5),
        'b_q_lora_norm': gain((L, B_Q_LORA)),
        'b_kv_lora_norm': gain((L, B_KV_LORA)),
        'b_w_uq': nrm((L, B_Q_LORA, B_HEADS * B_QK_DIM), B_Q_LORA ** -0.5),
        'b_w_ukv': nrm((L, B_KV_LORA, B_HEADS * (B_NOPE + B_V_DIM)), B_KV_LORA ** -0.5),
        'b_q_norm': gain((L, B_QK_DIM)),
        'b_k_norm': gain((L, B_QK_DIM)),
        'b_w_o': nrm((L, B_HEADS * B_V_DIM, D_MODEL), (B_HEADS * B_V_DIM) ** -0.5),
        'c_dw': nrm((L, C_CONV_WIDTH, C_CH), C_CONV_WIDTH ** -0.5),
        'c_dw_b': nrm((L, C_CH), 0.02),
        'c_ln_g': gain((L, C_CH)),
        'c_ln_b': nrm((L, C_CH), 0.02),
        'c_w_o': nrm((L, C_CH, D_MODEL), C_CH ** -0.5),
        'd_w_group': nrm((L, len(POOL_WINDOWS), D_GROUP, D_GROUP), D_GROUP ** -0.5),
        'd_scale': gain((L, D_CH), 0.1),
        'd_w_o': nrm((L, D_CH, D_MODEL), D_CH ** -0.5),
        'w_out': nrm((L, D_MODEL, D_MODEL), D_MODEL ** -0.5),
        'ffn_w1': nrm((N_DENSE, D_MODEL, FFN_DIM), D_MODEL ** -0.5),
        'ffn_w3': nrm((N_DENSE, D_MODEL, FFN_DIM), D_MODEL ** -0.5),
        'ffn_w2': nrm((N_DENSE, FFN_DIM, D_MODEL), FFN_DIM ** -0.5),
        'moe_router': nrm((N_MOE, D_MODEL, N_EXPERTS), D_MODEL ** -0.5),
        'moe_w1': nrm((N_MOE, N_EXPERTS, D_MODEL, EXPERT_DIM), D_MODEL ** -0.5),
        'moe_w3': nrm((N_MOE, N_EXPERTS, D_MODEL, EXPERT_DIM), D_MODEL ** -0.5),
        'moe_w2': nrm((N_MOE, N_EXPERTS, EXPERT_DIM, D_MODEL), EXPERT_DIM ** -0.5),
    }


def reference(x_prompt, x_sample, cache_a_k, cache_a_v, cache_b_ckv, cache_b_kpe, c, c_ctx,
              ada_w, ada_b, norm1_g, norm2_g, w_in,
              a_q_norm, a_k_norm, a_lambda, a_sub_norm, a_w_o,
              b_q_lora_norm, b_kv_lora_norm, b_w_uq, b_w_ukv, b_q_norm, b_k_norm, b_w_o,
              c_dw, c_dw_b, c_ln_g, c_ln_b, c_w_o,
              d_w_group, d_scale, d_w_o,
              w_out,
              ffn_w1, ffn_w3, ffn_w2,
              moe_router, moe_w1, moe_w3, moe_w2):
    y_prompt, y_sample = x_prompt, x_sample
    ak_list, av_list, ckv_list, kpe_list = [], [], [], []
    for l in range(DEPTH):
        lp = {
            'ada_w': ada_w[l], 'ada_b': ada_b[l], 'norm1_g': norm1_g[l], 'norm2_g': norm2_g[l],
            'w_in': w_in[l],
            'a_q_norm': a_q_norm[l], 'a_k_norm': a_k_norm[l], 'a_lambda': a_lambda[l],
            'a_sub_norm': a_sub_norm[l], 'a_w_o': a_w_o[l],
            'b_q_lora_norm': b_q_lora_norm[l], 'b_kv_lora_norm': b_kv_lora_norm[l],
            'b_w_uq': b_w_uq[l], 'b_w_ukv': b_w_ukv[l], 'b_q_norm': b_q_norm[l], 'b_k_norm': b_k_norm[l],
            'b_w_o': b_w_o[l],
            'c_dw': c_dw[l], 'c_dw_b': c_dw_b[l], 'c_ln_g': c_ln_g[l], 'c_ln_b': c_ln_b[l], 'c_w_o': c_w_o[l],
            'd_w_group': d_w_group[l], 'd_scale': d_scale[l], 'd_w_o': d_w_o[l],
            'w_out': w_out[l],
        }
        j = l // 2
        if l % 2 == 0:
            lp['ffn_w1'], lp['ffn_w3'], lp['ffn_w2'] = ffn_w1[j], ffn_w3[j], ffn_w2[j]
        else:
            lp['moe_router'], lp['moe_w1'], lp['moe_w3'], lp['moe_w2'] = moe_router[j], moe_w1[j], moe_w3[j], moe_w2[j]
        y_prompt, ctx_l = trunk_layer(y_prompt, c_ctx, lp, l, None)
        ak_list.append(ctx_l[0])
        av_list.append(ctx_l[1])
        ckv_list.append(ctx_l[2])
        kpe_list.append(ctx_l[3])
        y_sample, _ = trunk_layer(y_sample, c, lp, l,
                                  (cache_a_k[:, l], cache_a_v[:, l], cache_b_ckv[:, l], cache_b_kpe[:, l]))
    new_a_k = jnp.stack(ak_list, axis=1)
    new_a_v = jnp.stack(av_list, axis=1)
    new_b_ckv = jnp.stack(ckv_list, axis=1)
    new_b_kpe = jnp.stack(kpe_list, axis=1)
    return (y_prompt, y_sample, new_a_k, new_a_v, new_b_ckv, new_b_kpe)
```

```python
import functools
import math

import numpy as np
import jax
import jax.numpy as jnp
from jax import lax
from jax.experimental import pallas as pl
from jax.experimental.pallas import tpu as pltpu

F32 = jnp.float32
BF16 = jnp.bfloat16
I32 = jnp.int32

EPS = 1e-6
D_MODEL = 1024
GRID_W = 64
ROPE_THETA = 10000.0
A_HEADS = 4
A_QK = 64
A_V = 128
B_HEADS = 4
B_NOPE = 64
B_ROPE = 32
B_QK = B_NOPE + B_ROPE
B_V = 64
B_QL = 256
B_KVL = 128
B_HEAD_PAD = 128
C_CH = 256
C_W = 31
C_PAD = 16
POOL = (2, 4, 8, 16)
D_G = 64
D_CH = D_G * len(POOL)
D_PAD = 8
N_EXP = 8
N_EXP_PAD = 128

O_AQ, O_AK, O_AV, O_BQ, O_BKV, O_KPE, O_C, O_D, O_G, O_END = (
    0, 512, 1024, 1536, 1792, 1920, 1952, 2464, 2720, 6816)

TM = 512
TQ = 256
TG = 1024
TR = 256
VMEM_LIMIT = 56 * 1024 * 1024


def _cparams(sem):
    return pltpu.CompilerParams(dimension_semantics=sem, vmem_limit_bytes=VMEM_LIMIT)


def _dot(a, b):
    return jnp.dot(a, b, preferred_element_type=F32)


def _dot_nt(a, b):
    return lax.dot_general(a, b, (((1,), (1,)), ((), ())), preferred_element_type=F32)


def _bf(x):
    return x.astype(BF16)


def _rms(x, g):
    return x * lax.rsqrt(jnp.mean(x * x, axis=-1, keepdims=True) + EPS) * g


def _seg_sum_sq(x, bd_ref):
    xx = x * x
    hi = _bf(xx)
    lo = _bf(xx - hi.astype(F32))
    bd = bd_ref[...]
    return _dot(hi, bd) + _dot(lo, bd)


def _rope(x, c_ref, se_ref, so_ref):
    n = x.shape[-1]
    return (x * c_ref[...] + pltpu.roll(x, n - 1, 1) * se_ref[...]
            + pltpu.roll(x, 1, 1) * so_ref[...])


def _mod_row(i, tm, t_ctx, dec_seq):
    r = i * tm
    return jnp.where(r < t_ctx, 0, 1 + (r - t_ctx) // dec_seq)


def _ada_kernel(cv_ref, w_ref, b_ref, o_ref):
    cv = cv_ref[...]
    s = cv * jax.nn.sigmoid(cv)
    o_ref[...] = _dot(_bf(s), _bf(w_ref[...])) + b_ref[...]


def _ada_all(cv, ada_w, ada_b):
    L, d, n = ada_w.shape
    tn = 1536
    return pl.pallas_call(
        _ada_kernel,
        grid=(L, n // tn),
        in_specs=[pl.BlockSpec((16, d), lambda l, j: (0, 0)),
                  pl.BlockSpec((None, d, tn), lambda l, j: (l, 0, j)),
                  pl.BlockSpec((None, 1, tn), lambda l, j: (l, 0, j))],
        out_specs=pl.BlockSpec((None, 16, tn), lambda l, j: (l, 0, j)),
        out_shape=jax.ShapeDtypeStruct((L, 16, n), F32),
        compiler_params=_cparams(("parallel", "parallel")),
        name="ada_mod",
    )(cv, ada_w, ada_b.reshape(L, 1, n))


def _prep_kernel(x_ref, mod_ref, g_ref, h_ref):
    y = _rms(x_ref[...], g_ref[...])
    sh = mod_ref[:, 0:D_MODEL]
    sc = mod_ref[:, D_MODEL:2 * D_MODEL]
    h_ref[...] = _bf(y * (1.0 + sc) + sh)


def _prep(x, mod3, g, t_ctx, dec_seq):
    T = x.shape[0]
    row = functools.partial(_mod_row, tm=TM, t_ctx=t_ctx, dec_seq=dec_seq)
    return pl.pallas_call(
        _prep_kernel,
        grid=(T // TM,),
        in_specs=[pl.BlockSpec((TM, D_MODEL), lambda i: (i, 0)),
                  pl.BlockSpec((None, 1, 6 * D_MODEL), lambda i: (row(i), 0, 0)),
                  pl.BlockSpec((1, D_MODEL), lambda i: (0, 0))],
        out_specs=pl.BlockSpec((TM, D_MODEL), lambda i: (i, 0)),
        out_shape=jax.ShapeDtypeStruct((T, D_MODEL), BF16),
        compiler_params=_cparams(("parallel",)),
        name="prep",
    )(x, mod3, g)


def _proj_a_kernel(*refs, latent):
    if latent:
        (h_ref, w_ref, gq_ref, gk_ref, bd_ref, c_ref, se_ref, so_ref,
         q1_ref, q2_ref, k_ref, v_ref) = refs
    else:
        (h_ref, w_ref, gq_ref, gk_ref, bd_ref,
         q1_ref, q2_ref, k_ref, v_ref, nk_ref, nv_ref) = refs
    p = _dot(h_ref[...], w_ref[...])
    n = 2 * A_HEADS * A_QK
    q = p[:, 0:n]
    k = p[:, n:2 * n]
    v = p[:, 2 * n:3 * n]
    q = q * lax.rsqrt(_seg_sum_sq(q, bd_ref) * (1.0 / A_QK) + EPS) * gq_ref[...]
    k = k * lax.rsqrt(_seg_sum_sq(k, bd_ref) * (1.0 / A_QK) + EPS) * gk_ref[...]
    if latent:
        q = _rope(q, c_ref, se_ref, so_ref)
        k = _rope(k, c_ref, se_ref, so_ref)
    else:
        nk_ref[...] = k
        nv_ref[...] = v
    q = q * (A_QK ** -0.5)
    lane = lax.broadcasted_iota(I32, q.shape, 1)
    first = (lane % (2 * A_QK)) < A_QK
    q1_ref[...] = _bf(jnp.where(first, q, 0.0))
    q2_ref[...] = _bf(jnp.where(first, 0.0, q))
    k_ref[...] = _bf(k)
    v_ref[...] = _bf(v)


def _proj_a(h1, w_a, gq, gk, bd, rope_tabs, row0, rows, latent):
    n = 2 * A_HEADS * A_QK
    b0 = row0 // TM
    tab_blocks = rope_tabs[0].shape[0] // TM if latent else 1
    const = lambda i: (0, 0)
    in_specs = [pl.BlockSpec((TM, D_MODEL), lambda i: (i + b0, 0)),
                pl.BlockSpec((D_MODEL, 3 * n), const),
                pl.BlockSpec((1, n), const), pl.BlockSpec((1, n), const),
                pl.BlockSpec((n, n), const)]
    args = [h1, w_a, gq, gk, bd]
    if latent:
        in_specs += [pl.BlockSpec((TM, n), lambda i: (i % tab_blocks, 0))] * 3
        args += list(rope_tabs)
    out_spec = pl.BlockSpec((TM, n), lambda i: (i, 0))
    out_shape = [jax.ShapeDtypeStruct((rows, n), BF16)] * 4
    out_specs = [out_spec] * 4
    if not latent:
        out_shape += [jax.ShapeDtypeStruct((rows, n), F32)] * 2
        out_specs += [out_spec] * 2
    return pl.pallas_call(
        functools.partial(_proj_a_kernel, latent=latent),
        grid=(rows // TM,),
        in_specs=in_specs, out_specs=out_specs, out_shape=out_shape,
        compiler_params=_cparams(("parallel",)),
        name="proj_a_lat" if latent else "proj_a_ctx",
    )(*args)


def _softmax_parts(scores):
    m = functools.reduce(jnp.maximum, [jnp.max(s, axis=-1, keepdims=True) for s in scores])
    es = [jnp.exp(s - m) for s in scores]
    l = functools.reduce(lambda a, b: a + b, [jnp.sum(e, axis=-1, keepdims=True) for e in es])
    return es, 1.0 / l


def _attn_a_kernel(*refs, latent, lam_init):
    if latent:
        q1_ref, q2_ref, k_ref, v_ref, kc_ref, vc_ref, lam_ref, g_ref, o_ref = refs
    else:
        q1_ref, q2_ref, k_ref, v_ref, lam_ref, g_ref, o_ref = refs
    lm = lam_ref[...]
    lam = (jnp.exp(jnp.sum(lm[0:1] * lm[1:2], axis=-1, keepdims=True))
           - jnp.exp(jnp.sum(lm[2:3] * lm[3:4], axis=-1, keepdims=True)) + lam_init)
    for h in range(A_HEADS):
        sl = slice(A_V * h, A_V * (h + 1))
        ks = [k_ref[:, sl]]
        vs = [v_ref[:, sl]]
        if latent:
            ks.append(kc_ref[:, sl])
            vs.append(vc_ref[:, sl])
        q1 = q1_ref[:, sl]
        q2 = q2_ref[:, sl]
        e1, r1 = _softmax_parts([_dot_nt(q1, k) for k in ks])
        e2, r2 = _softmax_parts([_dot_nt(q2, k) for k in ks])
        r2 = r2 * lam
        o = None
        for a, b, v in zip(e1, e2, vs):
            t = _dot(_bf(a * r1 - b * r2), v)
            o = t if o is None else o + t
        o = _rms(o, g_ref[...]) * (1.0 - lam_init)
        o_ref[:, sl] = _bf(o)


def _attn_a(q1, q2, k, v, cache_k, cache_v, a_lambda, g_sub, layer, seq, latent):
    rows, n = q1.shape
    nb = rows // seq
    lam_init = 0.8 - 0.6 * math.exp(-0.3 * layer)
    kern = functools.partial(_attn_a_kernel, latent=latent, lam_init=lam_init)
    if latent:
        nq = seq // TQ
        past = cache_k.shape[2]
        grid = (nb, nq)
        qs = pl.BlockSpec((TQ, n), lambda b, j: (b * nq + j, 0))
        kv = pl.BlockSpec((seq, n), lambda b, j: (b, 0))
        cs = pl.BlockSpec((None, None, past, n), lambda b, j: (b, layer, 0, 0))
        const = lambda b, j: (0, 0)
        in_specs = [qs, qs, kv, kv, cs, cs,
                    pl.BlockSpec((4, A_QK), const), pl.BlockSpec((1, A_V), const)]
        args = (q1, q2, k, v, cache_k, cache_v, a_lambda, g_sub)
        sem = ("parallel", "parallel")
        out_spec = qs
    else:
        grid = (nb,)
        bs = pl.BlockSpec((seq, n), lambda b: (b, 0))
        const = lambda b: (0, 0)
        in_specs = [bs, bs, bs, bs, pl.BlockSpec((4, A_QK), const), pl.BlockSpec((1, A_V), const)]
        args = (q1, q2, k, v, a_lambda, g_sub)
        sem = ("parallel",)
        out_spec = bs
    return pl.pallas_call(
        kern, grid=grid, in_specs=in_specs, out_specs=out_spec,
        out_shape=jax.ShapeDtypeStruct((rows, n), BF16),
        compiler_params=_cparams(sem),
        name="attn_a_lat" if latent else "attn_a_ctx",
    )(*args)


def _mla_keys(ckv, kpe, wuk_ref, wuv_ref, gk_ref, bd_ref):
    cb = _bf(ckv)
    kn = _dot(cb, wuk_ref[...]) + jnp.concatenate([kpe] * B_HEADS, axis=1)
    k = kn * lax.rsqrt(_seg_sum_sq(kn, bd_ref) * (1.0 / B_QK) + EPS) * gk_ref[...]
    v = _dot(cb, wuv_ref[...])
    return k, v


def _proj_b_kernel(*refs, latent):
    if latent:
        (h_ref, w_ref, gql_ref, gkvl_ref, wuq_ref, wuk_ref, wuv_ref, gq_ref, gk_ref, bd_ref,
         c_ref, se_ref, so_ref, q_ref, k_ref, v_ref) = refs
    else:
        (h_ref, w_ref, gql_ref, gkvl_ref, wuq_ref, wuk_ref, wuv_ref, gq_ref, gk_ref, bd_ref,
         q_ref, k_ref, v_ref, nckv_ref, nkpe_ref) = refs
    p = _dot(h_ref[...], w_ref[...])
    bq = p[:, 0:B_QL]
    bkv = p[:, B_QL:B_QL + B_KVL]
    kpe = p[:, B_QL + B_KVL:B_QL + B_KVL + B_HEAD_PAD]
    q = _dot(_bf(_rms(bq, gql_ref[...])), wuq_ref[...])
    q = q * lax.rsqrt(_seg_sum_sq(q, bd_ref) * (1.0 / B_QK) + EPS) * gq_ref[...]
    ckv = _rms(bkv, gkvl_ref[...])
    k, v = _mla_keys(ckv, kpe, wuk_ref, wuv_ref, gk_ref, bd_ref)
    if latent:
        q = _rope(q, c_ref, se_ref, so_ref)
        k = _rope(k, c_ref, se_ref, so_ref)
    else:
        nckv_ref[...] = ckv
        nkpe_ref[...] = kpe
    q_ref[...] = _bf(q * (B_QK ** -0.5))
    k_ref[...] = _bf(k)
    v_ref[...] = _bf(v)


def _proj_b(h1, w_b, gql, gkvl, wuq, wuk, wuv, gq, gk, bd, rope_tabs, row0, rows, latent):
    n = B_HEADS * B_HEAD_PAD
    nv = B_HEADS * B_V
    b0 = row0 // TM
    tab_blocks = rope_tabs[0].shape[0] // TM if latent else 1
    const = lambda i: (0, 0)
    full = lambda a: pl.BlockSpec(a.shape, const)
    in_specs = [pl.BlockSpec((TM, D_MODEL), lambda i: (i + b0, 0))] + [
        full(a) for a in (w_b, gql, gkvl, wuq, wuk, wuv, gq, gk, bd)]
    args = [h1, w_b, gql, gkvl, wuq, wuk, wuv, gq, gk, bd]
    if latent:
        in_specs += [pl.BlockSpec((TM, n), lambda i: (i % tab_blocks, 0))] * 3
        args += list(rope_tabs)
    row = lambda w: pl.BlockSpec((TM, w), lambda i: (i, 0))
    out_shape = [jax.ShapeDtypeStruct((rows, n), BF16), jax.ShapeDtypeStruct((rows, n), BF16),
                 jax.ShapeDtypeStruct((rows, nv), BF16)]
    out_specs = [row(n), row(n), row(nv)]
    if not latent:
        out_shape += [jax.ShapeDtypeStruct((rows, B_KVL), F32),
                      jax.ShapeDtypeStruct((rows, B_HEAD_PAD), F32)]
        out_specs += [row(B_KVL), row(B_HEAD_PAD)]
    return pl.pallas_call(
        functools.partial(_proj_b_kernel, latent=latent),
        grid=(rows // TM,),
        in_specs=in_specs, out_specs=out_specs, out_shape=out_shape,
        compiler_params=_cparams(("parallel",)),
        name="proj_b_lat" if latent else "proj_b_ctx",
    )(*args)


def _cache_b_kernel(ckv_ref, kpe_ref, wuk_ref, wuv_ref, gk_ref, bd_ref, k_ref, v_ref):
    k, v = _mla_keys(ckv_ref[...], kpe_ref[...], wuk_ref, wuv_ref, gk_ref, bd_ref)
    k_ref[...] = _bf(k)
    v_ref[...] = _bf(v)


def _cache_b(ckv, kpe_pad, wuk, wuv, gk, bd):
    db, L, past, _ = ckv.shape
    n = B_HEADS * B_HEAD_PAD
    nv = B_HEADS * B_V
    blk = lambda w: pl.BlockSpec((None, None, past, w), lambda l, b: (b, l, 0, 0))
    wl = lambda a: pl.BlockSpec((None,) + a.shape[1:], lambda l, b: (l, 0, 0))
    return pl.pallas_call(
        _cache_b_kernel,
        grid=(L, db),
        in_specs=[blk(B_KVL), blk(B_HEAD_PAD), wl(wuk), wl(wuv), wl(gk),
                  pl.BlockSpec(bd.shape, lambda l, b: (0, 0))],
        out_specs=[blk(n), blk(nv)],
        out_shape=[jax.ShapeDtypeStruct((db, L, past, n), BF16),
                   jax.ShapeDtypeStruct((db, L, past, nv), BF16)],
        compiler_params=_cparams(("parallel", "parallel")),
        name="cache_b_expand",
    )(ckv, kpe_pad, wuk, wuv, gk, bd)


def _attn_b_kernel(*refs, latent):
    if latent:
        q_ref, k_ref, v_ref, kc_ref, vc_ref, o_ref = refs
    else:
        q_ref, k_ref, v_ref, o_ref = refs
    lane = lax.broadcasted_iota(I32, (q_ref.shape[0], 2 * B_V), 1)
    for hp in range(B_HEADS // 2):
        vsl = slice(2 * B_V * hp, 2 * B_V * (hp + 1))
        vs = [v_ref[:, vsl]] + ([vc_ref[:, vsl]] if latent else [])
        outs = []
        for h in (2 * hp, 2 * hp + 1):
            sl = slice(B_HEAD_PAD * h, B_HEAD_PAD * (h + 1))
            ks = [k_ref[:, sl]] + ([kc_ref[:, sl]] if latent else [])
            q = q_ref[:, sl]
            es, r = _softmax_parts([_dot_nt(q, k) for k in ks])
            o = None
            for e, v in zip(es, vs):
                t = _dot(_bf(e * r), v)
                o = t if o is None else o + t
            outs.append(o)
        o_ref[:, vsl] = _bf(jnp.where(lane < B_V, outs[0], outs[1]))


def _attn_b(q, k, v, cache_k, cache_v, layer, seq, latent):
    rows, n = q.shape
    nv = v.shape[1]
    nb = rows // seq
    kern = functools.partial(_attn_b_kernel, latent=latent)
    if latent:
        nq = seq // TQ
        past = cache_k.shape[2]
        grid = (nb, nq)
        in_specs = [pl.BlockSpec((TQ, n), lambda b, j: (b * nq + j, 0)),
                    pl.BlockSpec((seq, n), lambda b, j: (b, 0)),
                    pl.BlockSpec((seq, nv), lambda b, j: (b, 0)),
                    pl.BlockSpec((None, None, past, n), lambda b, j: (b, layer, 0, 0)),
                    pl.BlockSpec((None, None, past, nv), lambda b, j: (b, layer, 0, 0))]
        args = (q, k, v, cache_k, cache_v)
        out_spec = pl.BlockSpec((TQ, nv), lambda b, j: (b * nq + j, 0))
        sem = ("parallel", "parallel")
    else:
        grid = (nb,)
        in_specs = [pl.BlockSpec((seq, n), lambda b: (b, 0)),
                    pl.BlockSpec((seq, n), lambda b: (b, 0)),
                    pl.BlockSpec((seq, nv), lambda b: (b, 0))]
        args = (q, k, v)
        out_spec = pl.BlockSpec((seq, nv), lambda b: (b, 0))
        sem = ("parallel",)
    return pl.pallas_call(
        kern, grid=grid, in_specs=in_specs, out_specs=out_spec,
        out_shape=jax.ShapeDtypeStruct((rows, nv), BF16),
        compiler_params=_cparams(sem),
        name="attn_b_lat" if latent else "attn_b_ctx",
    )(*args)


CONV_CHUNK = 64


def _mix_cd_kernel(h_ref, w_ref, dw_ref, dwb_ref, lng_ref, lnb_ref, bdd_ref, dsc_ref, pm_ref,
                   oc_ref, od_ref, gpad, dpad, *, seq):
    p = _dot(h_ref[...], w_ref[...])
    glu = p[:, 0:C_CH] * jax.nn.sigmoid(p[:, C_CH:2 * C_CH])
    gpad[0:C_PAD, :] = jnp.zeros((C_PAD, C_CH), F32)
    gpad[C_PAD + seq:2 * C_PAD + seq, :] = jnp.zeros((C_PAD, C_CH), F32)
    gpad[C_PAD:C_PAD + seq, :] = glu
    half = C_W // 2
    for c0 in range(0, seq, CONV_CHUNK):
        acc = jnp.zeros((CONV_CHUNK, C_CH), F32) + dwb_ref[...]
        for j in range(C_W):
            s = c0 + C_PAD - half + j
            acc = acc + gpad[s:s + CONV_CHUNK, :] * dw_ref[j:j + 1, :]
        mu = jnp.mean(acc, axis=-1, keepdims=True)
        xc = acc - mu
        y = xc * lax.rsqrt(jnp.mean(xc * xc, axis=-1, keepdims=True) + EPS)
        y = y * lng_ref[...] + lnb_ref[...]
        oc_ref[c0:c0 + CONV_CHUNK, :] = _bf(y * jax.nn.sigmoid(y))
    d = p[:, 2 * C_CH:2 * C_CH + D_CH]
    dpad[0:D_PAD, :] = jnp.zeros((D_PAD, D_CH), F32)
    dpad[D_PAD + seq:2 * D_PAD + seq, :] = jnp.zeros((D_PAD, D_CH), F32)
    dpad[D_PAD:D_PAD + seq, :] = d
    lane = lax.broadcasted_iota(I32, (CONV_CHUNK, D_CH), 1)
    for c0 in range(0, seq, CONV_CHUNK):
        acc = jnp.zeros((CONV_CHUNK, D_CH), F32)
        for j in range(2 * D_PAD):
            acc = acc + dpad[c0 + j:c0 + j + CONV_CHUNK, :] * pm_ref[j:j + 1, :]
        t = c0 + lax.broadcasted_iota(I32, (CONV_CHUNK, D_CH), 0)
        cnt = jnp.zeros((CONV_CHUNK, D_CH), I32)
        for gi, w in enumerate(POOL):
            left = w // 2
            right = w - 1 - left
            c = jnp.minimum(t + right + 1, seq) - jnp.maximum(t - left, 0)
            cnt = jnp.where(lane // D_G == gi, c, cnt)
        pooled = acc / cnt.astype(F32)
        diff = pooled - dpad[c0 + D_PAD:c0 + D_PAD + CONV_CHUNK, :]
        od_ref[c0:c0 + CONV_CHUNK, :] = _bf(_dot(_bf(diff), bdd_ref[...]) * dsc_ref[...])


def _mix_cd(h1, w_cd, dw, dwb, lng, lnb, bdd, dsc, pmask, row0, rows, seq):
    b0 = row0 // seq
    const = lambda b: (0, 0)
    full = lambda a: pl.BlockSpec(a.shape, const)
    return pl.pallas_call(
        functools.partial(_mix_cd_kernel, seq=seq),
        grid=(rows // seq,),
        in_specs=[pl.BlockSpec((seq, D_MODEL), lambda b: (b + b0, 0))] + [
            full(a) for a in (w_cd, dw, dwb, lng, lnb, bdd, dsc, pmask)],
        out_specs=[pl.BlockSpec((seq, C_CH), lambda b: (b, 0)),
                   pl.BlockSpec((seq, D_CH), lambda b: (b, 0))],
        out_shape=[jax.ShapeDtypeStruct((rows, C_CH), BF16),
                   jax.ShapeDtypeStruct((rows, D_CH), BF16)],
        scratch_shapes=[pltpu.VMEM((seq + 2 * C_PAD, C_CH), F32),
                        pltpu.VMEM((seq + 2 * D_PAD, D_CH), F32)],
        compiler_params=_cparams(("parallel",)),
        name="mix_cd_%d" % seq,
    )(h1, w_cd, dw, dwb, lng, lnb, bdd, dsc, pmask)


def _merge_kernel(*refs, moe):
    (x_ref, h_ref, oa_ref, ob_ref, oc_ref, od_ref, mod_ref, n2g_ref,
     wg_ref, wa_ref, wb_ref, wc_ref, wd_ref, wo_ref) = refs[:14]
    if moe:
        rhi_ref, rlo_ref, x1_ref, h2_ref, h2f_ref, ridx_ref, rw_ref, sel_ref = refs[14:]
    else:
        x1_ref, h2_ref = refs[14:]
    h = h_ref[...]
    acc = None
    for i, (o_ref, w_ref) in enumerate(((oa_ref, wa_ref), (ob_ref, wb_ref),
                                        (oc_ref, wc_ref), (od_ref, wd_ref))):
        gate = jax.nn.sigmoid(_dot(h, wg_ref[:, i * D_MODEL:(i + 1) * D_MODEL]))
        t = gate * _dot(o_ref[...], w_ref[...])
        acc = t if acc is None else acc + t
    y = _dot(_bf(acc), wo_ref[...])
    x1 = x_ref[...] + mod_ref[:, 2 * D_MODEL:3 * D_MODEL] * y
    x1_ref[...] = x1
    h2 = (_rms(x1, n2g_ref[...]) * (1.0 + mod_ref[:, 4 * D_MODEL:5 * D_MODEL])
          + mod_ref[:, 3 * D_MODEL:4 * D_MODEL])
    h2_ref[...] = _bf(h2)
    if moe:
        h2f_ref[...] = h2
        hi = _bf(h2)
        lo = _bf(h2 - hi.astype(F32))
        logits = _dot(hi, rhi_ref[...]) + _dot(lo, rhi_ref[...]) + _dot(hi, rlo_ref[...])
        lane = lax.broadcasted_iota(I32, logits.shape, 1)
        lanef = lane.astype(F32)
        neg = jnp.float32(-jnp.inf)
        lg = jnp.where(lane < N_EXP, logits, neg)
        m0 = jnp.max(lg, axis=-1, keepdims=True)
        i0 = jnp.min(jnp.where(lg == m0, lanef, float(N_EXP_PAD)), axis=-1, keepdims=True)
        sel0 = lanef == i0
        lg1 = jnp.where(sel0, neg, lg)
        m1 = jnp.max(lg1, axis=-1, keepdims=True)
        i1 = jnp.min(jnp.where(lg1 == m1, lanef, float(N_EXP_PAD)), axis=-1, keepdims=True)
        sel1 = lanef == i1
        e = jnp.exp(m1 - m0)
        w0 = 1.0 / (1.0 + e)
        w1 = e / (1.0 + e)
        ridx_ref[...] = jnp.where(lane == 0, i0, jnp.where(lane == 1, i1, 0.0)).astype(I32)
        rw_ref[...] = jnp.where(lane == 0, w0, jnp.where(lane == 1, w1, 0.0))
        sel_ref[...] = jnp.where(sel0 | sel1, 1.0, 0.0).astype(BF16)


def _merge(x, h1, oa, ob, oc, od, mod3, n2g, wg, wa, wb, wc, wd, wo, router, t_ctx, dec_seq):
    T = x.shape[0]
    moe = router is not None
    row = functools.partial(_mod_row, tm=TM, t_ctx=t_ctx, dec_seq=dec_seq)
    const = lambda i: (0, 0)
    full = lambda a: pl.BlockSpec(a.shape, const)
    rowspec = lambda w: pl.BlockSpec((TM, w), lambda i: (i, 0))
    in_specs = [rowspec(D_MODEL), rowspec(D_MODEL), rowspec(oa.shape[1]), rowspec(ob.shape[1]),
                rowspec(oc.shape[1]), rowspec(od.shape[1]),
                pl.BlockSpec((None, 1, 6 * D_MODEL), lambda i: (row(i), 0, 0)),
                full(n2g), full(wg), full(wa), full(wb), full(wc), full(wd), full(wo)]
    args = [x, h1, oa, ob, oc, od, mod3, n2g, wg, wa, wb, wc, wd, wo]
    out_shape = [jax.ShapeDtypeStruct((T, D_MODEL), F32), jax.ShapeDtypeStruct((T, D_MODEL), BF16)]
    out_specs = [rowspec(D_MODEL), rowspec(D_MODEL)]
    if moe:
        in_specs += [full(router[0]), full(router[1])]
        args += list(router)
        out_shape += [jax.ShapeDtypeStruct((T, D_MODEL), F32),
                      jax.ShapeDtypeStruct((T, N_EXP_PAD), I32),
                      jax.ShapeDtypeStruct((T, N_EXP_PAD), F32),
                      jax.ShapeDtypeStruct((T, N_EXP_PAD), BF16)]
        out_specs += [rowspec(D_MODEL), rowspec(N_EXP_PAD), rowspec(N_EXP_PAD), rowspec(N_EXP_PAD)]
    return pl.pallas_call(
        functools.partial(_merge_kernel, moe=moe),
        grid=(T // TM,),
        in_specs=in_specs, out_specs=out_specs, out_shape=out_shape,
        compiler_params=_cparams(("parallel",)),
        name="merge_moe" if moe else "merge",
    )(*args)


def _ffn_kernel(te_ref, na_ref, x_ref, w1_ref, w3_ref, w2_ref, *rest, residual):
    if residual:
        x1_ref, mod_ref, o_ref, acc_ref = rest
    else:
        o_ref, = rest
        acc_ref = o_ref
    i = pl.program_id(0)
    f = pl.program_id(1)
    nf = pl.num_programs(1)
    active = i < na_ref[0]

    @pl.when(active)
    def _():
        x = x_ref[...]
        a = _dot(x, _bf(w1_ref[...]))
        b = _dot(x, _bf(w3_ref[...]))
        t = _dot(_bf(a * jax.nn.sigmoid(a) * b), _bf(w2_ref[...]))

        @pl.when(f == 0)
        def _():
            acc_ref[...] = t

        @pl.when(f > 0)
        def _():
            acc_ref[...] += t

    @pl.when(jnp.logical_and(jnp.logical_not(active), f == 0))
    def _():
        acc_ref[...] = jnp.zeros(acc_ref.shape, F32)

    if residual:
        @pl.when(f == nf - 1)
        def _():
            o_ref[...] = x1_ref[...] + mod_ref[:, 5 * D_MODEL:6 * D_MODEL] * acc_ref[...]


def _ffn(x, w1, w3, w2, tile_e, n_active, tf, resid=None):
    rows = x.shape[0]
    fdim = w1.shape[2]
    nf = fdim // tf
    nt = rows // TG

    def fidx(i, f, na):
        return jnp.where(i < na[0], f, nf - 1)

    in_specs = [pl.BlockSpec((TG, D_MODEL), lambda i, f, te, na: (i, 0)),
                pl.BlockSpec((None, D_MODEL, tf), lambda i, f, te, na: (te[i], 0, fidx(i, f, na))),
                pl.BlockSpec((None, D_MODEL, tf), lambda i, f, te, na: (te[i], 0, fidx(i, f, na))),
                pl.BlockSpec((None, tf, D_MODEL), lambda i, f, te, na: (te[i], fidx(i, f, na), 0))]
    args = [tile_e, n_active, x, w1, w3, w2]
    scratch = []
    if resid is not None:
        x1, mod3, t_ctx, dec_seq = resid
        row = functools.partial(_mod_row, tm=TG, t_ctx=t_ctx, dec_seq=dec_seq)
        in_specs += [pl.BlockSpec((TG, D_MODEL), lambda i, f, te, na: (i, 0)),
                     pl.BlockSpec((None, 1, 6 * D_MODEL), lambda i, f, te, na: (row(i), 0, 0))]
        args += [x1, mod3]
        scratch = [pltpu.VMEM((TG, D_MODEL), F32)]
    return pl.pallas_call(
        functools.partial(_ffn_kernel, residual=resid is not None),
        grid_spec=pltpu.PrefetchScalarGridSpec(
            num_scalar_prefetch=2, grid=(nt, nf), in_specs=in_specs,
            out_specs=pl.BlockSpec((TG, D_MODEL), lambda i, f, te, na: (i, 0)),
            scratch_shapes=scratch),
        out_shape=jax.ShapeDtypeStruct((rows, D_MODEL), F32),
        compiler_params=_cparams(("parallel", "arbitrary")),
        name="ffn_dense" if resid is not None else "ffn_experts",
    )(*args)


def _rank_kernel(sel_ref, tri_ref, rank_ref, cnt_ref, carry):
    i = pl.program_id(0)

    @pl.when(i == 0)
    def _():
        carry[...] = jnp.zeros(carry.shape, F32)

    s = sel_ref[...]
    rank_ref[...] = _dot(tri_ref[...], s) + carry[...]
    carry[...] += jnp.sum(s.astype(F32), axis=0, keepdims=True)
    cnt_ref[...] = jnp.broadcast_to(carry[...], cnt_ref.shape)


def _rank(sel, tri):
    T = sel.shape[0]
    return pl.pallas_call(
        _rank_kernel,
        grid=(T // TR,),
        in_specs=[pl.BlockSpec((TR, N_EXP_PAD), lambda i: (i, 0)),
                  pl.BlockSpec((TR, TR), lambda i: (0, 0))],
        out_specs=[pl.BlockSpec((TR, N_EXP_PAD), lambda i: (i, 0)),
                   pl.BlockSpec((8, N_EXP_PAD), lambda i: (0, 0))],
        out_shape=[jax.ShapeDtypeStruct((T, N_EXP_PAD), F32),
                   jax.ShapeDtypeStruct((8, N_EXP_PAD), F32)],
        scratch_shapes=[pltpu.VMEM((1, N_EXP_PAD), F32)],
        compiler_params=_cparams(("arbitrary",)),
        name="route_rank",
    )(sel, tri)


def _row_copy(src_hbm, row, buf, slot, r, sem):
    return pltpu.make_async_copy(src_hbm.at[pl.ds(row, 1)], buf.at[slot, pl.ds(r, 1)], sem.at[slot])


def _dispatch_kernel(pos0_ref, pos1_ref, h_hbm, xs_ref, tok, buf, sem, *, n_tok, n_rows):
    i = pl.program_id(0)
    nt = pl.num_programs(0)

    def issue(tile, slot):
        def body(r, c):
            _row_copy(h_hbm, tok[tile * TR + r], buf, slot, r, sem).start()
            return c
        lax.fori_loop(0, TR, body, 0)

    @pl.when(i == 0)
    def _():
        def zero(p, c):
            tok[p] = 0
            return c
        lax.fori_loop(0, n_rows, zero, 0)

        def fill(t, c):
            tok[pos0_ref[t]] = t
            tok[pos1_ref[t]] = t
            return c
        lax.fori_loop(0, n_tok, fill, 0)
        issue(0, 0)

    @pl.when(i + 1 < nt)
    def _():
        issue(i + 1, (i + 1) % 2)

    slot = i % 2

    def wait(r, c):
        _row_copy(h_hbm, 0, buf, slot, r, sem).wait()
        return c
    lax.fori_loop(0, TR, wait, 0)
    xs_ref[...] = _bf(buf[slot])


def _dispatch(pos0, pos1, h2f, n_rows):
    T = h2f.shape[0]
    return pl.pallas_call(
        functools.partial(_dispatch_kernel, n_tok=T, n_rows=n_rows),
        grid_spec=pltpu.PrefetchScalarGridSpec(
            num_scalar_prefetch=2, grid=(n_rows // TR,),
            in_specs=[pl.BlockSpec(memory_space=pl.ANY)],
            out_specs=pl.BlockSpec((TR, D_MODEL), lambda i, p0, p1: (i, 0)),
            scratch_shapes=[pltpu.SMEM((n_rows,), I32),
                            pltpu.VMEM((2, TR, D_MODEL), F32),
                            pltpu.SemaphoreType.DMA((2,))]),
        out_shape=jax.ShapeDtypeStruct((n_rows, D_MODEL), BF16),
        compiler_params=_cparams(("arbitrary",)),
        name="moe_dispatch",
    )(pos0, pos1, h2f)


def _combine_kernel(pos0_ref, pos1_ref, y_hbm, x1_ref, rw_ref, mod_ref, o_ref, buf0, buf1, sem0, sem1):
    i = pl.program_id(0)
    nt = pl.num_programs(0)

    def issue(tile, slot):
        def body(r, c):
            t = tile * TR + r
            _row_copy(y_hbm, pos0_ref[t], buf0, slot, r, sem0).start()
            _row_copy(y_hbm, pos1_ref[t], buf1, slot, r, sem1).start()
            return c
        lax.fori_loop(0, TR, body, 0)

    @pl.when(i == 0)
    def _():
        issue(0, 0)

    @pl.when(i + 1 < nt)
    def _():
        issue(i + 1, (i + 1) % 2)

    slot = i % 2

    def wait(r, c):
        _row_copy(y_hbm, 0, buf0, slot, r, sem0).wait()
        _row_copy(y_hbm, 0, buf1, slot, r, sem1).wait()
        return c
    lax.fori_loop(0, TR, wait, 0)
    rw = rw_ref[...]
    f = rw[:, 0:1] * buf0[slot] + rw[:, 1:2] * buf1[slot]
    o_ref[...] = x1_ref[...] + mod_ref[:, 5 * D_MODEL:6 * D_MODEL] * f


def _combine(pos0, pos1, ys, x1, rw, mod3, t_ctx, dec_seq):
    T = x1.shape[0]
    row = functools.partial(_mod_row, tm=TR, t_ctx=t_ctx, dec_seq=dec_seq)
    return pl.pallas_call(
        _combine_kernel,
        grid_spec=pltpu.PrefetchScalarGridSpec(
            num_scalar_prefetch=2, grid=(T // TR,),
            in_specs=[pl.BlockSpec(memory_space=pl.ANY),
                      pl.BlockSpec((TR, D_MODEL), lambda i, p0, p1: (i, 0)),
                      pl.BlockSpec((TR, N_EXP_PAD), lambda i, p0, p1: (i, 0)),
                      pl.BlockSpec((None, 1, 6 * D_MODEL), lambda i, p0, p1: (row(i), 0, 0))],
            out_specs=pl.BlockSpec((TR, D_MODEL), lambda i, p0, p1: (i, 0)),
            scratch_shapes=[pltpu.VMEM((2, TR, D_MODEL), F32), pltpu.VMEM((2, TR, D_MODEL), F32),
                            pltpu.SemaphoreType.DMA((2,)), pltpu.SemaphoreType.DMA((2,))]),
        out_shape=jax.ShapeDtypeStruct((T, D_MODEL), F32),
        compiler_params=_cparams(("arbitrary",)),
        name="moe_combine",
    )(pos0, pos1, ys, x1, rw, mod3)


def _block_ones(n, seg):
    idx = np.arange(n) // seg
    return jnp.asarray((idx[:, None] == idx[None, :]).astype(np.float32), dtype=BF16)


def _rope_tables(seq, head_w, rot_off, rot_dim, n_heads):
    rows = seq // GRID_W
    nf = rot_dim // 4
    freqs = ROPE_THETA ** (-np.arange(nf, dtype=np.float64) / nf)
    row = np.repeat(np.arange(rows, dtype=np.float64), GRID_W)
    col = np.tile(np.arange(GRID_W, dtype=np.float64), rows)
    ang = np.concatenate([row[:, None] * freqs, col[:, None] * freqs], axis=-1)
    ang = np.repeat(ang, 2, axis=-1)
    even = (np.arange(rot_dim) % 2 == 0)[None, :]
    c = np.ones((seq, head_w))
    se = np.zeros((seq, head_w))
    so = np.zeros((seq, head_w))
    c[:, rot_off:rot_off + rot_dim] = np.cos(ang)
    se[:, rot_off:rot_off + rot_dim] = np.where(even, -np.sin(ang), 0.0)
    so[:, rot_off:rot_off + rot_dim] = np.where(even, 0.0, np.sin(ang))
    return tuple(jnp.asarray(np.tile(t, (1, n_heads)), dtype=F32) for t in (c, se, so))


def _pool_mask():
    m = np.zeros((2 * D_PAD, D_CH), np.float32)
    for gi, w in enumerate(POOL):
        left = w // 2
        right = w - 1 - left
        for off in range(-left, right + 1):
            m[off + D_PAD, gi * D_G:(gi + 1) * D_G] = 1.0
    return jnp.asarray(m)


def _pad_heads(w, real, pad):
    lead = w.shape[:-1]
    h = w.shape[-1] // real
    w = w.reshape(lead + (h, real))
    w = jnp.pad(w, [(0, 0)] * len(lead) + [(0, 0), (0, pad - real)])
    return w.reshape(lead + (h * pad,))


def kernel(x_prompt, x_sample, cache_a_k, cache_a_v, cache_b_ckv, cache_b_kpe, c, c_ctx, ada_w, ada_b, norm1_g, norm2_g, w_in, a_q_norm, a_k_norm, a_lambda, a_sub_norm, a_w_o, b_q_lora_norm, b_kv_lora_norm, b_w_uq, b_w_ukv, b_q_norm, b_k_norm, b_w_o, c_dw, c_dw_b, c_ln_g, c_ln_b, c_w_o, d_w_group, d_scale, d_w_o, w_out, ffn_w1, ffn_w3, ffn_w2, moe_router, moe_w1, moe_w3, moe_w2):
    nb, seq, _ = x_prompt.shape
    db, dseq, _ = x_sample.shape
    L = w_in.shape[0]
    past = cache_a_k.shape[2]
    t_ctx = nb * seq
    t_lat = db * dseq
    T = t_ctx + t_lat
    na = 2 * A_HEADS * A_QK
    assert t_ctx % TG == 0 and dseq % TG == 0 and seq % TR == 0 and db + 1 <= 16

    x = jnp.concatenate([x_prompt.reshape(t_ctx, D_MODEL), x_sample.reshape(t_lat, D_MODEL)], axis=0)
    cv = jnp.concatenate([c_ctx[None, :], c, jnp.zeros((15 - db, D_MODEL), F32)], axis=0)
    mod = _ada_all(cv, ada_w, ada_b)

    bd_a = _block_ones(na, A_QK)
    bd_b = _block_ones(B_HEADS * B_HEAD_PAD, B_HEAD_PAD)
    rope_a = _rope_tables(dseq, A_QK, 0, A_QK, 2 * A_HEADS)
    rope_b = _rope_tables(dseq, B_HEAD_PAD, B_NOPE, B_ROPE, B_HEADS)
    pmask = _pool_mask()
    tri = jnp.asarray(np.tril(np.ones((TR, TR), np.float32), -1), dtype=BF16)

    row1 = lambda a: a.reshape(L, 1, -1)
    w_in_b = w_in.astype(BF16)
    w_a = w_in_b[:, :, O_AQ:O_BQ]
    w_b = jnp.concatenate([w_in_b[:, :, O_BQ:O_KPE], jnp.zeros((L, D_MODEL, B_NOPE), BF16),
                           w_in_b[:, :, O_KPE:O_C],
                           jnp.zeros((L, D_MODEL, B_HEAD_PAD - B_NOPE - B_ROPE), BF16)], axis=2)
    w_cd = w_in_b[:, :, O_C:O_G]
    w_g = w_in_b[:, :, O_G:O_END]
    gq_a = row1(jnp.tile(a_q_norm, (1, 2 * A_HEADS)))
    gk_a = row1(jnp.tile(a_k_norm, (1, 2 * A_HEADS)))
    wuq = _pad_heads(b_w_uq, B_QK, B_HEAD_PAD).astype(BF16)
    ukv = b_w_ukv.reshape(L, B_KVL, B_HEADS, B_NOPE + B_V)
    wuk = _pad_heads(ukv[..., :B_NOPE].reshape(L, B_KVL, B_HEADS * B_NOPE), B_NOPE, B_HEAD_PAD).astype(BF16)
    wuv = ukv[..., B_NOPE:].reshape(L, B_KVL, B_HEADS * B_V).astype(BF16)
    gq_b = row1(jnp.tile(jnp.pad(b_q_norm, ((0, 0), (0, B_HEAD_PAD - B_QK))), (1, B_HEADS)))
    gk_b = row1(jnp.tile(jnp.pad(b_k_norm, ((0, 0), (0, B_HEAD_PAD - B_QK))), (1, B_HEADS)))
    bdd = jnp.zeros((L, D_CH, D_CH), F32)
    for gi in range(len(POOL)):
        bdd = bdd.at[:, gi * D_G:(gi + 1) * D_G, gi * D_G:(gi + 1) * D_G].set(d_w_group[:, gi])
    bdd = bdd.astype(BF16)
    a_wo, b_wo, c_wo, d_wo, wo = (w.astype(BF16) for w in (a_w_o, b_w_o, c_w_o, d_w_o, w_out))
    ffn1, ffn3, ffn2 = (w.astype(BF16) for w in (ffn_w1, ffn_w3, ffn_w2))
    r_pad = jnp.pad(moe_router, ((0, 0), (0, 0), (0, N_EXP_PAD - N_EXP)))
    r_hi = r_pad.astype(BF16)
    r_lo = (r_pad - r_hi.astype(F32)).astype(BF16)

    ck_a = cache_a_k.reshape(db, L, past, na).astype(BF16)
    cv_a = cache_a_v.reshape(db, L, past, A_HEADS * A_V).astype(BF16)
    kpe_pad = jnp.pad(cache_b_kpe, ((0, 0), (0, 0), (0, 0), (B_NOPE, B_HEAD_PAD - B_NOPE - B_ROPE)))
    ck_b, cv_b = _cache_b(cache_b_ckv, kpe_pad, wuk, wuv, gk_b, bd_b)

    n_rows = 2 * T + N_EXP * TG
    nt_g = n_rows // TG
    dense_te = jnp.zeros((T // TG,), I32)
    dense_na = jnp.full((1,), T // TG, I32)

    new_ak, new_av, new_ckv, new_kpe = [], [], [], []
    for l in range(L):
        mod3 = mod[l].reshape(16, 1, 6 * D_MODEL)
        h1 = _prep(x, mod3, norm1_g[l][None, :], t_ctx, dseq)

        q1c, q2c, kc, vc, nk, nv = _proj_a(h1, w_a[l], gq_a[l], gk_a[l], bd_a, None, 0, t_ctx, False)
        q1l, q2l, kl, vl = _proj_a(h1, w_a[l], gq_a[l], gk_a[l], bd_a, rope_a, t_ctx, t_lat, True)
        g_sub = a_sub_norm[l][None, :]
        oa = jnp.concatenate([
            _attn_a(q1c, q2c, kc, vc, None, None, a_lambda[l], g_sub, l, seq, False),
            _attn_a(q1l, q2l, kl, vl, ck_a, cv_a, a_lambda[l], g_sub, l, dseq, True)], axis=0)
        new_ak.append(nk)
        new_av.append(nv)

        bargs = (w_b[l], b_q_lora_norm[l][None, :], b_kv_lora_norm[l][None, :], wuq[l], wuk[l], wuv[l],
                 gq_b[l], gk_b[l], bd_b)
        qc, kc, vc, nckv, nkpe = _proj_b(h1, *bargs, None, 0, t_ctx, False)
        ql, kl, vl = _proj_b(h1, *bargs, rope_b, t_ctx, t_lat, True)
        ob = jnp.concatenate([
            _attn_b(qc, kc, vc, None, None, l, seq, False),
            _attn_b(ql, kl, vl, ck_b, cv_b, l, dseq, True)], axis=0)
        new_ckv.append(nckv)
        new_kpe.append(nkpe[:, B_NOPE:B_NOPE + B_ROPE])

        cdargs = (w_cd[l], c_dw[l], c_dw_b[l][None, :], c_ln_g[l][None, :], c_ln_b[l][None, :],
                  bdd[l], d_scale[l][None, :], pmask)
        occ, odc = _mix_cd(h1, *cdargs, 0, t_ctx, seq)
        ocl, odl = _mix_cd(h1, *cdargs, t_ctx, t_lat, dseq)
        oc = jnp.concatenate([occ, ocl], axis=0)
        od = jnp.concatenate([odc, odl], axis=0)

        j = l // 2
        moe = l % 2 == 1
        router = (r_hi[j], r_lo[j]) if moe else None
        outs = _merge(x, h1, oa, ob, oc, od, mod3, norm2_g[l][None, :], w_g[l], a_wo[l], b_wo[l],
                      c_wo[l], d_wo[l], wo[l], router, t_ctx, dseq)
        if not moe:
            x1, h2 = outs
            x = _ffn(h2, ffn1[j:j + 1], ffn3[j:j + 1], ffn2[j:j + 1], dense_te, dense_na, 1408,
                     resid=(x1, mod3, t_ctx, dseq))
        else:
            x1, h2, h2f, ridx, rw, sel = outs
            rank, cnt = _rank(sel, tri)
            counts = cnt[0, :N_EXP].astype(I32)
            padded = ((counts + TG - 1) // TG) * TG
            ends = jnp.cumsum(padded)
            offs = ends - padded
            pos_all = offs[None, :] + rank[:, :N_EXP].astype(I32)
            pos0 = jnp.take_along_axis(pos_all, ridx[:, 0:1], axis=1)[:, 0]
            pos1 = jnp.take_along_axis(pos_all, ridx[:, 1:2], axis=1)[:, 0]
            tile_start = jnp.arange(nt_g, dtype=I32) * TG
            tile_e = jnp.minimum(jnp.sum(tile_start[:, None] >= ends[None, :], axis=1), N_EXP - 1).astype(I32)
            n_active = (ends[-1] // TG).astype(I32).reshape(1)
            last_e = tile_e[jnp.maximum(n_active[0] - 1, 0)]
            tile_e = jnp.where(tile_start < ends[-1], tile_e, last_e)
            xs = _dispatch(pos0, pos1, h2f, n_rows)
            ys = _ffn(xs, moe_w1[j], moe_w3[j], moe_w2[j], tile_e, n_active, 512)
            x = _combine(pos0, pos1, ys, x1, rw, mod3, t_ctx, dseq)

    y_prompt = x[:t_ctx].reshape(nb, seq, D_MODEL)
    y_sample = x[t_ctx:].reshape(db, dseq, D_MODEL)
    new_a_k = jnp.stack(new_ak, axis=0).reshape(L, nb, seq, 2 * A_HEADS, A_QK).transpose(1, 0, 2, 3, 4)
    new_a_v = jnp.stack(new_av, axis=0).reshape(L, nb, seq, A_HEADS, A_V).transpose(1, 0, 2, 3, 4)
    new_b_ckv = jnp.stack(new_ckv, axis=0).reshape(L, nb, seq, B_KVL).transpose(1, 0, 2, 3)
    new_b_kpe = jnp.stack(new_kpe, axis=0).reshape(L, nb, seq, B_ROPE).transpose(1, 0, 2, 3)
    return (y_prompt, y_sample, new_a_k, new_a_v, new_b_ckv, new_b_kpe)
```

```python
import functools
import math

import numpy as np
import jax
import jax.numpy as jnp
from jax import lax
from jax.experimental import pallas as pl
from jax.experimental.pallas import tpu as pltpu

F32 = jnp.float32
BF16 = jnp.bfloat16
I32 = jnp.int32

EPS = 1e-6
D_MODEL = 1024
GRID_W = 64
ROPE_THETA = 10000.0
A_HEADS = 4
A_QK = 64
A_V = 128
B_HEADS = 4
B_NOPE = 64
B_ROPE = 32
B_QK = B_NOPE + B_ROPE
B_V = 64
B_QL = 256
B_KVL = 128
B_HEAD_PAD = 128
C_CH = 256
C_W = 31
C_PAD = 16
POOL = (2, 4, 8, 16)
D_G = 64
D_CH = D_G * len(POOL)
D_PAD = 8
N_EXP = 8
N_EXP_PAD = 128
LANES = 128
SUB = 8

O_AQ, O_AK, O_AV, O_BQ, O_BKV, O_KPE, O_C, O_D, O_G, O_END = (
    0, 512, 1024, 1536, 1792, 1920, 1952, 2464, 2720, 6816)
W_B_COLS = (O_KPE - O_BQ) + LANES

TM = 512
TQ = 256
TG = 1024
TR = 256
TF_DENSE = 1408
TF_MOE = 512
VMEM_LIMIT = 56 * 1024 * 1024


def _cparams(sem):
    return pltpu.CompilerParams(dimension_semantics=sem, vmem_limit_bytes=VMEM_LIMIT)


def _dot(a, b):
    return jnp.dot(a, b, preferred_element_type=F32)


def _dot_nt(a, b):
    return lax.dot_general(a, b, (((1,), (1,)), ((), ())), preferred_element_type=F32)


def _bf(x):
    return x.astype(BF16)


def _rms(x, g):
    return x * lax.rsqrt(jnp.mean(x * x, axis=-1, keepdims=True) + EPS) * g


def _seg_sum_sq(x, bd_ref):
    xx = x * x
    hi = _bf(xx)
    lo = _bf(xx - hi.astype(F32))
    bd = bd_ref[...]
    return _dot(hi, bd) + _dot(lo, bd)


def _rope(x, c_ref, se_ref, so_ref):
    n = x.shape[-1]
    return (x * c_ref[...] + pltpu.roll(x, n - 1, 1) * se_ref[...]
            + pltpu.roll(x, 1, 1) * so_ref[...])


def _mod_row(i, tm, t_ctx, dec_seq):
    r = i * tm
    return jnp.where(r < t_ctx, 0, 1 + (r - t_ctx) // dec_seq)


def _layer(a, l):
    nd = a.ndim
    return pl.BlockSpec((None,) + a.shape[1:], lambda *_: (l,) + (0,) * (nd - 1))


def _whole(a):
    nd = a.ndim
    return pl.BlockSpec(a.shape, lambda *_: (0,) * nd)


def _ada_kernel(cv_ref, w_ref, b_ref, o_ref):
    cv = cv_ref[...]
    s = cv * jax.nn.sigmoid(cv)
    o_ref[...] = _dot(_bf(s), _bf(w_ref[...])) + b_ref[...]


def _ada_all(cv, ada_w, ada_b):
    L, d, n = ada_w.shape
    tn = 1536
    return pl.pallas_call(
        _ada_kernel,
        grid=(L, n // tn),
        in_specs=[pl.BlockSpec((16, d), lambda l, j: (0, 0)),
                  pl.BlockSpec((None, d, tn), lambda l, j: (l, 0, j)),
                  pl.BlockSpec((None, 1, tn), lambda l, j: (l, 0, j))],
        out_specs=pl.BlockSpec((None, 16, tn), lambda l, j: (l, 0, j)),
        out_shape=jax.ShapeDtypeStruct((L, 16, n), F32),
        compiler_params=_cparams(("parallel", "parallel")),
        name="ada_mod",
    )(cv, ada_w, ada_b.reshape(L, 1, n))


def _repack_kernel(w_ref, wa_ref, wb_ref, wcd_ref, wg_ref):
    wa_ref[...] = _bf(w_ref[:, O_AQ:O_BQ])
    wb_ref[:, 0:O_KPE - O_BQ] = _bf(w_ref[:, O_BQ:O_KPE])
    blk = w_ref[:, O_KPE:O_KPE + LANES]
    lane = lax.broadcasted_iota(I32, blk.shape, 1)
    wb_ref[:, O_KPE - O_BQ:W_B_COLS] = _bf(jnp.where(lane < B_ROPE, blk, 0.0))
    wcd_ref[...] = _bf(w_ref[:, O_C:O_G])
    wg_ref[...] = _bf(w_ref[:, O_G:O_END])


def _repack(w_in):
    L, d, n = w_in.shape
    tr = 256
    widths = (O_BQ - O_AQ, W_B_COLS, O_G - O_C, O_END - O_G)
    return pl.pallas_call(
        _repack_kernel,
        grid=(L, d // tr),
        in_specs=[pl.BlockSpec((None, tr, n), lambda l, i: (l, i, 0))],
        out_specs=[pl.BlockSpec((None, tr, w), lambda l, i: (l, i, 0)) for w in widths],
        out_shape=[jax.ShapeDtypeStruct((L, d, w), BF16) for w in widths],
        compiler_params=_cparams(("parallel", "parallel")),
        name="repack_w_in",
    )(w_in)


def _prep_kernel(x_ref, mod_ref, g_ref, h_ref):
    y = _rms(x_ref[...], g_ref[...])
    sh = mod_ref[:, 0:D_MODEL]
    sc = mod_ref[:, D_MODEL:2 * D_MODEL]
    h_ref[...] = _bf(y * (1.0 + sc) + sh)


def _prep(x, mod4, g, l, t_ctx, dec_seq):
    T = x.shape[0]
    row = functools.partial(_mod_row, tm=TM, t_ctx=t_ctx, dec_seq=dec_seq)
    return pl.pallas_call(
        _prep_kernel,
        grid=(T // TM,),
        in_specs=[pl.BlockSpec((TM, D_MODEL), lambda i: (i, 0)),
                  pl.BlockSpec((None, None, 1, 6 * D_MODEL), lambda i: (l, row(i), 0, 0)),
                  _layer(g, l)],
        out_specs=pl.BlockSpec((TM, D_MODEL), lambda i: (i, 0)),
        out_shape=jax.ShapeDtypeStruct((T, D_MODEL), BF16),
        compiler_params=_cparams(("parallel",)),
        name="prep",
    )(x, mod4, g)


def _proj_a_kernel(*refs, latent):
    if latent:
        (h_ref, w_ref, gq_ref, gk_ref, bd_ref, c_ref, se_ref, so_ref,
         q1_ref, q2_ref, k_ref, v_ref) = refs
    else:
        (h_ref, w_ref, gq_ref, gk_ref, bd_ref,
         q1_ref, q2_ref, k_ref, v_ref, nk_ref, nv_ref) = refs
    p = _dot(h_ref[...], w_ref[...])
    n = 2 * A_HEADS * A_QK
    q = p[:, 0:n]
    k = p[:, n:2 * n]
    v = p[:, 2 * n:3 * n]
    q = q * lax.rsqrt(_seg_sum_sq(q, bd_ref) * (1.0 / A_QK) + EPS) * gq_ref[...]
    k = k * lax.rsqrt(_seg_sum_sq(k, bd_ref) * (1.0 / A_QK) + EPS) * gk_ref[...]
    if latent:
        q = _rope(q, c_ref, se_ref, so_ref)
        k = _rope(k, c_ref, se_ref, so_ref)
    else:
        nk_ref[...] = k
        nv_ref[...] = v
    q = q * (A_QK ** -0.5)
    lane = lax.broadcasted_iota(I32, q.shape, 1)
    first = (lane % (2 * A_QK)) < A_QK
    q1_ref[...] = _bf(jnp.where(first, q, 0.0))
    q2_ref[...] = _bf(jnp.where(first, 0.0, q))
    k_ref[...] = _bf(k)
    v_ref[...] = _bf(v)


def _proj_a(h1, w_a, gq, gk, bd, rope_tabs, l, row0, rows, latent):
    n = 2 * A_HEADS * A_QK
    b0 = row0 // TM
    tab_blocks = rope_tabs[0].shape[0] // TM if latent else 1
    in_specs = [pl.BlockSpec((TM, D_MODEL), lambda i: (i + b0, 0)),
                _layer(w_a, l), _layer(gq, l), _layer(gk, l), _whole(bd)]
    args = [h1, w_a, gq, gk, bd]
    if latent:
        in_specs += [pl.BlockSpec((TM, n), lambda i: (i % tab_blocks, 0))] * 3
        args += list(rope_tabs)
    out_spec = pl.BlockSpec((TM, n), lambda i: (i, 0))
    out_shape = [jax.ShapeDtypeStruct((rows, n), BF16)] * 4
    out_specs = [out_spec] * 4
    if not latent:
        out_shape += [jax.ShapeDtypeStruct((rows, n), F32)] * 2
        out_specs += [out_spec] * 2
    return pl.pallas_call(
        functools.partial(_proj_a_kernel, latent=latent),
        grid=(rows // TM,),
        in_specs=in_specs, out_specs=out_specs, out_shape=out_shape,
        compiler_params=_cparams(("parallel",)),
        name="proj_a_lat" if latent else "proj_a_ctx",
    )(*args)


def _softmax_parts(scores):
    m = functools.reduce(jnp.maximum, [jnp.max(s, axis=-1, keepdims=True) for s in scores])
    es = [jnp.exp(s - m) for s in scores]
    l = functools.reduce(lambda a, b: a + b, [jnp.sum(e, axis=-1, keepdims=True) for e in es])
    return es, 1.0 / l


def _attn_a_kernel(*refs, latent, lam_init):
    if latent:
        q1_ref, q2_ref, k_ref, v_ref, kc_ref, vc_ref, lam_ref, g_ref, o_ref = refs
    else:
        q1_ref, q2_ref, k_ref, v_ref, lam_ref, g_ref, o_ref = refs
    lm = lam_ref[...]
    lam = (jnp.exp(jnp.sum(lm[0:1] * lm[1:2], axis=-1, keepdims=True))
           - jnp.exp(jnp.sum(lm[2:3] * lm[3:4], axis=-1, keepdims=True)) + lam_init)
    for h in range(A_HEADS):
        sl = slice(A_V * h, A_V * (h + 1))
        ks = [k_ref[:, sl]]
        vs = [v_ref[:, sl]]
        if latent:
            ks.append(kc_ref[:, sl])
            vs.append(vc_ref[:, sl])
        q1 = q1_ref[:, sl]
        q2 = q2_ref[:, sl]
        e1, r1 = _softmax_parts([_dot_nt(q1, k) for k in ks])
        e2, r2 = _softmax_parts([_dot_nt(q2, k) for k in ks])
        r2 = r2 * lam
        o = None
        for a, b, v in zip(e1, e2, vs):
            t = _dot(_bf(a * r1 - b * r2), v)
            o = t if o is None else o + t
        o = _rms(o, g_ref[...]) * (1.0 - lam_init)
        o_ref[:, sl] = _bf(o)


def _attn_a(q1, q2, k, v, cache_k, cache_v, a_lambda, g_sub, layer, seq, latent):
    rows, n = q1.shape
    nb = rows // seq
    lam_init = 0.8 - 0.6 * math.exp(-0.3 * layer)
    kern = functools.partial(_attn_a_kernel, latent=latent, lam_init=lam_init)
    if latent:
        nq = seq // TQ
        past = cache_k.shape[2]
        grid = (nb, nq)
        qs = pl.BlockSpec((TQ, n), lambda b, j: (b * nq + j, 0))
        kv = pl.BlockSpec((seq, n), lambda b, j: (b, 0))
        cs = pl.BlockSpec((None, None, past, n), lambda b, j: (b, layer, 0, 0))
        in_specs = [qs, qs, kv, kv, cs, cs, _layer(a_lambda, layer), _layer(g_sub, layer)]
        args = (q1, q2, k, v, cache_k, cache_v, a_lambda, g_sub)
        sem = ("parallel", "parallel")
        out_spec = qs
    else:
        grid = (nb,)
        bs = pl.BlockSpec((seq, n), lambda b: (b, 0))
        in_specs = [bs, bs, bs, bs, _layer(a_lambda, layer), _layer(g_sub, layer)]
        args = (q1, q2, k, v, a_lambda, g_sub)
        sem = ("parallel",)
        out_spec = bs
    return pl.pallas_call(
        kern, grid=grid, in_specs=in_specs, out_specs=out_spec,
        out_shape=jax.ShapeDtypeStruct((rows, n), BF16),
        compiler_params=_cparams(sem),
        name="attn_a_lat" if latent else "attn_a_ctx",
    )(*args)


def _mla_keys(ckv, kpe, wuk_ref, wuv_ref, gk_ref, bd_ref):
    cb = _bf(ckv)
    kn = _dot(cb, wuk_ref[...]) + jnp.concatenate([kpe] * B_HEADS, axis=1)
    k = kn * lax.rsqrt(_seg_sum_sq(kn, bd_ref) * (1.0 / B_QK) + EPS) * gk_ref[...]
    v = _dot(cb, wuv_ref[...])
    return k, v


def _proj_b_kernel(*refs, latent):
    if latent:
        (h_ref, w_ref, gql_ref, gkvl_ref, wuq_ref, wuk_ref, wuv_ref, gq_ref, gk_ref, bd_ref,
         c_ref, se_ref, so_ref, q_ref, k_ref, v_ref) = refs
    else:
        (h_ref, w_ref, gql_ref, gkvl_ref, wuq_ref, wuk_ref, wuv_ref, gq_ref, gk_ref, bd_ref,
         q_ref, k_ref, v_ref, nckv_ref, nkpe_ref) = refs
    p = _dot(h_ref[...], w_ref[...])
    bq = p[:, 0:B_QL]
    bkv = p[:, B_QL:B_QL + B_KVL]
    kpe = p[:, B_QL + B_KVL:B_QL + B_KVL + LANES]
    q = _dot(_bf(_rms(bq, gql_ref[...])), wuq_ref[...])
    q = q * lax.rsqrt(_seg_sum_sq(q, bd_ref) * (1.0 / B_QK) + EPS) * gq_ref[...]
    ckv = _rms(bkv, gkvl_ref[...])
    k, v = _mla_keys(ckv, kpe, wuk_ref, wuv_ref, gk_ref, bd_ref)
    if latent:
        q = _rope(q, c_ref, se_ref, so_ref)
        k = _rope(k, c_ref, se_ref, so_ref)
    else:
        nckv_ref[...] = ckv
        nkpe_ref[...] = kpe
    q_ref[...] = _bf(q * (B_QK ** -0.5))
    k_ref[...] = _bf(k)
    v_ref[...] = _bf(v)


def _proj_b(h1, w_b, gql, gkvl, wuq, wuk, wuv, gq, gk, bd, rope_tabs, l, row0, rows, latent):
    n = B_HEADS * B_HEAD_PAD
    nv = B_HEADS * B_V
    b0 = row0 // TM
    tab_blocks = rope_tabs[0].shape[0] // TM if latent else 1
    in_specs = [pl.BlockSpec((TM, D_MODEL), lambda i: (i + b0, 0))] + [
        _layer(a, l) for a in (w_b, gql, gkvl, wuq, wuk, wuv, gq, gk)] + [_whole(bd)]
    args = [h1, w_b, gql, gkvl, wuq, wuk, wuv, gq, gk, bd]
    if latent:
        in_specs += [pl.BlockSpec((TM, n), lambda i: (i % tab_blocks, 0))] * 3
        args += list(rope_tabs)
    row = lambda w: pl.BlockSpec((TM, w), lambda i: (i, 0))
    out_shape = [jax.ShapeDtypeStruct((rows, n), BF16), jax.ShapeDtypeStruct((rows, n), BF16),
                 jax.ShapeDtypeStruct((rows, nv), BF16)]
    out_specs = [row(n), row(n), row(nv)]
    if not latent:
        out_shape += [jax.ShapeDtypeStruct((rows, B_KVL), F32),
                      jax.ShapeDtypeStruct((rows, LANES), F32)]
        out_specs += [row(B_KVL), row(LANES)]
    return pl.pallas_call(
        functools.partial(_proj_b_kernel, latent=latent),
        grid=(rows // TM,),
        in_specs=in_specs, out_specs=out_specs, out_shape=out_shape,
        compiler_params=_cparams(("parallel",)),
        name="proj_b_lat" if latent else "proj_b_ctx",
    )(*args)


def _cache_b_kernel(ckv_ref, kpe_ref, wuk_ref, wuv_ref, gk_ref, bd_ref, k_ref, v_ref):
    k, v = _mla_keys(ckv_ref[...], kpe_ref[...], wuk_ref, wuv_ref, gk_ref, bd_ref)
    k_ref[...] = _bf(k)
    v_ref[...] = _bf(v)


def _cache_b(ckv, kpe_pad, wuk, wuv, gk, bd):
    db, L, past, _ = ckv.shape
    n = B_HEADS * B_HEAD_PAD
    nv = B_HEADS * B_V
    blk = lambda w: pl.BlockSpec((None, None, past, w), lambda l, b: (b, l, 0, 0))
    wl = lambda a: pl.BlockSpec((None,) + a.shape[1:], lambda l, b: (l, 0, 0))
    return pl.pallas_call(
        _cache_b_kernel,
        grid=(L, db),
        in_specs=[blk(B_KVL), blk(LANES), wl(wuk), wl(wuv), wl(gk),
                  pl.BlockSpec(bd.shape, lambda l, b: (0, 0))],
        out_specs=[blk(n), blk(nv)],
        out_shape=[jax.ShapeDtypeStruct((db, L, past, n), BF16),
                   jax.ShapeDtypeStruct((db, L, past, nv), BF16)],
        compiler_params=_cparams(("parallel", "parallel")),
        name="cache_b_expand",
    )(ckv, kpe_pad, wuk, wuv, gk, bd)


def _attn_b_kernel(*refs, latent):
    if latent:
        q_ref, k_ref, v_ref, kc_ref, vc_ref, o_ref = refs
    else:
        q_ref, k_ref, v_ref, o_ref = refs
    lane = lax.broadcasted_iota(I32, (q_ref.shape[0], 2 * B_V), 1)
    for hp in range(B_HEADS // 2):
        vsl = slice(2 * B_V * hp, 2 * B_V * (hp + 1))
        vs = [v_ref[:, vsl]] + ([vc_ref[:, vsl]] if latent else [])
        outs = []
        for h in (2 * hp, 2 * hp + 1):
            sl = slice(B_HEAD_PAD * h, B_HEAD_PAD * (h + 1))
            ks = [k_ref[:, sl]] + ([kc_ref[:, sl]] if latent else [])
            q = q_ref[:, sl]
            es, r = _softmax_parts([_dot_nt(q, k) for k in ks])
            o = None
            for e, v in zip(es, vs):
                t = _dot(_bf(e * r), v)
                o = t if o is None else o + t
            outs.append(o)
        o_ref[:, vsl] = _bf(jnp.where(lane < B_V, outs[0], outs[1]))


def _attn_b(q, k, v, cache_k, cache_v, layer, seq, latent):
    rows, n = q.shape
    nv = v.shape[1]
    nb = rows // seq
    kern = functools.partial(_attn_b_kernel, latent=latent)
    if latent:
        nq = seq // TQ
        past = cache_k.shape[2]
        grid = (nb, nq)
        in_specs = [pl.BlockSpec((TQ, n), lambda b, j: (b * nq + j, 0)),
                    pl.BlockSpec((seq, n), lambda b, j: (b, 0)),
                    pl.BlockSpec((seq, nv), lambda b, j: (b, 0)),
                    pl.BlockSpec((None, None, past, n), lambda b, j: (b, layer, 0, 0)),
                    pl.BlockSpec((None, None, past, nv), lambda b, j: (b, layer, 0, 0))]
        args = (q, k, v, cache_k, cache_v)
        out_spec = pl.BlockSpec((TQ, nv), lambda b, j: (b * nq + j, 0))
        sem = ("parallel", "parallel")
    else:
        grid = (nb,)
        in_specs = [pl.BlockSpec((seq, n), lambda b: (b, 0)),
                    pl.BlockSpec((seq, n), lambda b: (b, 0)),
                    pl.BlockSpec((seq, nv), lambda b: (b, 0))]
        args = (q, k, v)
        out_spec = pl.BlockSpec((seq, nv), lambda b: (b, 0))
        sem = ("parallel",)
    return pl.pallas_call(
        kern, grid=grid, in_specs=in_specs, out_specs=out_spec,
        out_shape=jax.ShapeDtypeStruct((rows, nv), BF16),
        compiler_params=_cparams(sem),
        name="attn_b_lat" if latent else "attn_b_ctx",
    )(*args)


CONV_CHUNK = 64


def _mix_cd_kernel(h_ref, w_ref, dw_ref, dwb_ref, lng_ref, lnb_ref, bdd_ref, dsc_ref, pm_ref,
                   oc_ref, od_ref, gpad, dpad, *, seq):
    p = _dot(h_ref[...], w_ref[...])
    glu = p[:, 0:C_CH] * jax.nn.sigmoid(p[:, C_CH:2 * C_CH])
    gpad[0:C_PAD, :] = jnp.zeros((C_PAD, C_CH), F32)
    gpad[C_PAD + seq:2 * C_PAD + seq, :] = jnp.zeros((C_PAD, C_CH), F32)
    gpad[C_PAD:C_PAD + seq, :] = glu
    half = C_W // 2
    for c0 in range(0, seq, CONV_CHUNK):
        acc = jnp.zeros((CONV_CHUNK, C_CH), F32) + dwb_ref[...]
        for j in range(C_W):
            s = c0 + C_PAD - half + j
            acc = acc + gpad[s:s + CONV_CHUNK, :] * dw_ref[j:j + 1, :]
        mu = jnp.mean(acc, axis=-1, keepdims=True)
        xc = acc - mu
        y = xc * lax.rsqrt(jnp.mean(xc * xc, axis=-1, keepdims=True) + EPS)
        y = y * lng_ref[...] + lnb_ref[...]
        oc_ref[c0:c0 + CONV_CHUNK, :] = _bf(y * jax.nn.sigmoid(y))
    d = p[:, 2 * C_CH:2 * C_CH + D_CH]
    dpad[0:D_PAD, :] = jnp.zeros((D_PAD, D_CH), F32)
    dpad[D_PAD + seq:2 * D_PAD + seq, :] = jnp.zeros((D_PAD, D_CH), F32)
    dpad[D_PAD:D_PAD + seq, :] = d
    lane = lax.broadcasted_iota(I32, (CONV_CHUNK, D_CH), 1)
    for c0 in range(0, seq, CONV_CHUNK):
        acc = jnp.zeros((CONV_CHUNK, D_CH), F32)
        for j in range(2 * D_PAD):
            acc = acc + dpad[c0 + j:c0 + j + CONV_CHUNK, :] * pm_ref[j:j + 1, :]
        t = c0 + lax.broadcasted_iota(I32, (CONV_CHUNK, D_CH), 0)
        cnt = jnp.zeros((CONV_CHUNK, D_CH), I32)
        for gi, w in enumerate(POOL):
            left = w // 2
            right = w - 1 - left
            c = jnp.minimum(t + right + 1, seq) - jnp.maximum(t - left, 0)
            cnt = jnp.where(lane // D_G == gi, c, cnt)
        pooled = acc / cnt.astype(F32)
        diff = pooled - dpad[c0 + D_PAD:c0 + D_PAD + CONV_CHUNK, :]
        od_ref[c0:c0 + CONV_CHUNK, :] = _bf(_dot(_bf(diff), bdd_ref[...]) * dsc_ref[...])


def _mix_cd(h1, w_cd, dw, dwb, lng, lnb, bdd, dsc, pmask, l, row0, rows, seq):
    b0 = row0 // seq
    return pl.pallas_call(
        functools.partial(_mix_cd_kernel, seq=seq),
        grid=(rows // seq,),
        in_specs=[pl.BlockSpec((seq, D_MODEL), lambda b: (b + b0, 0))] + [
            _layer(a, l) for a in (w_cd, dw, dwb, lng, lnb, bdd, dsc)] + [_whole(pmask)],
        out_specs=[pl.BlockSpec((seq, C_CH), lambda b: (b, 0)),
                   pl.BlockSpec((seq, D_CH), lambda b: (b, 0))],
        out_shape=[jax.ShapeDtypeStruct((rows, C_CH), BF16),
                   jax.ShapeDtypeStruct((rows, D_CH), BF16)],
        scratch_shapes=[pltpu.VMEM((seq + 2 * C_PAD, C_CH), F32),
                        pltpu.VMEM((seq + 2 * D_PAD, D_CH), F32)],
        compiler_params=_cparams(("parallel",)),
        name="mix_cd_%d" % seq,
    )(h1, w_cd, dw, dwb, lng, lnb, bdd, dsc, pmask)


def _merge_kernel(*refs, moe, n_ctx_tiles):
    (x_ref, h_ref, oac_ref, oal_ref, obc_ref, obl_ref, occ_ref, ocl_ref, odc_ref, odl_ref,
     mod_ref, n2g_ref, wg_ref, wa_ref, wb_ref, wc_ref, wd_ref, wo_ref) = refs[:18]
    if moe:
        rhi_ref, rlo_ref, x1_ref, h2t_ref, ridx_ref, rw_ref, sel_ref = refs[18:]
    else:
        x1_ref, h2_ref = refs[18:]
    is_ctx = pl.program_id(0) < n_ctx_tiles
    h = h_ref[...]
    acc = None
    for i, (c_ref, l_ref, w_ref) in enumerate(((oac_ref, oal_ref, wa_ref), (obc_ref, obl_ref, wb_ref),
                                               (occ_ref, ocl_ref, wc_ref), (odc_ref, odl_ref, wd_ref))):
        gate = jax.nn.sigmoid(_dot(h, wg_ref[:, i * D_MODEL:(i + 1) * D_MODEL]))
        o = jnp.where(is_ctx, c_ref[...], l_ref[...])
        t = gate * _dot(o, w_ref[...])
        acc = t if acc is None else acc + t
    y = _dot(_bf(acc), wo_ref[...])
    x1 = x_ref[...] + mod_ref[:, 2 * D_MODEL:3 * D_MODEL] * y
    x1_ref[...] = x1
    h2 = (_rms(x1, n2g_ref[...]) * (1.0 + mod_ref[:, 4 * D_MODEL:5 * D_MODEL])
          + mod_ref[:, 3 * D_MODEL:4 * D_MODEL])
    if not moe:
        h2_ref[...] = _bf(h2)
        return
    tm = h2.shape[0]
    for j in range(SUB):
        h2t_ref[pl.ds(j, tm, stride=SUB), :] = h2[:, LANES * j:LANES * (j + 1)]
    hi = _bf(h2)
    lo = _bf(h2 - hi.astype(F32))
    logits = _dot(hi, rhi_ref[...]) + _dot(lo, rhi_ref[...]) + _dot(hi, rlo_ref[...])
    lane = lax.broadcasted_iota(I32, logits.shape, 1)
    lanef = lane.astype(F32)
    neg = jnp.float32(-jnp.inf)
    lg = jnp.where(lane < N_EXP, logits, neg)
    m0 = jnp.max(lg, axis=-1, keepdims=True)
    i0 = jnp.min(jnp.where(lg == m0, lanef, float(N_EXP_PAD)), axis=-1, keepdims=True)
    sel0 = lanef == i0
    lg1 = jnp.where(sel0, neg, lg)
    m1 = jnp.max(lg1, axis=-1, keepdims=True)
    i1 = jnp.min(jnp.where(lg1 == m1, lanef, float(N_EXP_PAD)), axis=-1, keepdims=True)
    sel1 = lanef == i1
    e = jnp.exp(m1 - m0)
    w0 = 1.0 / (1.0 + e)
    w1 = e / (1.0 + e)
    ridx_ref[...] = jnp.where(lane == 0, i0, jnp.where(lane == 1, i1, 0.0)).astype(I32)
    rw_ref[...] = jnp.where(lane == 0, w0, jnp.where(lane == 1, w1, 0.0))
    sel_ref[...] = jnp.where(sel0 | sel1, 1.0, 0.0).astype(BF16)


def _merge(x, h1, branches, mod4, n2g, wg, wa, wb, wc, wd, wo, router, l, j, t_ctx, dec_seq):
    T = x.shape[0]
    moe = router is not None
    nc = t_ctx // TM
    row = functools.partial(_mod_row, tm=TM, t_ctx=t_ctx, dec_seq=dec_seq)
    rowspec = lambda w: pl.BlockSpec((TM, w), lambda i: (i, 0))
    ctxspec = lambda w: pl.BlockSpec((TM, w), lambda i: (jnp.minimum(i, nc - 1), 0))
    latspec = lambda w: pl.BlockSpec((TM, w), lambda i: (jnp.maximum(i - nc, 0), 0))
    in_specs = [rowspec(D_MODEL), rowspec(D_MODEL)]
    args = [x, h1]
    for oc, ol in branches:
        in_specs += [ctxspec(oc.shape[1]), latspec(ol.shape[1])]
        args += [oc, ol]
    in_specs += [pl.BlockSpec((None, None, 1, 6 * D_MODEL), lambda i: (l, row(i), 0, 0))]
    in_specs += [_layer(a, l) for a in (n2g, wg, wa, wb, wc, wd, wo)]
    args += [mod4, n2g, wg, wa, wb, wc, wd, wo]
    out_shape = [jax.ShapeDtypeStruct((T, D_MODEL), F32)]
    out_specs = [rowspec(D_MODEL)]
    if moe:
        in_specs += [_layer(router[0], j), _layer(router[1], j)]
        args += list(router)
        out_shape += [jax.ShapeDtypeStruct((T * SUB, LANES), F32),
                      jax.ShapeDtypeStruct((T, N_EXP_PAD), I32),
                      jax.ShapeDtypeStruct((T, N_EXP_PAD), F32),
                      jax.ShapeDtypeStruct((T, N_EXP_PAD), BF16)]
        out_specs += [pl.BlockSpec((TM * SUB, LANES), lambda i: (i, 0)),
                      rowspec(N_EXP_PAD), rowspec(N_EXP_PAD), rowspec(N_EXP_PAD)]
    else:
        out_shape += [jax.ShapeDtypeStruct((T, D_MODEL), BF16)]
        out_specs += [rowspec(D_MODEL)]
    return pl.pallas_call(
        functools.partial(_merge_kernel, moe=moe, n_ctx_tiles=nc),
        grid=(T // TM,),
        in_specs=in_specs, out_specs=out_specs, out_shape=out_shape,
        compiler_params=_cparams(("parallel",)),
        name="merge_moe" if moe else "merge",
    )(*args)


def _swiglu_step(x, w1_ref, w3_ref, w2_ref, acc_ref, f):
    a = _dot(x, _bf(w1_ref[...]))
    b = _dot(x, _bf(w3_ref[...]))
    t = _dot(_bf(a * jax.nn.sigmoid(a) * b), _bf(w2_ref[...]))

    @pl.when(f == 0)
    def _():
        acc_ref[...] = t

    @pl.when(f > 0)
    def _():
        acc_ref[...] += t


def _ffn_dense_kernel(x_ref, w1_ref, w3_ref, w2_ref, x1_ref, mod_ref, o_ref, acc_ref):
    f = pl.program_id(1)
    _swiglu_step(x_ref[...], w1_ref, w3_ref, w2_ref, acc_ref, f)

    @pl.when(f == pl.num_programs(1) - 1)
    def _():
        o_ref[...] = x1_ref[...] + mod_ref[:, 5 * D_MODEL:6 * D_MODEL] * acc_ref[...]


def _ffn_dense(h2, w1, w3, w2, x1, mod4, l, j, t_ctx, dec_seq):
    T = h2.shape[0]
    nf = w1.shape[2] // TF_DENSE
    row = functools.partial(_mod_row, tm=TG, t_ctx=t_ctx, dec_seq=dec_seq)
    return pl.pallas_call(
        _ffn_dense_kernel,
        grid=(T // TG, nf),
        in_specs=[pl.BlockSpec((TG, D_MODEL), lambda i, f: (i, 0)),
                  pl.BlockSpec((None, D_MODEL, TF_DENSE), lambda i, f: (j, 0, f)),
                  pl.BlockSpec((None, D_MODEL, TF_DENSE), lambda i, f: (j, 0, f)),
                  pl.BlockSpec((None, TF_DENSE, D_MODEL), lambda i, f: (j, f, 0)),
                  pl.BlockSpec((TG, D_MODEL), lambda i, f: (i, 0)),
                  pl.BlockSpec((None, None, 1, 6 * D_MODEL), lambda i, f: (l, row(i), 0, 0))],
        out_specs=pl.BlockSpec((TG, D_MODEL), lambda i, f: (i, 0)),
        out_shape=jax.ShapeDtypeStruct((T, D_MODEL), F32),
        scratch_shapes=[pltpu.VMEM((TG, D_MODEL), F32)],
        compiler_params=_cparams(("parallel", "arbitrary")),
        name="ffn_dense",
    )(h2, w1, w3, w2, x1, mod4)


def _ffn_experts_kernel(te_ref, na_ref, x_ref, w1_ref, w3_ref, w2_ref, o_ref, xb_ref, acc_ref):
    i = pl.program_id(0)
    f = pl.program_id(1)
    active = i < na_ref[0]

    @pl.when(jnp.logical_and(active, f == 0))
    def _():
        for j in range(SUB):
            xb_ref[:, LANES * j:LANES * (j + 1)] = _bf(x_ref[pl.ds(j, TG, stride=SUB), :])

    @pl.when(active)
    def _():
        _swiglu_step(xb_ref[...], w1_ref, w3_ref, w2_ref, acc_ref, f)

    @pl.when(jnp.logical_and(active, f == pl.num_programs(1) - 1))
    def _():
        for j in range(SUB):
            o_ref[pl.ds(j, TG, stride=SUB), :] = acc_ref[:, LANES * j:LANES * (j + 1)]

    @pl.when(jnp.logical_and(jnp.logical_not(active), f == 0))
    def _():
        o_ref[...] = jnp.zeros(o_ref.shape, F32)


def _ffn_experts(xs, w1, w3, w2, tile_e, n_active, j):
    rows = xs.shape[0] // SUB
    nf = w1.shape[3] // TF_MOE

    def fidx(i, f, na):
        return jnp.where(i < na[0], f, nf - 1)

    def xidx(i, na):
        return jnp.minimum(i, na[0] - 1)

    return pl.pallas_call(
        _ffn_experts_kernel,
        grid_spec=pltpu.PrefetchScalarGridSpec(
            num_scalar_prefetch=2, grid=(rows // TG, nf),
            in_specs=[pl.BlockSpec((TG * SUB, LANES), lambda i, f, te, na: (xidx(i, na), 0)),
                      pl.BlockSpec((None, None, D_MODEL, TF_MOE),
                                   lambda i, f, te, na: (j, te[i], 0, fidx(i, f, na))),
                      pl.BlockSpec((None, None, D_MODEL, TF_MOE),
                                   lambda i, f, te, na: (j, te[i], 0, fidx(i, f, na))),
                      pl.BlockSpec((None, None, TF_MOE, D_MODEL),
                                   lambda i, f, te, na: (j, te[i], fidx(i, f, na), 0))],
            out_specs=pl.BlockSpec((TG * SUB, LANES), lambda i, f, te, na: (i, 0)),
            scratch_shapes=[pltpu.VMEM((TG, D_MODEL), BF16), pltpu.VMEM((TG, D_MODEL), F32)]),
        out_shape=jax.ShapeDtypeStruct((rows * SUB, LANES), F32),
        compiler_params=_cparams(("parallel", "arbitrary")),
        name="ffn_experts",
    )(tile_e, n_active, xs, w1, w3, w2)


def _rank_kernel(sel_ref, tri_ref, rank_ref, cnt_ref, carry):
    i = pl.program_id(0)

    @pl.when(i == 0)
    def _():
        carry[...] = jnp.zeros(carry.shape, F32)

    s = sel_ref[...]
    rank_ref[...] = _dot(tri_ref[...], s) + carry[...]
    carry[...] += jnp.sum(s.astype(F32), axis=0, keepdims=True)
    cnt_ref[...] = jnp.broadcast_to(carry[...], cnt_ref.shape)


def _rank(sel, tri):
    T = sel.shape[0]
    return pl.pallas_call(
        _rank_kernel,
        grid=(T // TR,),
        in_specs=[pl.BlockSpec((TR, N_EXP_PAD), lambda i: (i, 0)),
                  pl.BlockSpec((TR, TR), lambda i: (0, 0))],
        out_specs=[pl.BlockSpec((TR, N_EXP_PAD), lambda i: (i, 0)),
                   pl.BlockSpec((8, N_EXP_PAD), lambda i: (0, 0))],
        out_shape=[jax.ShapeDtypeStruct((T, N_EXP_PAD), F32),
                   jax.ShapeDtypeStruct((8, N_EXP_PAD), F32)],
        scratch_shapes=[pltpu.VMEM((1, N_EXP_PAD), F32)],
        compiler_params=_cparams(("arbitrary",)),
        name="route_rank",
    )(sel, tri)


TILE_ROWS = TR * SUB


def _token_tile(ref, tok):
    return ref.at[pl.ds(pl.multiple_of(tok * SUB, SUB), SUB)]


def _dispatch_kernel(pos0_ref, pos1_ref, h_ref, xs_in, xs_hbm, stage, sem):
    del xs_in
    i = pl.program_id(0)
    slot = i % 2
    base = pl.multiple_of(slot * TILE_ROWS, TILE_ROWS)
    stage[pl.ds(base, TILE_ROWS), :] = h_ref[...]

    def issue(r, c):
        t = i * TR + r
        src = _token_tile(stage, slot * TR + r)
        pltpu.make_async_copy(src, _token_tile(xs_hbm, pos0_ref[t]), sem.at[slot]).start()
        pltpu.make_async_copy(src, _token_tile(xs_hbm, pos1_ref[t]), sem.at[slot]).start()
        return c
    lax.fori_loop(0, TR, issue, 0, unroll=8)

    def wait_slot(s):
        b = pl.multiple_of(s * TILE_ROWS, TILE_ROWS)
        for _ in range(2):
            pltpu.make_async_copy(stage.at[pl.ds(b, TILE_ROWS)], xs_hbm.at[pl.ds(0, TILE_ROWS)],
                                  sem.at[s]).wait()

    @pl.when(i > 0)
    def _():
        wait_slot(1 - slot)

    @pl.when(i == pl.num_programs(0) - 1)
    def _():
        wait_slot(slot)


def _dispatch(pos0, pos1, h2t, n_rows):
    T = h2t.shape[0] // SUB
    zeros = jnp.zeros((n_rows * SUB, LANES), F32)
    return pl.pallas_call(
        _dispatch_kernel,
        grid_spec=pltpu.PrefetchScalarGridSpec(
            num_scalar_prefetch=2, grid=(T // TR,),
            in_specs=[pl.BlockSpec((TILE_ROWS, LANES), lambda i, p0, p1: (i, 0)),
                      pl.BlockSpec(memory_space=pl.ANY)],
            out_specs=pl.BlockSpec(memory_space=pl.ANY),
            scratch_shapes=[pltpu.VMEM((2 * TILE_ROWS, LANES), F32),
                            pltpu.SemaphoreType.DMA((2,))]),
        out_shape=jax.ShapeDtypeStruct((n_rows * SUB, LANES), F32),
        input_output_aliases={3: 0},
        compiler_params=_cparams(("arbitrary",)),
        name="moe_dispatch",
    )(pos0, pos1, h2t, zeros)


def _combine_kernel(pos0_ref, pos1_ref, y_hbm, x1_ref, rw_ref, mod_ref, o_ref, buf0, buf1, sem0, sem1):
    i = pl.program_id(0)
    nt = pl.num_programs(0)

    def issue(tile, slot):
        def body(r, c):
            t = tile * TR + r
            pltpu.make_async_copy(_token_tile(y_hbm, pos0_ref[t]), _token_tile(buf0, slot * TR + r),
                                  sem0.at[slot]).start()
            pltpu.make_async_copy(_token_tile(y_hbm, pos1_ref[t]), _token_tile(buf1, slot * TR + r),
                                  sem1.at[slot]).start()
            return c
        lax.fori_loop(0, TR, body, 0, unroll=8)

    @pl.when(i == 0)
    def _():
        issue(0, 0)

    @pl.when(i + 1 < nt)
    def _():
        issue(i + 1, (i + 1) % 2)

    slot = i % 2
    base = pl.multiple_of(slot * TILE_ROWS, TILE_ROWS)
    for buf, sem in ((buf0, sem0), (buf1, sem1)):
        pltpu.make_async_copy(y_hbm.at[pl.ds(0, TILE_ROWS)], buf.at[pl.ds(base, TILE_ROWS)],
                              sem.at[slot]).wait()
    rw = rw_ref[...]
    w0 = rw[:, 0:1]
    w1 = rw[:, 1:2]
    for j in range(SUB):
        sl = slice(LANES * j, LANES * (j + 1))
        f = (w0 * buf0[pl.ds(base + j, TR, stride=SUB), :]
             + w1 * buf1[pl.ds(base + j, TR, stride=SUB), :])
        o_ref[:, sl] = x1_ref[:, sl] + mod_ref[:, 5 * D_MODEL + LANES * j:5 * D_MODEL + LANES * (j + 1)] * f


def _combine(pos0, pos1, ys, x1, rw, mod4, l, t_ctx, dec_seq):
    T = x1.shape[0]
    row = functools.partial(_mod_row, tm=TR, t_ctx=t_ctx, dec_seq=dec_seq)
    return pl.pallas_call(
        _combine_kernel,
        grid_spec=pltpu.PrefetchScalarGridSpec(
            num_scalar_prefetch=2, grid=(T // TR,),
            in_specs=[pl.BlockSpec(memory_space=pl.ANY),
                      pl.BlockSpec((TR, D_MODEL), lambda i, p0, p1: (i, 0)),
                      pl.BlockSpec((TR, N_EXP_PAD), lambda i, p0, p1: (i, 0)),
                      pl.BlockSpec((None, None, 1, 6 * D_MODEL), lambda i, p0, p1: (l, row(i), 0, 0))],
            out_specs=pl.BlockSpec((TR, D_MODEL), lambda i, p0, p1: (i, 0)),
            scratch_shapes=[pltpu.VMEM((2 * TILE_ROWS, LANES), F32), pltpu.VMEM((2 * TILE_ROWS, LANES), F32),
                            pltpu.SemaphoreType.DMA((2,)), pltpu.SemaphoreType.DMA((2,))]),
        out_shape=jax.ShapeDtypeStruct((T, D_MODEL), F32),
        compiler_params=_cparams(("arbitrary",)),
        name="moe_combine",
    )(pos0, pos1, ys, x1, rw, mod4)


def _block_ones(n, seg):
    idx = np.arange(n) // seg
    return jnp.asarray((idx[:, None] == idx[None, :]).astype(np.float32), dtype=BF16)


def _rope_tables(seq, head_w, rot_dim, n_heads):
    rows = seq // GRID_W
    nf = rot_dim // 4
    freqs = ROPE_THETA ** (-np.arange(nf, dtype=np.float64) / nf)
    row = np.repeat(np.arange(rows, dtype=np.float64), GRID_W)
    col = np.tile(np.arange(GRID_W, dtype=np.float64), rows)
    ang = np.concatenate([row[:, None] * freqs, col[:, None] * freqs], axis=-1)
    ang = np.repeat(ang, 2, axis=-1)
    even = (np.arange(rot_dim) % 2 == 0)[None, :]
    c = np.ones((seq, head_w))
    se = np.zeros((seq, head_w))
    so = np.zeros((seq, head_w))
    c[:, :rot_dim] = np.cos(ang)
    se[:, :rot_dim] = np.where(even, -np.sin(ang), 0.0)
    so[:, :rot_dim] = np.where(even, 0.0, np.sin(ang))
    return tuple(jnp.asarray(np.tile(t, (1, n_heads)), dtype=F32) for t in (c, se, so))


def _pool_mask():
    m = np.zeros((2 * D_PAD, D_CH), np.float32)
    for gi, w in enumerate(POOL):
        left = w // 2
        right = w - 1 - left
        for off in range(-left, right + 1):
            m[off + D_PAD, gi * D_G:(gi + 1) * D_G] = 1.0
    return jnp.asarray(m)


def _b_heads(nope, rope):
    ref = nope if nope is not None else rope
    lead = ref.shape[:-1]
    z = lambda w: jnp.zeros(lead + (w,), ref.dtype)
    parts = [rope if rope is not None else z(B_ROPE), nope if nope is not None else z(B_NOPE),
             z(B_HEAD_PAD - B_QK)]
    out = jnp.concatenate(parts, axis=-1)
    return out.reshape(lead[:-1] + (lead[-1] * B_HEAD_PAD,))


def kernel(x_prompt, x_sample, cache_a_k, cache_a_v, cache_b_ckv, cache_b_kpe, c, c_ctx, ada_w, ada_b, norm1_g, norm2_g, w_in, a_q_norm, a_k_norm, a_lambda, a_sub_norm, a_w_o, b_q_lora_norm, b_kv_lora_norm, b_w_uq, b_w_ukv, b_q_norm, b_k_norm, b_w_o, c_dw, c_dw_b, c_ln_g, c_ln_b, c_w_o, d_w_group, d_scale, d_w_o, w_out, ffn_w1, ffn_w3, ffn_w2, moe_router, moe_w1, moe_w3, moe_w2):
    nb, seq, _ = x_prompt.shape
    db, dseq, _ = x_sample.shape
    L = w_in.shape[0]
    past = cache_a_k.shape[2]
    t_ctx = nb * seq
    t_lat = db * dseq
    T = t_ctx + t_lat
    na = 2 * A_HEADS * A_QK
    assert t_ctx % TG == 0 and dseq % TG == 0 and seq % TR == 0 and db + 1 <= 16

    x = jnp.concatenate([x_prompt.reshape(t_ctx, D_MODEL), x_sample.reshape(t_lat, D_MODEL)], axis=0)
    cv = jnp.concatenate([c_ctx[None, :], c, jnp.zeros((15 - db, D_MODEL), F32)], axis=0)
    mod4 = _ada_all(cv, ada_w, ada_b).reshape(L, 16, 1, 6 * D_MODEL)

    bd_a = _block_ones(na, A_QK)
    bd_b = _block_ones(B_HEADS * B_HEAD_PAD, B_HEAD_PAD)
    rope_a = _rope_tables(dseq, A_QK, A_QK, 2 * A_HEADS)
    rope_b = _rope_tables(dseq, B_HEAD_PAD, B_ROPE, B_HEADS)
    pmask = _pool_mask()
    tri = jnp.asarray(np.tril(np.ones((TR, TR), np.float32), -1), dtype=BF16)

    row1 = lambda a: a.reshape(L, 1, -1)
    w_a, w_b, w_cd, w_g = _repack(w_in)
    gq_a = row1(jnp.tile(a_q_norm, (1, 2 * A_HEADS)))
    gk_a = row1(jnp.tile(a_k_norm, (1, 2 * A_HEADS)))
    uq = b_w_uq.reshape(L, B_QL, B_HEADS, B_QK)
    wuq = _b_heads(uq[..., :B_NOPE], uq[..., B_NOPE:]).astype(BF16)
    ukv = b_w_ukv.reshape(L, B_KVL, B_HEADS, B_NOPE + B_V)
    wuk = _b_heads(ukv[..., :B_NOPE], None).astype(BF16)
    wuv = ukv[..., B_NOPE:].reshape(L, B_KVL, B_HEADS * B_V).astype(BF16)
    gq_b = row1(jnp.tile(_b_heads(b_q_norm[:, None, :B_NOPE], b_q_norm[:, None, B_NOPE:]), (1, B_HEADS)))
    gk_b = row1(jnp.tile(_b_heads(b_k_norm[:, None, :B_NOPE], b_k_norm[:, None, B_NOPE:]), (1, B_HEADS)))
    bdd = jnp.zeros((L, D_CH, D_CH), F32)
    for gi in range(len(POOL)):
        bdd = bdd.at[:, gi * D_G:(gi + 1) * D_G, gi * D_G:(gi + 1) * D_G].set(d_w_group[:, gi])
    bdd = bdd.astype(BF16)
    a_wo, b_wo, c_wo, d_wo, wo = (w.astype(BF16) for w in (a_w_o, b_w_o, c_w_o, d_w_o, w_out))
    ffn1, ffn3, ffn2 = (w.astype(BF16) for w in (ffn_w1, ffn_w3, ffn_w2))
    r_pad = jnp.pad(moe_router, ((0, 0), (0, 0), (0, N_EXP_PAD - N_EXP)))
    r_hi = r_pad.astype(BF16)
    r_lo = (r_pad - r_hi.astype(F32)).astype(BF16)
    n1g, n2g = row1(norm1_g), row1(norm2_g)
    g_sub = row1(a_sub_norm)
    gql, gkvl = row1(b_q_lora_norm), row1(b_kv_lora_norm)
    dwb, lng, lnb, dsc = row1(c_dw_b), row1(c_ln_g), row1(c_ln_b), row1(d_scale)

    ck_a = cache_a_k.reshape(db, L, past, na).astype(BF16)
    cv_a = cache_a_v.reshape(db, L, past, A_HEADS * A_V).astype(BF16)
    kpe_pad = jnp.pad(cache_b_kpe, ((0, 0), (0, 0), (0, 0), (0, LANES - B_ROPE)))
    ck_b, cv_b = _cache_b(cache_b_ckv, kpe_pad, wuk, wuv, gk_b, bd_b)

    n_rows = 2 * T + N_EXP * TG
    nt_g = n_rows // TG

    new_ak, new_av, new_ckv, new_kpe = [], [], [], []
    for l in range(L):
        h1 = _prep(x, mod4, n1g, l, t_ctx, dseq)

        q1c, q2c, kc, vc, nk, nv = _proj_a(h1, w_a, gq_a, gk_a, bd_a, None, l, 0, t_ctx, False)
        q1l, q2l, kl, vl = _proj_a(h1, w_a, gq_a, gk_a, bd_a, rope_a, l, t_ctx, t_lat, True)
        oa = (_attn_a(q1c, q2c, kc, vc, None, None, a_lambda, g_sub, l, seq, False),
              _attn_a(q1l, q2l, kl, vl, ck_a, cv_a, a_lambda, g_sub, l, dseq, True))
        new_ak.append(nk)
        new_av.append(nv)

        bargs = (w_b, gql, gkvl, wuq, wuk, wuv, gq_b, gk_b, bd_b)
        qc, kc, vc, nckv, nkpe = _proj_b(h1, *bargs, None, l, 0, t_ctx, False)
        ql, kl, vl = _proj_b(h1, *bargs, rope_b, l, t_ctx, t_lat, True)
        ob = (_attn_b(qc, kc, vc, None, None, l, seq, False),
              _attn_b(ql, kl, vl, ck_b, cv_b, l, dseq, True))
        new_ckv.append(nckv)
        new_kpe.append(nkpe[:, 0:B_ROPE])

        cdargs = (w_cd, c_dw, dwb, lng, lnb, bdd, dsc, pmask)
        occ, odc = _mix_cd(h1, *cdargs, l, 0, t_ctx, seq)
        ocl, odl = _mix_cd(h1, *cdargs, l, t_ctx, t_lat, dseq)

        j = l // 2
        moe = l % 2 == 1
        router = (r_hi, r_lo) if moe else None
        outs = _merge(x, h1, (oa, ob, (occ, ocl), (odc, odl)), mod4, n2g, w_g, a_wo, b_wo,
                      c_wo, d_wo, wo, router, l, j, t_ctx, dseq)
        if not moe:
            x1, h2 = outs
            x = _ffn_dense(h2, ffn1, ffn3, ffn2, x1, mod4, l, j, t_ctx, dseq)
        else:
            x1, h2t, ridx, rw, sel = outs
            rank, cnt = _rank(sel, tri)
            counts = cnt[0, :N_EXP].astype(I32)
            padded = ((counts + TG - 1) // TG) * TG
            ends = jnp.cumsum(padded)
            offs = ends - padded
            pos_all = offs[None, :] + rank[:, :N_EXP].astype(I32)
            pos0 = jnp.take_along_axis(pos_all, ridx[:, 0:1], axis=1)[:, 0]
            pos1 = jnp.take_along_axis(pos_all, ridx[:, 1:2], axis=1)[:, 0]
            tile_start = jnp.arange(nt_g, dtype=I32) * TG
            tile_e = jnp.minimum(jnp.sum(tile_start[:, None] >= ends[None, :], axis=1), N_EXP - 1).astype(I32)
            n_active = (ends[-1] // TG).astype(I32).reshape(1)
            last_e = tile_e[jnp.maximum(n_active[0] - 1, 0)]
            tile_e = jnp.where(tile_start < ends[-1], tile_e, last_e)
            xs = _dispatch(pos0, pos1, h2t, n_rows)
            ys = _ffn_experts(xs, moe_w1, moe_w3, moe_w2, tile_e, n_active, j)
            x = _combine(pos0, pos1, ys, x1, rw, mod4, l, t_ctx, dseq)

    y_prompt = x[:t_ctx].reshape(nb, seq, D_MODEL)
    y_sample = x[t_ctx:].reshape(db, dseq, D_MODEL)
    new_a_k = jnp.stack(new_ak, axis=0).reshape(L, nb, seq, 2 * A_HEADS, A_QK).transpose(1, 0, 2, 3, 4)
    new_a_v = jnp.stack(new_av, axis=0).reshape(L, nb, seq, A_HEADS, A_V).transpose(1, 0, 2, 3, 4)
    new_b_ckv = jnp.stack(new_ckv, axis=0).reshape(L, nb, seq, B_KVL).transpose(1, 0, 2, 3)
    new_b_kpe = jnp.stack(new_kpe, axis=0).reshape(L, nb, seq, B_ROPE).transpose(1, 0, 2, 3)
    return (y_prompt, y_sample, new_a_k, new_a_v, new_b_ckv, new_b_kpe)
```

```python
import functools
import math

import numpy as np
import jax
import jax.numpy as jnp
from jax import lax
from jax.experimental import pallas as pl
from jax.experimental.pallas import tpu as pltpu

F32 = jnp.float32
BF16 = jnp.bfloat16
I32 = jnp.int32

EPS = 1e-6
D_MODEL = 1024
GRID_W = 64
ROPE_THETA = 10000.0
A_HEADS = 4
A_QK = 64
A_V = 128
B_HEADS = 4
B_NOPE = 64
B_ROPE = 32
B_QK = B_NOPE + B_ROPE
B_V = 64
B_QL = 256
B_KVL = 128
B_HEAD_PAD = 128
C_CH = 256
C_W = 31
C_PAD = 16
POOL = (2, 4, 8, 16)
D_G = 64
D_CH = D_G * len(POOL)
D_PAD = 8
N_EXP = 8
N_EXP_PAD = 128
LANES = 128
SUB = 8

O_AQ, O_AK, O_AV, O_BQ, O_BKV, O_KPE, O_C, O_D, O_G, O_END = (
    0, 512, 1024, 1536, 1792, 1920, 1952, 2464, 2720, 6816)
W_B_COLS = (O_KPE - O_BQ) + LANES

TM = 512
TQ = 512
TG = 1024
TR = 256
TF_DENSE = 1408
TF_MOE = 512
VMEM_LIMIT = 56 * 1024 * 1024


def _cparams(sem):
    return pltpu.CompilerParams(dimension_semantics=sem, vmem_limit_bytes=VMEM_LIMIT)


def _dot(a, b):
    return jnp.dot(a, b, preferred_element_type=F32)


def _dot_nt(a, b):
    return lax.dot_general(a, b, (((1,), (1,)), ((), ())), preferred_element_type=F32)


def _bf(x):
    return x.astype(BF16)


def _rms(x, g):
    return x * lax.rsqrt(jnp.mean(x * x, axis=-1, keepdims=True) + EPS) * g


def _seg_sum_sq(x, bd_ref):
    return _dot(_bf(x * x), bd_ref[...])


def _rope(x, c_ref, se_ref, so_ref):
    n = x.shape[-1]
    return (x * c_ref[...] + pltpu.roll(x, n - 1, 1) * se_ref[...]
            + pltpu.roll(x, 1, 1) * so_ref[...])


def _mod_row(i, tm, t_ctx, dec_seq):
    r = i * tm
    return jnp.where(r < t_ctx, 0, 1 + (r - t_ctx) // dec_seq)


def _layer(a, l):
    nd = a.ndim
    return pl.BlockSpec((None,) + a.shape[1:], lambda *_: (l,) + (0,) * (nd - 1))


def _whole(a):
    nd = a.ndim
    return pl.BlockSpec(a.shape, lambda *_: (0,) * nd)


def _ada_kernel(cv_ref, w_ref, b_ref, o_ref):
    cv = cv_ref[...]
    s = cv * jax.nn.sigmoid(cv)
    o_ref[...] = _dot(_bf(s), _bf(w_ref[...])) + b_ref[...]


def _ada_all(cv, ada_w, ada_b):
    L, d, n = ada_w.shape
    tn = 1536
    return pl.pallas_call(
        _ada_kernel,
        grid=(L, n // tn),
        in_specs=[pl.BlockSpec((16, d), lambda l, j: (0, 0)),
                  pl.BlockSpec((None, d, tn), lambda l, j: (l, 0, j)),
                  pl.BlockSpec((None, 1, tn), lambda l, j: (l, 0, j))],
        out_specs=pl.BlockSpec((None, 16, tn), lambda l, j: (l, 0, j)),
        out_shape=jax.ShapeDtypeStruct((L, 16, n), F32),
        compiler_params=_cparams(("parallel", "parallel")),
        name="ada_mod",
    )(cv, ada_w, ada_b.reshape(L, 1, n))


def _repack_kernel(w_ref, wa_ref, wb_ref, wcd_ref, wg_ref):
    wa_ref[...] = _bf(w_ref[:, O_AQ:O_BQ])
    wb_ref[:, 0:O_KPE - O_BQ] = _bf(w_ref[:, O_BQ:O_KPE])
    blk = w_ref[:, O_KPE:O_KPE + LANES]
    lane = lax.broadcasted_iota(I32, blk.shape, 1)
    wb_ref[:, O_KPE - O_BQ:W_B_COLS] = _bf(jnp.where(lane < B_ROPE, blk, 0.0))
    wcd_ref[...] = _bf(w_ref[:, O_C:O_G])
    wg_ref[...] = _bf(w_ref[:, O_G:O_END])


def _repack(w_in):
    L, d, n = w_in.shape
    tr = 256
    widths = (O_BQ - O_AQ, W_B_COLS, O_G - O_C, O_END - O_G)
    return pl.pallas_call(
        _repack_kernel,
        grid=(L, d // tr),
        in_specs=[pl.BlockSpec((None, tr, n), lambda l, i: (l, i, 0))],
        out_specs=[pl.BlockSpec((None, tr, w), lambda l, i: (l, i, 0)) for w in widths],
        out_shape=[jax.ShapeDtypeStruct((L, d, w), BF16) for w in widths],
        compiler_params=_cparams(("parallel", "parallel")),
        name="repack_w_in",
    )(w_in)


def _prep_kernel(x_ref, mod_ref, g_ref, h_ref):
    h_ref[...] = _pre_norm1(x_ref[...], mod_ref, g_ref)


def _prep(x, mod4, g, l, t_ctx, dec_seq):
    T = x.shape[0]
    row = functools.partial(_mod_row, tm=TM, t_ctx=t_ctx, dec_seq=dec_seq)
    return pl.pallas_call(
        _prep_kernel,
        grid=(T // TM,),
        in_specs=[pl.BlockSpec((TM, D_MODEL), lambda i: (i, 0)),
                  pl.BlockSpec((None, None, 1, 6 * D_MODEL), lambda i: (l, row(i), 0, 0)),
                  _layer(g, l)],
        out_specs=pl.BlockSpec((TM, D_MODEL), lambda i: (i, 0)),
        out_shape=jax.ShapeDtypeStruct((T, D_MODEL), BF16),
        compiler_params=_cparams(("parallel",)),
        name="prep",
    )(x, mod4, g)


def _proj_a_kernel(*refs, latent):
    if latent:
        (h_ref, w_ref, gq_ref, gk_ref, bd_ref, c_ref, se_ref, so_ref,
         q1_ref, q2_ref, k_ref, v_ref) = refs
    else:
        (h_ref, w_ref, gq_ref, gk_ref, bd_ref,
         q1_ref, q2_ref, k_ref, v_ref, nk_ref, nv_ref) = refs
    p = _dot(h_ref[...], w_ref[...])
    n = 2 * A_HEADS * A_QK
    q = p[:, 0:n]
    k = p[:, n:2 * n]
    v = p[:, 2 * n:3 * n]
    q = q * lax.rsqrt(_seg_sum_sq(q, bd_ref) * (1.0 / A_QK) + EPS) * gq_ref[...]
    k = k * lax.rsqrt(_seg_sum_sq(k, bd_ref) * (1.0 / A_QK) + EPS) * gk_ref[...]
    if latent:
        q = _rope(q, c_ref, se_ref, so_ref)
        k = _rope(k, c_ref, se_ref, so_ref)
    else:
        nk_ref[...] = k
        nv_ref[...] = v
    q = q * (A_QK ** -0.5 * LOG2E)
    lane = lax.broadcasted_iota(I32, q.shape, 1)
    first = (lane % (2 * A_QK)) < A_QK
    q1_ref[...] = _bf(jnp.where(first, q, 0.0))
    q2_ref[...] = _bf(jnp.where(first, 0.0, q))
    k_ref[...] = _bf(k)
    v_ref[...] = _bf(v)


def _proj_a(h1, w_a, gq, gk, bd, rope_tabs, l, row0, rows, latent):
    n = 2 * A_HEADS * A_QK
    b0 = row0 // TM
    tab_blocks = rope_tabs[0].shape[0] // TM if latent else 1
    in_specs = [pl.BlockSpec((TM, D_MODEL), lambda i: (i + b0, 0)),
                _layer(w_a, l), _layer(gq, l), _layer(gk, l), _whole(bd)]
    args = [h1, w_a, gq, gk, bd]
    if latent:
        in_specs += [pl.BlockSpec((TM, n), lambda i: (i % tab_blocks, 0))] * 3
        args += list(rope_tabs)
    out_spec = pl.BlockSpec((TM, n), lambda i: (i, 0))
    out_shape = [jax.ShapeDtypeStruct((rows, n), BF16)] * 4
    out_specs = [out_spec] * 4
    if not latent:
        out_shape += [jax.ShapeDtypeStruct((rows, n), F32)] * 2
        out_specs += [out_spec] * 2
    return pl.pallas_call(
        functools.partial(_proj_a_kernel, latent=latent),
        grid=(rows // TM,),
        in_specs=in_specs, out_specs=out_specs, out_shape=out_shape,
        compiler_params=_cparams(("parallel",)),
        name="proj_a_lat" if latent else "proj_a_ctx",
    )(*args)


LOG2E = math.log2(math.e)


def _attend(qs, kss, vs):
    es, rs = [], []
    for q, ks in zip(qs, kss):
        scores = [_dot_nt(q, k) for k in ks]
        m = functools.reduce(jnp.maximum, [jnp.max(s, axis=-1, keepdims=True) for s in scores])
        e = [jnp.exp2(s - m) for s in scores]
        l = functools.reduce(lambda a, b: a + b, [jnp.sum(x, axis=-1, keepdims=True) for x in e])
        es.append([_bf(x) for x in e])
        rs.append(1.0 / l)
    o = None
    for p, v in enumerate(vs):
        stacked = es[0][p] if len(qs) == 1 else jnp.concatenate([e[p] for e in es], axis=0)
        t = _dot(stacked, v)
        o = t if o is None else o + t
    outs, r0 = [], 0
    for q, r in zip(qs, rs):
        outs.append(o[r0:r0 + q.shape[0]] * r)
        r0 += q.shape[0]
    return outs


def _attn_a_kernel(*refs, latent, lam_init):
    if latent:
        q1_ref, q2_ref, k_ref, v_ref, kc_ref, vc_ref, lam_ref, g_ref, o_ref = refs
    else:
        q1_ref, q2_ref, k_ref, v_ref, lam_ref, g_ref, o_ref = refs
    lm = lam_ref[...]
    lam = (jnp.exp(jnp.sum(lm[0:1] * lm[1:2], axis=-1, keepdims=True))
           - jnp.exp(jnp.sum(lm[2:3] * lm[3:4], axis=-1, keepdims=True)) + lam_init)
    for h in range(A_HEADS):
        sl = slice(A_V * h, A_V * (h + 1))
        ks = [k_ref[:, sl]]
        vs = [v_ref[:, sl]]
        if latent:
            ks.append(kc_ref[:, sl])
            vs.append(vc_ref[:, sl])
        tq = q1_ref.shape[0]
        oc, = _attend([jnp.concatenate([q1_ref[:, sl], q2_ref[:, sl]], axis=0)], [ks], vs)
        o = oc[0:tq] - lam * oc[tq:2 * tq]
        o = _rms(o, g_ref[...]) * (1.0 - lam_init)
        o_ref[:, sl] = _bf(o)


def _attn_a(q1, q2, k, v, cache_k, cache_v, a_lambda, g_sub, layer, seq, latent):
    rows, n = q1.shape
    nb = rows // seq
    lam_init = 0.8 - 0.6 * math.exp(-0.3 * layer)
    kern = functools.partial(_attn_a_kernel, latent=latent, lam_init=lam_init)
    if latent:
        nq = seq // TQ
        past = cache_k.shape[2]
        grid = (nb, nq)
        qs = pl.BlockSpec((TQ, n), lambda b, j: (b * nq + j, 0))
        kv = pl.BlockSpec((seq, n), lambda b, j: (b, 0))
        cs = pl.BlockSpec((None, None, past, n), lambda b, j: (b, layer, 0, 0))
        in_specs = [qs, qs, kv, kv, cs, cs, _layer(a_lambda, layer), _layer(g_sub, layer)]
        args = (q1, q2, k, v, cache_k, cache_v, a_lambda, g_sub)
        sem = ("parallel", "parallel")
        out_spec = qs
    else:
        grid = (nb,)
        bs = pl.BlockSpec((seq, n), lambda b: (b, 0))
        in_specs = [bs, bs, bs, bs, _layer(a_lambda, layer), _layer(g_sub, layer)]
        args = (q1, q2, k, v, a_lambda, g_sub)
        sem = ("parallel",)
        out_spec = bs
    return pl.pallas_call(
        kern, grid=grid, in_specs=in_specs, out_specs=out_spec,
        out_shape=jax.ShapeDtypeStruct((rows, n), BF16),
        compiler_params=_cparams(sem),
        name="attn_a_lat" if latent else "attn_a_ctx",
    )(*args)


def _mla_keys(ckv, kpe, wuk_ref, wuv_ref, gk_ref, bd_ref):
    cb = _bf(ckv)
    kn = _dot(cb, wuk_ref[...]) + jnp.concatenate([kpe] * B_HEADS, axis=1)
    k = kn * lax.rsqrt(_seg_sum_sq(kn, bd_ref) * (1.0 / B_QK) + EPS) * gk_ref[...]
    v = _dot(cb, wuv_ref[...])
    return k, v


def _proj_b_kernel(*refs, latent):
    if latent:
        (h_ref, w_ref, gql_ref, gkvl_ref, wuq_ref, wuk_ref, wuv_ref, gq_ref, gk_ref, bd_ref,
         c_ref, se_ref, so_ref, q_ref, k_ref, v_ref) = refs
    else:
        (h_ref, w_ref, gql_ref, gkvl_ref, wuq_ref, wuk_ref, wuv_ref, gq_ref, gk_ref, bd_ref,
         q_ref, k_ref, v_ref, nckv_ref, nkpe_ref) = refs
    p = _dot(h_ref[...], w_ref[...])
    bq = p[:, 0:B_QL]
    bkv = p[:, B_QL:B_QL + B_KVL]
    kpe = p[:, B_QL + B_KVL:B_QL + B_KVL + LANES]
    q = _dot(_bf(_rms(bq, gql_ref[...])), wuq_ref[...])
    q = q * lax.rsqrt(_seg_sum_sq(q, bd_ref) * (1.0 / B_QK) + EPS) * gq_ref[...]
    ckv = _rms(bkv, gkvl_ref[...])
    k, v = _mla_keys(ckv, kpe, wuk_ref, wuv_ref, gk_ref, bd_ref)
    if latent:
        q = _rope(q, c_ref, se_ref, so_ref)
        k = _rope(k, c_ref, se_ref, so_ref)
    else:
        nckv_ref[...] = ckv
        nkpe_ref[...] = kpe
    q_ref[...] = _bf(q * (B_QK ** -0.5 * LOG2E))
    k_ref[...] = _bf(k)
    v_ref[...] = _bf(v)


def _proj_b(h1, w_b, gql, gkvl, wuq, wuk, wuv, gq, gk, bd, rope_tabs, l, row0, rows, latent):
    n = B_HEADS * B_HEAD_PAD
    nv = B_HEADS * B_V
    b0 = row0 // TM
    tab_blocks = rope_tabs[0].shape[0] // TM if latent else 1
    in_specs = [pl.BlockSpec((TM, D_MODEL), lambda i: (i + b0, 0))] + [
        _layer(a, l) for a in (w_b, gql, gkvl, wuq, wuk, wuv, gq, gk)] + [_whole(bd)]
    args = [h1, w_b, gql, gkvl, wuq, wuk, wuv, gq, gk, bd]
    if latent:
        in_specs += [pl.BlockSpec((TM, n), lambda i: (i % tab_blocks, 0))] * 3
        args += list(rope_tabs)
    row = lambda w: pl.BlockSpec((TM, w), lambda i: (i, 0))
    out_shape = [jax.ShapeDtypeStruct((rows, n), BF16), jax.ShapeDtypeStruct((rows, n), BF16),
                 jax.ShapeDtypeStruct((rows, nv), BF16)]
    out_specs = [row(n), row(n), row(nv)]
    if not latent:
        out_shape += [jax.ShapeDtypeStruct((rows, B_KVL), F32),
                      jax.ShapeDtypeStruct((rows, LANES), F32)]
        out_specs += [row(B_KVL), row(LANES)]
    return pl.pallas_call(
        functools.partial(_proj_b_kernel, latent=latent),
        grid=(rows // TM,),
        in_specs=in_specs, out_specs=out_specs, out_shape=out_shape,
        compiler_params=_cparams(("parallel",)),
        name="proj_b_lat" if latent else "proj_b_ctx",
    )(*args)


def _cache_b_kernel(ckv_ref, kpe_ref, wuk_ref, wuv_ref, gk_ref, bd_ref, k_ref, v_ref):
    k, v = _mla_keys(ckv_ref[...], kpe_ref[...], wuk_ref, wuv_ref, gk_ref, bd_ref)
    k_ref[...] = _bf(k)
    v_ref[...] = _bf(v)


def _cache_b(ckv, kpe_pad, wuk, wuv, gk, bd):
    db, L, past, _ = ckv.shape
    n = B_HEADS * B_HEAD_PAD
    nv = B_HEADS * B_V
    blk = lambda w: pl.BlockSpec((None, None, past, w), lambda l, b: (b, l, 0, 0))
    wl = lambda a: pl.BlockSpec((None,) + a.shape[1:], lambda l, b: (l, 0, 0))
    return pl.pallas_call(
        _cache_b_kernel,
        grid=(L, db),
        in_specs=[blk(B_KVL), blk(LANES), wl(wuk), wl(wuv), wl(gk),
                  pl.BlockSpec(bd.shape, lambda l, b: (0, 0))],
        out_specs=[blk(n), blk(nv)],
        out_shape=[jax.ShapeDtypeStruct((db, L, past, n), BF16),
                   jax.ShapeDtypeStruct((db, L, past, nv), BF16)],
        compiler_params=_cparams(("parallel", "parallel")),
        name="cache_b_expand",
    )(ckv, kpe_pad, wuk, wuv, gk, bd)


def _attn_b_kernel(*refs, latent):
    if latent:
        q_ref, k_ref, v_ref, kc_ref, vc_ref, o_ref = refs
    else:
        q_ref, k_ref, v_ref, o_ref = refs
    lane = lax.broadcasted_iota(I32, (q_ref.shape[0], 2 * B_V), 1)
    for hp in range(B_HEADS // 2):
        vsl = slice(2 * B_V * hp, 2 * B_V * (hp + 1))
        vs = [v_ref[:, vsl]] + ([vc_ref[:, vsl]] if latent else [])
        qs, kss = [], []
        for h in (2 * hp, 2 * hp + 1):
            sl = slice(B_HEAD_PAD * h, B_HEAD_PAD * (h + 1))
            qs.append(q_ref[:, sl])
            kss.append([k_ref[:, sl]] + ([kc_ref[:, sl]] if latent else []))
        outs = _attend(qs, kss, vs)
        o_ref[:, vsl] = _bf(jnp.where(lane < B_V, outs[0], outs[1]))


def _attn_b(q, k, v, cache_k, cache_v, layer, seq, latent):
    rows, n = q.shape
    nv = v.shape[1]
    nb = rows // seq
    kern = functools.partial(_attn_b_kernel, latent=latent)
    if latent:
        nq = seq // TQ
        past = cache_k.shape[2]
        grid = (nb, nq)
        in_specs = [pl.BlockSpec((TQ, n), lambda b, j: (b * nq + j, 0)),
                    pl.BlockSpec((seq, n), lambda b, j: (b, 0)),
                    pl.BlockSpec((seq, nv), lambda b, j: (b, 0)),
                    pl.BlockSpec((None, None, past, n), lambda b, j: (b, layer, 0, 0)),
                    pl.BlockSpec((None, None, past, nv), lambda b, j: (b, layer, 0, 0))]
        args = (q, k, v, cache_k, cache_v)
        out_spec = pl.BlockSpec((TQ, nv), lambda b, j: (b * nq + j, 0))
        sem = ("parallel", "parallel")
    else:
        grid = (nb,)
        in_specs = [pl.BlockSpec((seq, n), lambda b: (b, 0)),
                    pl.BlockSpec((seq, n), lambda b: (b, 0)),
                    pl.BlockSpec((seq, nv), lambda b: (b, 0))]
        args = (q, k, v)
        out_spec = pl.BlockSpec((seq, nv), lambda b: (b, 0))
        sem = ("parallel",)
    return pl.pallas_call(
        kern, grid=grid, in_specs=in_specs, out_specs=out_spec,
        out_shape=jax.ShapeDtypeStruct((rows, nv), BF16),
        compiler_params=_cparams(sem),
        name="attn_b_lat" if latent else "attn_b_ctx",
    )(*args)


CONV_CHUNK = 64


def _mix_cd_kernel(h_ref, w_ref, dw_ref, dwb_ref, lng_ref, lnb_ref, bdd_ref, dsc_ref, pm_ref,
                   oc_ref, od_ref, gpad, dpad, *, seq):
    p = _dot(h_ref[...], w_ref[...])
    glu = p[:, 0:C_CH] * jax.nn.sigmoid(p[:, C_CH:2 * C_CH])
    gpad[0:C_PAD, :] = jnp.zeros((C_PAD, C_CH), F32)
    gpad[C_PAD + seq:2 * C_PAD + seq, :] = jnp.zeros((C_PAD, C_CH), F32)
    gpad[C_PAD:C_PAD + seq, :] = glu
    half = C_W // 2
    for c0 in range(0, seq, CONV_CHUNK):
        acc = jnp.zeros((CONV_CHUNK, C_CH), F32) + dwb_ref[...]
        for j in range(C_W):
            s = c0 + C_PAD - half + j
            acc = acc + gpad[s:s + CONV_CHUNK, :] * dw_ref[j:j + 1, :]
        mu = jnp.mean(acc, axis=-1, keepdims=True)
        xc = acc - mu
        y = xc * lax.rsqrt(jnp.mean(xc * xc, axis=-1, keepdims=True) + EPS)
        y = y * lng_ref[...] + lnb_ref[...]
        oc_ref[c0:c0 + CONV_CHUNK, :] = _bf(y * jax.nn.sigmoid(y))
    d = p[:, 2 * C_CH:2 * C_CH + D_CH]
    dpad[0:D_PAD, :] = jnp.zeros((D_PAD, D_CH), F32)
    dpad[D_PAD + seq:2 * D_PAD + seq, :] = jnp.zeros((D_PAD, D_CH), F32)
    dpad[D_PAD:D_PAD + seq, :] = d
    lane = lax.broadcasted_iota(I32, (CONV_CHUNK, D_CH), 1)
    for c0 in range(0, seq, CONV_CHUNK):
        acc = jnp.zeros((CONV_CHUNK, D_CH), F32)
        for j in range(2 * D_PAD):
            acc = acc + dpad[c0 + j:c0 + j + CONV_CHUNK, :] * pm_ref[j:j + 1, :]
        t = c0 + lax.broadcasted_iota(I32, (CONV_CHUNK, D_CH), 0)
        cnt = jnp.zeros((CONV_CHUNK, D_CH), I32)
        for gi, w in enumerate(POOL):
            left = w // 2
            right = w - 1 - left
            c = jnp.minimum(t + right + 1, seq) - jnp.maximum(t - left, 0)
            cnt = jnp.where(lane // D_G == gi, c, cnt)
        pooled = acc / cnt.astype(F32)
        diff = pooled - dpad[c0 + D_PAD:c0 + D_PAD + CONV_CHUNK, :]
        od_ref[c0:c0 + CONV_CHUNK, :] = _bf(_dot(_bf(diff), bdd_ref[...]) * dsc_ref[...])


def _mix_cd(h1, w_cd, dw, dwb, lng, lnb, bdd, dsc, pmask, l, row0, rows, seq):
    b0 = row0 // seq
    return pl.pallas_call(
        functools.partial(_mix_cd_kernel, seq=seq),
        grid=(rows // seq,),
        in_specs=[pl.BlockSpec((seq, D_MODEL), lambda b: (b + b0, 0))] + [
            _layer(a, l) for a in (w_cd, dw, dwb, lng, lnb, bdd, dsc)] + [_whole(pmask)],
        out_specs=[pl.BlockSpec((seq, C_CH), lambda b: (b, 0)),
                   pl.BlockSpec((seq, D_CH), lambda b: (b, 0))],
        out_shape=[jax.ShapeDtypeStruct((rows, C_CH), BF16),
                   jax.ShapeDtypeStruct((rows, D_CH), BF16)],
        scratch_shapes=[pltpu.VMEM((seq + 2 * C_PAD, C_CH), F32),
                        pltpu.VMEM((seq + 2 * D_PAD, D_CH), F32)],
        compiler_params=_cparams(("parallel",)),
        name="mix_cd_%d" % seq,
    )(h1, w_cd, dw, dwb, lng, lnb, bdd, dsc, pmask)


def _merge_kernel(*refs, moe, n_ctx_tiles):
    (x_ref, h_ref, oac_ref, oal_ref, obc_ref, obl_ref, occ_ref, ocl_ref, odc_ref, odl_ref,
     mod_ref, n2g_ref, wg_ref, wa_ref, wb_ref, wc_ref, wd_ref, wo_ref) = refs[:18]
    if moe:
        rhi_ref, rlo_ref, x1_ref, h2t_ref, ridx_ref, rw_ref, sel_ref = refs[18:]
    else:
        x1_ref, h2_ref = refs[18:]
    is_ctx = pl.program_id(0) < n_ctx_tiles
    h = h_ref[...]
    acc = None
    for i, (c_ref, l_ref, w_ref) in enumerate(((oac_ref, oal_ref, wa_ref), (obc_ref, obl_ref, wb_ref),
                                               (occ_ref, ocl_ref, wc_ref), (odc_ref, odl_ref, wd_ref))):
        gate = jax.nn.sigmoid(_dot(h, wg_ref[:, i * D_MODEL:(i + 1) * D_MODEL]))
        o = jnp.where(is_ctx, c_ref[...], l_ref[...])
        t = gate * _dot(o, w_ref[...])
        acc = t if acc is None else acc + t
    y = _dot(_bf(acc), wo_ref[...])
    x1 = x_ref[...] + mod_ref[:, 2 * D_MODEL:3 * D_MODEL] * y
    x1_ref[...] = x1
    h2 = (_rms(x1, n2g_ref[...]) * (1.0 + mod_ref[:, 4 * D_MODEL:5 * D_MODEL])
          + mod_ref[:, 3 * D_MODEL:4 * D_MODEL])
    if not moe:
        h2_ref[...] = _bf(h2)
        return
    tm = h2.shape[0]
    for j in range(SUB):
        h2t_ref[pl.ds(j, tm, stride=SUB), :] = h2[:, LANES * j:LANES * (j + 1)]
    hi = _bf(h2)
    lo = _bf(h2 - hi.astype(F32))
    logits = _dot(hi, rhi_ref[...]) + _dot(lo, rhi_ref[...]) + _dot(hi, rlo_ref[...])
    lane = lax.broadcasted_iota(I32, logits.shape, 1)
    lanef = lane.astype(F32)
    neg = jnp.float32(-jnp.inf)
    lg = jnp.where(lane < N_EXP, logits, neg)
    m0 = jnp.max(lg, axis=-1, keepdims=True)
    i0 = jnp.min(jnp.where(lg == m0, lanef, float(N_EXP_PAD)), axis=-1, keepdims=True)
    sel0 = lanef == i0
    lg1 = jnp.where(sel0, neg, lg)
    m1 = jnp.max(lg1, axis=-1, keepdims=True)
    i1 = jnp.min(jnp.where(lg1 == m1, lanef, float(N_EXP_PAD)), axis=-1, keepdims=True)
    sel1 = lanef == i1
    e = jnp.exp(m1 - m0)
    w0 = 1.0 / (1.0 + e)
    w1 = e / (1.0 + e)
    ridx_ref[...] = jnp.where(lane == 0, i0, jnp.where(lane == 1, i1, 0.0)).astype(I32)
    rw_ref[...] = jnp.where(lane == 0, w0, jnp.where(lane == 1, w1, 0.0))
    sel_ref[...] = jnp.where(sel0 | sel1, 1.0, 0.0).astype(BF16)


def _merge(x, h1, branches, mod4, n2g, wg, wa, wb, wc, wd, wo, router, l, j, t_ctx, dec_seq):
    T = x.shape[0]
    moe = router is not None
    nc = t_ctx // TM
    row = functools.partial(_mod_row, tm=TM, t_ctx=t_ctx, dec_seq=dec_seq)
    rowspec = lambda w: pl.BlockSpec((TM, w), lambda i: (i, 0))
    ctxspec = lambda w: pl.BlockSpec((TM, w), lambda i: (jnp.minimum(i, nc - 1), 0))
    latspec = lambda w: pl.BlockSpec((TM, w), lambda i: (jnp.maximum(i - nc, 0), 0))
    in_specs = [rowspec(D_MODEL), rowspec(D_MODEL)]
    args = [x, h1]
    for oc, ol in branches:
        in_specs += [ctxspec(oc.shape[1]), latspec(ol.shape[1])]
        args += [oc, ol]
    in_specs += [pl.BlockSpec((None, None, 1, 6 * D_MODEL), lambda i: (l, row(i), 0, 0))]
    in_specs += [_layer(a, l) for a in (n2g, wg, wa, wb, wc, wd, wo)]
    args += [mod4, n2g, wg, wa, wb, wc, wd, wo]
    out_shape = [jax.ShapeDtypeStruct((T, D_MODEL), F32)]
    out_specs = [rowspec(D_MODEL)]
    if moe:
        in_specs += [_layer(router[0], j), _layer(router[1], j)]
        args += list(router)
        out_shape += [jax.ShapeDtypeStruct((T * SUB, LANES), F32),
                      jax.ShapeDtypeStruct((T, N_EXP_PAD), I32),
                      jax.ShapeDtypeStruct((T, N_EXP_PAD), F32),
                      jax.ShapeDtypeStruct((T, N_EXP_PAD), BF16)]
        out_specs += [pl.BlockSpec((TM * SUB, LANES), lambda i: (i, 0)),
                      rowspec(N_EXP_PAD), rowspec(N_EXP_PAD), rowspec(N_EXP_PAD)]
    else:
        out_shape += [jax.ShapeDtypeStruct((T, D_MODEL), BF16)]
        out_specs += [rowspec(D_MODEL)]
    return pl.pallas_call(
        functools.partial(_merge_kernel, moe=moe, n_ctx_tiles=nc),
        grid=(T // TM,),
        in_specs=in_specs, out_specs=out_specs, out_shape=out_shape,
        compiler_params=_cparams(("parallel",)),
        name="merge_moe" if moe else "merge",
    )(*args)


def _swiglu_step(x, w1_ref, w3_ref, w2_ref, acc_ref, f):
    a = _dot(x, _bf(w1_ref[...]))
    b = _dot(x, _bf(w3_ref[...]))
    t = _dot(_bf(a * jax.nn.sigmoid(a) * b), _bf(w2_ref[...]))

    @pl.when(f == 0)
    def _():
        acc_ref[...] = t

    @pl.when(f > 0)
    def _():
        acc_ref[...] += t


def _pre_norm1(x, mod_ref, g_ref):
    return _bf(_rms(x, g_ref[...]) * (1.0 + mod_ref[:, D_MODEL:2 * D_MODEL]) + mod_ref[:, 0:D_MODEL])


def _ffn_dense_kernel(x_ref, w1_ref, w3_ref, w2_ref, x1_ref, mod_ref, modn_ref, gn_ref,
                      o_ref, hn_ref, acc_ref):
    f = pl.program_id(1)
    _swiglu_step(x_ref[...], w1_ref, w3_ref, w2_ref, acc_ref, f)

    @pl.when(f == pl.num_programs(1) - 1)
    def _():
        x2 = x1_ref[...] + mod_ref[:, 5 * D_MODEL:6 * D_MODEL] * acc_ref[...]
        o_ref[...] = x2
        hn_ref[...] = _pre_norm1(x2, modn_ref, gn_ref)


def _ffn_dense(h2, w1, w3, w2, x1, mod4, n1g, l, j, t_ctx, dec_seq):
    T = h2.shape[0]
    nf = w1.shape[2] // TF_DENSE
    row = functools.partial(_mod_row, tm=TG, t_ctx=t_ctx, dec_seq=dec_seq)
    rows = pl.BlockSpec((TG, D_MODEL), lambda i, f: (i, 0))
    return pl.pallas_call(
        _ffn_dense_kernel,
        grid=(T // TG, nf),
        in_specs=[rows,
                  pl.BlockSpec((None, D_MODEL, TF_DENSE), lambda i, f: (j, 0, f)),
                  pl.BlockSpec((None, D_MODEL, TF_DENSE), lambda i, f: (j, 0, f)),
                  pl.BlockSpec((None, TF_DENSE, D_MODEL), lambda i, f: (j, f, 0)),
                  rows,
                  pl.BlockSpec((None, None, 1, 6 * D_MODEL), lambda i, f: (l, row(i), 0, 0)),
                  pl.BlockSpec((None, None, 1, 6 * D_MODEL), lambda i, f: (l + 1, row(i), 0, 0)),
                  _layer(n1g, l + 1)],
        out_specs=[rows, rows],
        out_shape=[jax.ShapeDtypeStruct((T, D_MODEL), F32), jax.ShapeDtypeStruct((T, D_MODEL), BF16)],
        scratch_shapes=[pltpu.VMEM((TG, D_MODEL), F32)],
        compiler_params=_cparams(("parallel", "arbitrary")),
        name="ffn_dense",
    )(h2, w1, w3, w2, x1, mod4, mod4, n1g)


def _ffn_experts_kernel(te_ref, na_ref, x_ref, w1_ref, w3_ref, w2_ref, o_ref, xb_ref, acc_ref):
    i = pl.program_id(0)
    f = pl.program_id(1)
    active = i < na_ref[0]

    @pl.when(jnp.logical_and(active, f == 0))
    def _():
        for j in range(SUB):
            xb_ref[:, LANES * j:LANES * (j + 1)] = _bf(x_ref[pl.ds(j, TG, stride=SUB), :])

    @pl.when(active)
    def _():
        _swiglu_step(xb_ref[...], w1_ref, w3_ref, w2_ref, acc_ref, f)

    @pl.when(jnp.logical_and(active, f == pl.num_programs(1) - 1))
    def _():
        for j in range(SUB):
            o_ref[pl.ds(j, TG, stride=SUB), :] = acc_ref[:, LANES * j:LANES * (j + 1)]

    @pl.when(jnp.logical_and(jnp.logical_not(active), f == 0))
    def _():
        o_ref[...] = jnp.zeros(o_ref.shape, F32)


def _ffn_experts(xs, w1, w3, w2, tile_e, n_active, j):
    rows = xs.shape[0] // SUB
    nf = w1.shape[3] // TF_MOE

    def fidx(i, f, na):
        return jnp.where(i < na[0], f, nf - 1)

    def xidx(i, na):
        return jnp.minimum(i, na[0] - 1)

    return pl.pallas_call(
        _ffn_experts_kernel,
        grid_spec=pltpu.PrefetchScalarGridSpec(
            num_scalar_prefetch=2, grid=(rows // TG, nf),
            in_specs=[pl.BlockSpec((TG * SUB, LANES), lambda i, f, te, na: (xidx(i, na), 0)),
                      pl.BlockSpec((None, None, D_MODEL, TF_MOE),
                                   lambda i, f, te, na: (j, te[i], 0, fidx(i, f, na))),
                      pl.BlockSpec((None, None, D_MODEL, TF_MOE),
                                   lambda i, f, te, na: (j, te[i], 0, fidx(i, f, na))),
                      pl.BlockSpec((None, None, TF_MOE, D_MODEL),
                                   lambda i, f, te, na: (j, te[i], fidx(i, f, na), 0))],
            out_specs=pl.BlockSpec((TG * SUB, LANES), lambda i, f, te, na: (i, 0)),
            scratch_shapes=[pltpu.VMEM((TG, D_MODEL), BF16), pltpu.VMEM((TG, D_MODEL), F32)]),
        out_shape=jax.ShapeDtypeStruct((rows * SUB, LANES), F32),
        compiler_params=_cparams(("parallel", "arbitrary")),
        name="ffn_experts",
    )(tile_e, n_active, xs, w1, w3, w2)


def _rank_kernel(sel_ref, tri_ref, rank_ref, cnt_ref, carry):
    i = pl.program_id(0)

    @pl.when(i == 0)
    def _():
        carry[...] = jnp.zeros(carry.shape, F32)

    s = sel_ref[...]
    rank_ref[...] = _dot(tri_ref[...], s) + carry[...]
    carry[...] += jnp.sum(s.astype(F32), axis=0, keepdims=True)
    cnt_ref[...] = jnp.broadcast_to(carry[...], cnt_ref.shape)


def _rank(sel, tri):
    T = sel.shape[0]
    return pl.pallas_call(
        _rank_kernel,
        grid=(T // TR,),
        in_specs=[pl.BlockSpec((TR, N_EXP_PAD), lambda i: (i, 0)),
                  pl.BlockSpec((TR, TR), lambda i: (0, 0))],
        out_specs=[pl.BlockSpec((TR, N_EXP_PAD), lambda i: (i, 0)),
                   pl.BlockSpec((8, N_EXP_PAD), lambda i: (0, 0))],
        out_shape=[jax.ShapeDtypeStruct((T, N_EXP_PAD), F32),
                   jax.ShapeDtypeStruct((8, N_EXP_PAD), F32)],
        scratch_shapes=[pltpu.VMEM((1, N_EXP_PAD), F32)],
        compiler_params=_cparams(("arbitrary",)),
        name="route_rank",
    )(sel, tri)


TILE_ROWS = TR * SUB


def _token_tile(ref, tok):
    return ref.at[pl.ds(pl.multiple_of(tok * SUB, SUB), SUB)]


def _dispatch_kernel(pos0_ref, pos1_ref, h_ref, xs_in, xs_hbm, stage, sem):
    del xs_in
    i = pl.program_id(0)
    slot = i % 2
    base = pl.multiple_of(slot * TILE_ROWS, TILE_ROWS)
    stage[pl.ds(base, TILE_ROWS), :] = h_ref[...]

    def issue(r, c):
        t = i * TR + r
        src = _token_tile(stage, slot * TR + r)
        pltpu.make_async_copy(src, _token_tile(xs_hbm, pos0_ref[t]), sem.at[slot]).start()
        pltpu.make_async_copy(src, _token_tile(xs_hbm, pos1_ref[t]), sem.at[slot]).start()
        return c
    lax.fori_loop(0, TR, issue, 0, unroll=8)

    def wait_slot(s):
        b = pl.multiple_of(s * TILE_ROWS, TILE_ROWS)
        for _ in range(2):
            pltpu.make_async_copy(stage.at[pl.ds(b, TILE_ROWS)], xs_hbm.at[pl.ds(0, TILE_ROWS)],
                                  sem.at[s]).wait()

    @pl.when(i > 0)
    def _():
        wait_slot(1 - slot)

    @pl.when(i == pl.num_programs(0) - 1)
    def _():
        wait_slot(slot)


def _dispatch(pos0, pos1, h2t, n_rows):
    T = h2t.shape[0] // SUB
    zeros = jnp.zeros((n_rows * SUB, LANES), F32)
    return pl.pallas_call(
        _dispatch_kernel,
        grid_spec=pltpu.PrefetchScalarGridSpec(
            num_scalar_prefetch=2, grid=(T // TR,),
            in_specs=[pl.BlockSpec((TILE_ROWS, LANES), lambda i, p0, p1: (i, 0)),
                      pl.BlockSpec(memory_space=pl.ANY)],
            out_specs=pl.BlockSpec(memory_space=pl.ANY),
            scratch_shapes=[pltpu.VMEM((2 * TILE_ROWS, LANES), F32),
                            pltpu.SemaphoreType.DMA((2,))]),
        out_shape=jax.ShapeDtypeStruct((n_rows * SUB, LANES), F32),
        input_output_aliases={3: 0},
        compiler_params=_cparams(("arbitrary",)),
        name="moe_dispatch",
    )(pos0, pos1, h2t, zeros)


def _combine_kernel(pos0_ref, pos1_ref, y_hbm, x1_ref, rw_ref, mod_ref, *rest, emit_next):
    if emit_next:
        modn_ref, gn_ref, o_ref, hn_ref, buf0, buf1, sem0, sem1 = rest
    else:
        o_ref, buf0, buf1, sem0, sem1 = rest
    i = pl.program_id(0)
    nt = pl.num_programs(0)

    def issue(tile, slot):
        def body(r, c):
            t = tile * TR + r
            pltpu.make_async_copy(_token_tile(y_hbm, pos0_ref[t]), _token_tile(buf0, slot * TR + r),
                                  sem0.at[slot]).start()
            pltpu.make_async_copy(_token_tile(y_hbm, pos1_ref[t]), _token_tile(buf1, slot * TR + r),
                                  sem1.at[slot]).start()
            return c
        lax.fori_loop(0, TR, body, 0, unroll=8)

    @pl.when(i == 0)
    def _():
        issue(0, 0)

    @pl.when(i + 1 < nt)
    def _():
        issue(i + 1, (i + 1) % 2)

    slot = i % 2
    base = pl.multiple_of(slot * TILE_ROWS, TILE_ROWS)
    for buf, sem in ((buf0, sem0), (buf1, sem1)):
        pltpu.make_async_copy(y_hbm.at[pl.ds(0, TILE_ROWS)], buf.at[pl.ds(base, TILE_ROWS)],
                              sem.at[slot]).wait()
    rw = rw_ref[...]
    w0 = rw[:, 0:1]
    w1 = rw[:, 1:2]
    for j in range(SUB):
        sl = slice(LANES * j, LANES * (j + 1))
        f = (w0 * buf0[pl.ds(base + j, TR, stride=SUB), :]
             + w1 * buf1[pl.ds(base + j, TR, stride=SUB), :])
        o_ref[:, sl] = x1_ref[:, sl] + mod_ref[:, 5 * D_MODEL + LANES * j:5 * D_MODEL + LANES * (j + 1)] * f
    if emit_next:
        hn_ref[...] = _pre_norm1(o_ref[...], modn_ref, gn_ref)


def _combine(pos0, pos1, ys, x1, rw, mod4, n1g, l, t_ctx, dec_seq, emit_next):
    T = x1.shape[0]
    row = functools.partial(_mod_row, tm=TR, t_ctx=t_ctx, dec_seq=dec_seq)
    rows = pl.BlockSpec((TR, D_MODEL), lambda i, p0, p1: (i, 0))
    in_specs = [pl.BlockSpec(memory_space=pl.ANY), rows,
                pl.BlockSpec((TR, N_EXP_PAD), lambda i, p0, p1: (i, 0)),
                pl.BlockSpec((None, None, 1, 6 * D_MODEL), lambda i, p0, p1: (l, row(i), 0, 0))]
    args = [pos0, pos1, ys, x1, rw, mod4]
    out_specs = [rows]
    out_shape = [jax.ShapeDtypeStruct((T, D_MODEL), F32)]
    if emit_next:
        in_specs += [pl.BlockSpec((None, None, 1, 6 * D_MODEL), lambda i, p0, p1: (l + 1, row(i), 0, 0)),
                     _layer(n1g, l + 1)]
        args += [mod4, n1g]
        out_specs += [rows]
        out_shape += [jax.ShapeDtypeStruct((T, D_MODEL), BF16)]
    return pl.pallas_call(
        functools.partial(_combine_kernel, emit_next=emit_next),
        grid_spec=pltpu.PrefetchScalarGridSpec(
            num_scalar_prefetch=2, grid=(T // TR,),
            in_specs=in_specs, out_specs=out_specs,
            scratch_shapes=[pltpu.VMEM((2 * TILE_ROWS, LANES), F32), pltpu.VMEM((2 * TILE_ROWS, LANES), F32),
                            pltpu.SemaphoreType.DMA((2,)), pltpu.SemaphoreType.DMA((2,))]),
        out_shape=out_shape,
        compiler_params=_cparams(("arbitrary",)),
        name="moe_combine",
    )(*args)


def _block_ones(n, seg):
    idx = np.arange(n) // seg
    return jnp.asarray((idx[:, None] == idx[None, :]).astype(np.float32), dtype=BF16)


def _rope_tables(seq, head_w, rot_dim, n_heads):
    rows = seq // GRID_W
    nf = rot_dim // 4
    freqs = ROPE_THETA ** (-np.arange(nf, dtype=np.float64) / nf)
    row = np.repeat(np.arange(rows, dtype=np.float64), GRID_W)
    col = np.tile(np.arange(GRID_W, dtype=np.float64), rows)
    ang = np.concatenate([row[:, None] * freqs, col[:, None] * freqs], axis=-1)
    ang = np.repeat(ang, 2, axis=-1)
    even = (np.arange(rot_dim) % 2 == 0)[None, :]
    c = np.ones((seq, head_w))
    se = np.zeros((seq, head_w))
    so = np.zeros((seq, head_w))
    c[:, :rot_dim] = np.cos(ang)
    se[:, :rot_dim] = np.where(even, -np.sin(ang), 0.0)
    so[:, :rot_dim] = np.where(even, 0.0, np.sin(ang))
    return tuple(jnp.asarray(np.tile(t, (1, n_heads)), dtype=F32) for t in (c, se, so))


def _pool_mask():
    m = np.zeros((2 * D_PAD, D_CH), np.float32)
    for gi, w in enumerate(POOL):
        left = w // 2
        right = w - 1 - left
        for off in range(-left, right + 1):
            m[off + D_PAD, gi * D_G:(gi + 1) * D_G] = 1.0
    return jnp.asarray(m)


def _b_heads(nope, rope):
    ref = nope if nope is not None else rope
    lead = ref.shape[:-1]
    z = lambda w: jnp.zeros(lead + (w,), ref.dtype)
    parts = [rope if rope is not None else z(B_ROPE), nope if nope is not None else z(B_NOPE),
             z(B_HEAD_PAD - B_QK)]
    out = jnp.concatenate(parts, axis=-1)
    return out.reshape(lead[:-1] + (lead[-1] * B_HEAD_PAD,))


def kernel(x_prompt, x_sample, cache_a_k, cache_a_v, cache_b_ckv, cache_b_kpe, c, c_ctx, ada_w, ada_b, norm1_g, norm2_g, w_in, a_q_norm, a_k_norm, a_lambda, a_sub_norm, a_w_o, b_q_lora_norm, b_kv_lora_norm, b_w_uq, b_w_ukv, b_q_norm, b_k_norm, b_w_o, c_dw, c_dw_b, c_ln_g, c_ln_b, c_w_o, d_w_group, d_scale, d_w_o, w_out, ffn_w1, ffn_w3, ffn_w2, moe_router, moe_w1, moe_w3, moe_w2):
    nb, seq, _ = x_prompt.shape
    db, dseq, _ = x_sample.shape
    L = w_in.shape[0]
    past = cache_a_k.shape[2]
    t_ctx = nb * seq
    t_lat = db * dseq
    T = t_ctx + t_lat
    na = 2 * A_HEADS * A_QK
    assert t_ctx % TG == 0 and dseq % TG == 0 and seq % TR == 0 and db + 1 <= 16

    x = jnp.concatenate([x_prompt.reshape(t_ctx, D_MODEL), x_sample.reshape(t_lat, D_MODEL)], axis=0)
    cv = jnp.concatenate([c_ctx[None, :], c, jnp.zeros((15 - db, D_MODEL), F32)], axis=0)
    mod4 = _ada_all(cv, ada_w, ada_b).reshape(L, 16, 1, 6 * D_MODEL)

    bd_a = _block_ones(na, A_QK)
    bd_b = _block_ones(B_HEADS * B_HEAD_PAD, B_HEAD_PAD)
    rope_a = _rope_tables(dseq, A_QK, A_QK, 2 * A_HEADS)
    rope_b = _rope_tables(dseq, B_HEAD_PAD, B_ROPE, B_HEADS)
    pmask = _pool_mask()
    tri = jnp.asarray(np.tril(np.ones((TR, TR), np.float32), -1), dtype=BF16)

    row1 = lambda a: a.reshape(L, 1, -1)
    w_a, w_b, w_cd, w_g = _repack(w_in)
    gq_a = row1(jnp.tile(a_q_norm, (1, 2 * A_HEADS)))
    gk_a = row1(jnp.tile(a_k_norm, (1, 2 * A_HEADS)))
    uq = b_w_uq.reshape(L, B_QL, B_HEADS, B_QK)
    wuq = _b_heads(uq[..., :B_NOPE], uq[..., B_NOPE:]).astype(BF16)
    ukv = b_w_ukv.reshape(L, B_KVL, B_HEADS, B_NOPE + B_V)
    wuk = _b_heads(ukv[..., :B_NOPE], None).astype(BF16)
    wuv = ukv[..., B_NOPE:].reshape(L, B_KVL, B_HEADS * B_V).astype(BF16)
    gq_b = row1(jnp.tile(_b_heads(b_q_norm[:, None, :B_NOPE], b_q_norm[:, None, B_NOPE:]), (1, B_HEADS)))
    gk_b = row1(jnp.tile(_b_heads(b_k_norm[:, None, :B_NOPE], b_k_norm[:, None, B_NOPE:]), (1, B_HEADS)))
    bdd = jnp.zeros((L, D_CH, D_CH), F32)
    for gi in range(len(POOL)):
        bdd = bdd.at[:, gi * D_G:(gi + 1) * D_G, gi * D_G:(gi + 1) * D_G].set(d_w_group[:, gi])
    bdd = bdd.astype(BF16)
    a_wo, b_wo, c_wo, d_wo, wo = (w.astype(BF16) for w in (a_w_o, b_w_o, c_w_o, d_w_o, w_out))
    ffn1, ffn3, ffn2 = (w.astype(BF16) for w in (ffn_w1, ffn_w3, ffn_w2))
    r_pad = jnp.pad(moe_router, ((0, 0), (0, 0), (0, N_EXP_PAD - N_EXP)))
    r_hi = r_pad.astype(BF16)
    r_lo = (r_pad - r_hi.astype(F32)).astype(BF16)
    n1g, n2g = row1(norm1_g), row1(norm2_g)
    g_sub = row1(a_sub_norm)
    gql, gkvl = row1(b_q_lora_norm), row1(b_kv_lora_norm)
    dwb, lng, lnb, dsc = row1(c_dw_b), row1(c_ln_g), row1(c_ln_b), row1(d_scale)

    ck_a = cache_a_k.reshape(db, L, past, na).astype(BF16)
    cv_a = cache_a_v.reshape(db, L, past, A_HEADS * A_V).astype(BF16)
    kpe_pad = jnp.pad(cache_b_kpe, ((0, 0), (0, 0), (0, 0), (0, LANES - B_ROPE)))
    ck_b, cv_b = _cache_b(cache_b_ckv, kpe_pad, wuk, wuv, gk_b, bd_b)

    n_rows = 2 * T + N_EXP * TG
    nt_g = n_rows // TG

    new_ak, new_av, new_ckv, new_kpe = [], [], [], []
    h1 = _prep(x, mod4, n1g, 0, t_ctx, dseq)
    for l in range(L):
        last = l == L - 1

        q1c, q2c, kc, vc, nk, nv = _proj_a(h1, w_a, gq_a, gk_a, bd_a, None, l, 0, t_ctx, False)
        q1l, q2l, kl, vl = _proj_a(h1, w_a, gq_a, gk_a, bd_a, rope_a, l, t_ctx, t_lat, True)
        oa = (_attn_a(q1c, q2c, kc, vc, None, None, a_lambda, g_sub, l, seq, False),
              _attn_a(q1l, q2l, kl, vl, ck_a, cv_a, a_lambda, g_sub, l, dseq, True))
        new_ak.append(nk)
        new_av.append(nv)

        bargs = (w_b, gql, gkvl, wuq, wuk, wuv, gq_b, gk_b, bd_b)
        qc, kc, vc, nckv, nkpe = _proj_b(h1, *bargs, None, l, 0, t_ctx, False)
        ql, kl, vl = _proj_b(h1, *bargs, rope_b, l, t_ctx, t_lat, True)
        ob = (_attn_b(qc, kc, vc, None, None, l, seq, False),
              _attn_b(ql, kl, vl, ck_b, cv_b, l, dseq, True))
        new_ckv.append(nckv)
        new_kpe.append(nkpe[:, 0:B_ROPE])

        cdargs = (w_cd, c_dw, dwb, lng, lnb, bdd, dsc, pmask)
        occ, odc = _mix_cd(h1, *cdargs, l, 0, t_ctx, seq)
        ocl, odl = _mix_cd(h1, *cdargs, l, t_ctx, t_lat, dseq)

        j = l // 2
        moe = l % 2 == 1
        router = (r_hi, r_lo) if moe else None
        outs = _merge(x, h1, (oa, ob, (occ, ocl), (odc, odl)), mod4, n2g, w_g, a_wo, b_wo,
                      c_wo, d_wo, wo, router, l, j, t_ctx, dseq)
        if not moe:
            x1, h2 = outs
            assert not last, "a dense layer is always followed by another layer"
            x, h1 = _ffn_dense(h2, ffn1, ffn3, ffn2, x1, mod4, n1g, l, j, t_ctx, dseq)
        else:
            x1, h2t, ridx, rw, sel = outs
            rank, cnt = _rank(sel, tri)
            counts = cnt[0, :N_EXP].astype(I32)
            padded = ((counts + TG - 1) // TG) * TG
            ends = jnp.cumsum(padded)
            offs = ends - padded
            pos_all = offs[None, :] + rank[:, :N_EXP].astype(I32)
            pos0 = jnp.take_along_axis(pos_all, ridx[:, 0:1], axis=1)[:, 0]
            pos1 = jnp.take_along_axis(pos_all, ridx[:, 1:2], axis=1)[:, 0]
            tile_start = jnp.arange(nt_g, dtype=I32) * TG
            tile_e = jnp.minimum(jnp.sum(tile_start[:, None] >= ends[None, :], axis=1), N_EXP - 1).astype(I32)
            n_active = (ends[-1] // TG).astype(I32).reshape(1)
            last_e = tile_e[jnp.maximum(n_active[0] - 1, 0)]
            tile_e = jnp.where(tile_start < ends[-1], tile_e, last_e)
            xs = _dispatch(pos0, pos1, h2t, n_rows)
            ys = _ffn_experts(xs, moe_w1, moe_w3, moe_w2, tile_e, n_active, j)
            outs = _combine(pos0, pos1, ys, x1, rw, mod4, n1g, l, t_ctx, dseq, not last)
            x, h1 = (outs[0], None) if last else outs

    y_prompt = x[:t_ctx].reshape(nb, seq, D_MODEL)
    y_sample = x[t_ctx:].reshape(db, dseq, D_MODEL)
    new_a_k = jnp.stack(new_ak, axis=0).reshape(L, nb, seq, 2 * A_HEADS, A_QK).transpose(1, 0, 2, 3, 4)
    new_a_v = jnp.stack(new_av, axis=0).reshape(L, nb, seq, A_HEADS, A_V).transpose(1, 0, 2, 3, 4)
    new_b_ckv = jnp.stack(new_ckv, axis=0).reshape(L, nb, seq, B_KVL).transpose(1, 0, 2, 3)
    new_b_kpe = jnp.stack(new_kpe, axis=0).reshape(L, nb, seq, B_ROPE).transpose(1, 0, 2, 3)
    return (y_prompt, y_sample, new_a_k, new_a_v, new_b_ckv, new_b_kpe)
```

```python
import functools
import math

import numpy as np
import jax
import jax.numpy as jnp
from jax import lax
from jax.experimental import pallas as pl
from jax.experimental.pallas import tpu as pltpu

F32 = jnp.float32
BF16 = jnp.bfloat16
I32 = jnp.int32

EPS = 1e-6
D_MODEL = 1024
GRID_W = 64
ROPE_THETA = 10000.0
A_HEADS = 4
A_QK = 64
A_V = 128
B_HEADS = 4
B_NOPE = 64
B_ROPE = 32
B_QK = B_NOPE + B_ROPE
B_V = 64
B_QL = 256
B_KVL = 128
B_HEAD_PAD = 128
C_CH = 256
C_W = 31
C_PAD = 16
POOL = (2, 4, 8, 16)
D_G = 64
D_CH = D_G * len(POOL)
D_PAD = 8
N_EXP = 8
N_EXP_PAD = 128
LANES = 128
SUB = 8

O_AQ, O_AK, O_AV, O_BQ, O_BKV, O_KPE, O_C, O_D, O_G, O_END = (
    0, 512, 1024, 1536, 1792, 1920, 1952, 2464, 2720, 6816)
W_B_COLS = (O_KPE - O_BQ) + LANES

TM = 512
TQ = 512
TG = 1024
TR = 256
TF_DENSE = 1408
TF_MOE = 512
VMEM_LIMIT = 56 * 1024 * 1024


def _cparams(sem):
    return pltpu.CompilerParams(dimension_semantics=sem, vmem_limit_bytes=VMEM_LIMIT)


def _dot(a, b):
    return jnp.dot(a, b, preferred_element_type=F32)


def _dot_nt(a, b):
    return lax.dot_general(a, b, (((1,), (1,)), ((), ())), preferred_element_type=F32)


def _bf(x):
    return x.astype(BF16)


def _rms(x, g):
    return x * lax.rsqrt(jnp.mean(x * x, axis=-1, keepdims=True) + EPS) * g


def _seg_sum_sq(x, bd_ref):
    return _dot(_bf(x * x), bd_ref[...])


def _rope(x, c_ref, se_ref, so_ref):
    n = x.shape[-1]
    return (x * c_ref[...] + pltpu.roll(x, n - 1, 1) * se_ref[...]
            + pltpu.roll(x, 1, 1) * so_ref[...])


def _mod_row(i, tm, t_ctx, dec_seq):
    r = i * tm
    return jnp.where(r < t_ctx, 0, 1 + (r - t_ctx) // dec_seq)


def _layer(a, l):
    nd = a.ndim
    return pl.BlockSpec((None,) + a.shape[1:], lambda *_: (l,) + (0,) * (nd - 1))


def _whole(a):
    nd = a.ndim
    return pl.BlockSpec(a.shape, lambda *_: (0,) * nd)


def _ada_kernel(cv_ref, w_ref, b_ref, o_ref):
    cv = cv_ref[...]
    s = cv * jax.nn.sigmoid(cv)
    o_ref[...] = _dot(_bf(s), _bf(w_ref[...])) + b_ref[...]


def _ada_all(cv, ada_w, ada_b):
    L, d, n = ada_w.shape
    tn = 1536
    return pl.pallas_call(
        _ada_kernel,
        grid=(L, n // tn),
        in_specs=[pl.BlockSpec((16, d), lambda l, j: (0, 0)),
                  pl.BlockSpec((None, d, tn), lambda l, j: (l, 0, j)),
                  pl.BlockSpec((None, 1, tn), lambda l, j: (l, 0, j))],
        out_specs=pl.BlockSpec((None, 16, tn), lambda l, j: (l, 0, j)),
        out_shape=jax.ShapeDtypeStruct((L, 16, n), F32),
        compiler_params=_cparams(("parallel", "parallel")),
        name="ada_mod",
    )(cv, ada_w, ada_b.reshape(L, 1, n))


def _repack_kernel(w_ref, wa_ref, wb_ref, wcd_ref, wg_ref):
    wa_ref[...] = _bf(w_ref[:, O_AQ:O_BQ])
    wb_ref[:, 0:O_KPE - O_BQ] = _bf(w_ref[:, O_BQ:O_KPE])
    blk = w_ref[:, O_KPE:O_KPE + LANES]
    lane = lax.broadcasted_iota(I32, blk.shape, 1)
    wb_ref[:, O_KPE - O_BQ:W_B_COLS] = _bf(jnp.where(lane < B_ROPE, blk, 0.0))
    wcd_ref[...] = _bf(w_ref[:, O_C:O_G])
    wg_ref[...] = _bf(w_ref[:, O_G:O_END])


def _repack(w_in):
    L, d, n = w_in.shape
    tr = 256
    widths = (O_BQ - O_AQ, W_B_COLS, O_G - O_C, O_END - O_G)
    return pl.pallas_call(
        _repack_kernel,
        grid=(L, d // tr),
        in_specs=[pl.BlockSpec((None, tr, n), lambda l, i: (l, i, 0))],
        out_specs=[pl.BlockSpec((None, tr, w), lambda l, i: (l, i, 0)) for w in widths],
        out_shape=[jax.ShapeDtypeStruct((L, d, w), BF16) for w in widths],
        compiler_params=_cparams(("parallel", "parallel")),
        name="repack_w_in",
    )(w_in)


def _prep_kernel(x_ref, mod_ref, g_ref, h_ref):
    h_ref[...] = _pre_norm1(x_ref[...], mod_ref, g_ref)


def _prep(x, mod4, g, l, t_ctx, dec_seq):
    T = x.shape[0]
    row = functools.partial(_mod_row, tm=TM, t_ctx=t_ctx, dec_seq=dec_seq)
    return pl.pallas_call(
        _prep_kernel,
        grid=(T // TM,),
        in_specs=[pl.BlockSpec((TM, D_MODEL), lambda i: (i, 0)),
                  pl.BlockSpec((None, None, 1, 6 * D_MODEL), lambda i: (l, row(i), 0, 0)),
                  _layer(g, l)],
        out_specs=pl.BlockSpec((TM, D_MODEL), lambda i: (i, 0)),
        out_shape=jax.ShapeDtypeStruct((T, D_MODEL), BF16),
        compiler_params=_cparams(("parallel",)),
        name="prep",
    )(x, mod4, g)


def _proj_a_kernel(*refs, latent):
    if latent:
        (h_ref, w_ref, gq_ref, gk_ref, bd_ref, c_ref, se_ref, so_ref,
         q1_ref, q2_ref, k_ref, v_ref) = refs
    else:
        (h_ref, w_ref, gq_ref, gk_ref, bd_ref, _, _,
         q1_ref, q2_ref, k_ref, v_ref, nk_ref, nv_ref) = refs
    p = _dot(h_ref[...], w_ref[...])
    n = 2 * A_HEADS * A_QK
    q = p[:, 0:n]
    k = p[:, n:2 * n]
    v = p[:, 2 * n:3 * n]
    q = q * lax.rsqrt(_seg_sum_sq(q, bd_ref) * (1.0 / A_QK) + EPS) * gq_ref[...]
    k = k * lax.rsqrt(_seg_sum_sq(k, bd_ref) * (1.0 / A_QK) + EPS) * gk_ref[...]
    if latent:
        q = _rope(q, c_ref, se_ref, so_ref)
        k = _rope(k, c_ref, se_ref, so_ref)
    else:
        nk_ref[...] = k.reshape(nk_ref.shape)
        nv_ref[...] = v.reshape(nv_ref.shape)
    q = q * (A_QK ** -0.5 * LOG2E)
    lane = lax.broadcasted_iota(I32, q.shape, 1)
    first = (lane % (2 * A_QK)) < A_QK
    q1_ref[...] = _bf(jnp.where(first, q, 0.0))
    q2_ref[...] = _bf(jnp.where(first, 0.0, q))
    k_ref[...] = _bf(k)
    v_ref[...] = _bf(v)


def _cache_out_spec(buf, l):
    nb, _, seq, w = buf.shape
    return pl.BlockSpec((TM // seq, None, seq, w), lambda i: (i, l, 0, 0))


def _proj_a(h1, w_a, gq, gk, bd, rope_tabs, l, row0, rows, latent, cache_out=None):
    n = 2 * A_HEADS * A_QK
    b0 = row0 // TM
    tab_blocks = rope_tabs[0].shape[0] // TM if latent else 1
    in_specs = [pl.BlockSpec((TM, D_MODEL), lambda i: (i + b0, 0)),
                _layer(w_a, l), _layer(gq, l), _layer(gk, l), _whole(bd)]
    args = [h1, w_a, gq, gk, bd]
    out_spec = pl.BlockSpec((TM, n), lambda i: (i, 0))
    out_shape = [jax.ShapeDtypeStruct((rows, n), BF16)] * 4
    out_specs = [out_spec] * 4
    aliases = {}
    if latent:
        in_specs += [pl.BlockSpec((TM, n), lambda i: (i % tab_blocks, 0))] * 3
        args += list(rope_tabs)
    else:
        for buf in cache_out:
            aliases[len(args)] = len(out_shape)
            in_specs.append(pl.BlockSpec(memory_space=pl.ANY))
            args.append(buf)
            out_shape.append(jax.ShapeDtypeStruct(buf.shape, buf.dtype))
            out_specs.append(_cache_out_spec(buf, l))
    return pl.pallas_call(
        functools.partial(_proj_a_kernel, latent=latent),
        grid=(rows // TM,),
        in_specs=in_specs, out_specs=out_specs, out_shape=out_shape,
        input_output_aliases=aliases,
        compiler_params=_cparams(("parallel",)),
        name="proj_a_lat" if latent else "proj_a_ctx",
    )(*args)


LOG2E = math.log2(math.e)


def _attend(qs, kss, vs):
    es, rs = [], []
    for q, ks in zip(qs, kss):
        scores = [_dot_nt(q, k) for k in ks]
        m = functools.reduce(jnp.maximum, [jnp.max(s, axis=-1, keepdims=True) for s in scores])
        e = [jnp.exp2(s - m) for s in scores]
        l = functools.reduce(lambda a, b: a + b, [jnp.sum(x, axis=-1, keepdims=True) for x in e])
        es.append([_bf(x) for x in e])
        rs.append(1.0 / l)
    o = None
    for p, v in enumerate(vs):
        stacked = es[0][p] if len(qs) == 1 else jnp.concatenate([e[p] for e in es], axis=0)
        t = _dot(stacked, v)
        o = t if o is None else o + t
    outs, r0 = [], 0
    for q, r in zip(qs, rs):
        outs.append(o[r0:r0 + q.shape[0]] * r)
        r0 += q.shape[0]
    return outs


def _attn_a_kernel(*refs, latent, lam_init):
    if latent:
        q1_ref, q2_ref, k_ref, v_ref, kc_ref, vc_ref, lam_ref, g_ref, o_ref = refs
    else:
        q1_ref, q2_ref, k_ref, v_ref, lam_ref, g_ref, o_ref = refs
    lm = lam_ref[...]
    lam = (jnp.exp(jnp.sum(lm[0:1] * lm[1:2], axis=-1, keepdims=True))
           - jnp.exp(jnp.sum(lm[2:3] * lm[3:4], axis=-1, keepdims=True)) + lam_init)
    for h in range(A_HEADS):
        sl = slice(A_V * h, A_V * (h + 1))
        ks = [k_ref[:, sl]]
        vs = [v_ref[:, sl]]
        if latent:
            ks.append(kc_ref[:, sl])
            vs.append(vc_ref[:, sl])
        tq = q1_ref.shape[0]
        oc, = _attend([jnp.concatenate([q1_ref[:, sl], q2_ref[:, sl]], axis=0)], [ks], vs)
        o = oc[0:tq] - lam * oc[tq:2 * tq]
        o = _rms(o, g_ref[...]) * (1.0 - lam_init)
        o_ref[:, sl] = _bf(o)


def _attn_a(q1, q2, k, v, cache_k, cache_v, a_lambda, g_sub, layer, seq, latent):
    rows, n = q1.shape
    nb = rows // seq
    lam_init = 0.8 - 0.6 * math.exp(-0.3 * layer)
    kern = functools.partial(_attn_a_kernel, latent=latent, lam_init=lam_init)
    if latent:
        nq = seq // TQ
        past = cache_k.shape[2]
        grid = (nb, nq)
        qs = pl.BlockSpec((TQ, n), lambda b, j: (b * nq + j, 0))
        kv = pl.BlockSpec((seq, n), lambda b, j: (b, 0))
        cs = pl.BlockSpec((None, None, past, n), lambda b, j: (b, layer, 0, 0))
        in_specs = [qs, qs, kv, kv, cs, cs, _layer(a_lambda, layer), _layer(g_sub, layer)]
        args = (q1, q2, k, v, cache_k, cache_v, a_lambda, g_sub)
        sem = ("parallel", "parallel")
        out_spec = qs
    else:
        grid = (nb,)
        bs = pl.BlockSpec((seq, n), lambda b: (b, 0))
        in_specs = [bs, bs, bs, bs, _layer(a_lambda, layer), _layer(g_sub, layer)]
        args = (q1, q2, k, v, a_lambda, g_sub)
        sem = ("parallel",)
        out_spec = bs
    return pl.pallas_call(
        kern, grid=grid, in_specs=in_specs, out_specs=out_spec,
        out_shape=jax.ShapeDtypeStruct((rows, n), BF16),
        compiler_params=_cparams(sem),
        name="attn_a_lat" if latent else "attn_a_ctx",
    )(*args)


def _mla_keys(ckv, kpe, wuk_ref, wuv_ref, gk_ref, bd_ref):
    cb = _bf(ckv)
    kn = _dot(cb, wuk_ref[...]) + jnp.concatenate([kpe] * B_HEADS, axis=1)
    k = kn * lax.rsqrt(_seg_sum_sq(kn, bd_ref) * (1.0 / B_QK) + EPS) * gk_ref[...]
    v = _dot(cb, wuv_ref[...])
    return k, v


def _proj_b_kernel(*refs, latent):
    if latent:
        (h_ref, w_ref, gql_ref, gkvl_ref, wuq_ref, wuk_ref, wuv_ref, gq_ref, gk_ref, bd_ref,
         c_ref, se_ref, so_ref, q_ref, k_ref, v_ref) = refs
    else:
        (h_ref, w_ref, gql_ref, gkvl_ref, wuq_ref, wuk_ref, wuv_ref, gq_ref, gk_ref, bd_ref, _, _,
         q_ref, k_ref, v_ref, nckv_ref, nkpe_ref) = refs
    p = _dot(h_ref[...], w_ref[...])
    bq = p[:, 0:B_QL]
    bkv = p[:, B_QL:B_QL + B_KVL]
    kpe = p[:, B_QL + B_KVL:B_QL + B_KVL + LANES]
    q = _dot(_bf(_rms(bq, gql_ref[...])), wuq_ref[...])
    q = q * lax.rsqrt(_seg_sum_sq(q, bd_ref) * (1.0 / B_QK) + EPS) * gq_ref[...]
    ckv = _rms(bkv, gkvl_ref[...])
    k, v = _mla_keys(ckv, kpe, wuk_ref, wuv_ref, gk_ref, bd_ref)
    if latent:
        q = _rope(q, c_ref, se_ref, so_ref)
        k = _rope(k, c_ref, se_ref, so_ref)
    else:
        nckv_ref[...] = ckv.reshape(nckv_ref.shape)
        nkpe_ref[...] = kpe[:, 0:B_ROPE].reshape(nkpe_ref.shape)
    q_ref[...] = _bf(q * (B_QK ** -0.5 * LOG2E))
    k_ref[...] = _bf(k)
    v_ref[...] = _bf(v)


def _proj_b(h1, w_b, gql, gkvl, wuq, wuk, wuv, gq, gk, bd, rope_tabs, l, row0, rows, latent,
            cache_out=None):
    n = B_HEADS * B_HEAD_PAD
    nv = B_HEADS * B_V
    b0 = row0 // TM
    tab_blocks = rope_tabs[0].shape[0] // TM if latent else 1
    in_specs = [pl.BlockSpec((TM, D_MODEL), lambda i: (i + b0, 0))] + [
        _layer(a, l) for a in (w_b, gql, gkvl, wuq, wuk, wuv, gq, gk)] + [_whole(bd)]
    args = [h1, w_b, gql, gkvl, wuq, wuk, wuv, gq, gk, bd]
    row = lambda w: pl.BlockSpec((TM, w), lambda i: (i, 0))
    out_shape = [jax.ShapeDtypeStruct((rows, n), BF16), jax.ShapeDtypeStruct((rows, n), BF16),
                 jax.ShapeDtypeStruct((rows, nv), BF16)]
    out_specs = [row(n), row(n), row(nv)]
    aliases = {}
    if latent:
        in_specs += [pl.BlockSpec((TM, n), lambda i: (i % tab_blocks, 0))] * 3
        args += list(rope_tabs)
    else:
        for buf in cache_out:
            aliases[len(args)] = len(out_shape)
            in_specs.append(pl.BlockSpec(memory_space=pl.ANY))
            args.append(buf)
            out_shape.append(jax.ShapeDtypeStruct(buf.shape, buf.dtype))
            out_specs.append(_cache_out_spec(buf, l))
    return pl.pallas_call(
        functools.partial(_proj_b_kernel, latent=latent),
        grid=(rows // TM,),
        in_specs=in_specs, out_specs=out_specs, out_shape=out_shape,
        input_output_aliases=aliases,
        compiler_params=_cparams(("parallel",)),
        name="proj_b_lat" if latent else "proj_b_ctx",
    )(*args)


def _cache_b_kernel(ckv_ref, kpe_ref, wuk_ref, wuv_ref, gk_ref, bd_ref, k_ref, v_ref):
    k, v = _mla_keys(ckv_ref[...], kpe_ref[...], wuk_ref, wuv_ref, gk_ref, bd_ref)
    k_ref[...] = _bf(k)
    v_ref[...] = _bf(v)


def _cache_b(ckv, kpe_pad, wuk, wuv, gk, bd):
    db, L, past, _ = ckv.shape
    n = B_HEADS * B_HEAD_PAD
    nv = B_HEADS * B_V
    blk = lambda w: pl.BlockSpec((None, None, past, w), lambda l, b: (b, l, 0, 0))
    wl = lambda a: pl.BlockSpec((None,) + a.shape[1:], lambda l, b: (l, 0, 0))
    return pl.pallas_call(
        _cache_b_kernel,
        grid=(L, db),
        in_specs=[blk(B_KVL), blk(LANES), wl(wuk), wl(wuv), wl(gk),
                  pl.BlockSpec(bd.shape, lambda l, b: (0, 0))],
        out_specs=[blk(n), blk(nv)],
        out_shape=[jax.ShapeDtypeStruct((db, L, past, n), BF16),
                   jax.ShapeDtypeStruct((db, L, past, nv), BF16)],
        compiler_params=_cparams(("parallel", "parallel")),
        name="cache_b_expand",
    )(ckv, kpe_pad, wuk, wuv, gk, bd)


def _attn_b_kernel(*refs, latent):
    if latent:
        q_ref, k_ref, v_ref, kc_ref, vc_ref, o_ref = refs
    else:
        q_ref, k_ref, v_ref, o_ref = refs
    lane = lax.broadcasted_iota(I32, (q_ref.shape[0], 2 * B_V), 1)
    for hp in range(B_HEADS // 2):
        vsl = slice(2 * B_V * hp, 2 * B_V * (hp + 1))
        vs = [v_ref[:, vsl]] + ([vc_ref[:, vsl]] if latent else [])
        qs, kss = [], []
        for h in (2 * hp, 2 * hp + 1):
            sl = slice(B_HEAD_PAD * h, B_HEAD_PAD * (h + 1))
            qs.append(q_ref[:, sl])
            kss.append([k_ref[:, sl]] + ([kc_ref[:, sl]] if latent else []))
        outs = _attend(qs, kss, vs)
        o_ref[:, vsl] = _bf(jnp.where(lane < B_V, outs[0], outs[1]))


def _attn_b(q, k, v, cache_k, cache_v, layer, seq, latent):
    rows, n = q.shape
    nv = v.shape[1]
    nb = rows // seq
    kern = functools.partial(_attn_b_kernel, latent=latent)
    if latent:
        nq = seq // TQ
        past = cache_k.shape[2]
        grid = (nb, nq)
        in_specs = [pl.BlockSpec((TQ, n), lambda b, j: (b * nq + j, 0)),
                    pl.BlockSpec((seq, n), lambda b, j: (b, 0)),
                    pl.BlockSpec((seq, nv), lambda b, j: (b, 0)),
                    pl.BlockSpec((None, None, past, n), lambda b, j: (b, layer, 0, 0)),
                    pl.BlockSpec((None, None, past, nv), lambda b, j: (b, layer, 0, 0))]
        args = (q, k, v, cache_k, cache_v)
        out_spec = pl.BlockSpec((TQ, nv), lambda b, j: (b * nq + j, 0))
        sem = ("parallel", "parallel")
    else:
        grid = (nb,)
        in_specs = [pl.BlockSpec((seq, n), lambda b: (b, 0)),
                    pl.BlockSpec((seq, n), lambda b: (b, 0)),
                    pl.BlockSpec((seq, nv), lambda b: (b, 0))]
        args = (q, k, v)
        out_spec = pl.BlockSpec((seq, nv), lambda b: (b, 0))
        sem = ("parallel",)
    return pl.pallas_call(
        kern, grid=grid, in_specs=in_specs, out_specs=out_spec,
        out_shape=jax.ShapeDtypeStruct((rows, nv), BF16),
        compiler_params=_cparams(sem),
        name="attn_b_lat" if latent else "attn_b_ctx",
    )(*args)


CONV_CHUNK = 64


def _mix_cd_kernel(h_ref, w_ref, dw_ref, dwb_ref, lng_ref, lnb_ref, bdd_ref, dsc_ref, pm_ref, pcnt_ref,
                   oc_ref, od_ref, gpad, dpad, *, seq):
    p = _dot(h_ref[...], w_ref[...])
    glu = p[:, 0:C_CH] * jax.nn.sigmoid(p[:, C_CH:2 * C_CH])
    gpad[0:C_PAD, :] = jnp.zeros((C_PAD, C_CH), F32)
    gpad[C_PAD + seq:2 * C_PAD + seq, :] = jnp.zeros((C_PAD, C_CH), F32)
    gpad[C_PAD:C_PAD + seq, :] = glu
    half = C_W // 2
    for c0 in range(0, seq, CONV_CHUNK):
        acc = jnp.zeros((CONV_CHUNK, C_CH), F32) + dwb_ref[...]
        for j in range(C_W):
            s = c0 + C_PAD - half + j
            acc = acc + gpad[s:s + CONV_CHUNK, :] * dw_ref[j:j + 1, :]
        mu = jnp.mean(acc, axis=-1, keepdims=True)
        xc = acc - mu
        y = xc * lax.rsqrt(jnp.mean(xc * xc, axis=-1, keepdims=True) + EPS)
        y = y * lng_ref[...] + lnb_ref[...]
        oc_ref[c0:c0 + CONV_CHUNK, :] = _bf(y * jax.nn.sigmoid(y))
    d = p[:, 2 * C_CH:2 * C_CH + D_CH]
    dpad[0:D_PAD, :] = jnp.zeros((D_PAD, D_CH), F32)
    dpad[D_PAD + seq:2 * D_PAD + seq, :] = jnp.zeros((D_PAD, D_CH), F32)
    dpad[D_PAD:D_PAD + seq, :] = d
    for c0 in range(0, seq, CONV_CHUNK):
        acc = jnp.zeros((CONV_CHUNK, D_CH), F32)
        for j in range(2 * D_PAD):
            acc = acc + dpad[c0 + j:c0 + j + CONV_CHUNK, :] * pm_ref[j:j + 1, :]
        pooled = acc / pcnt_ref[c0:c0 + CONV_CHUNK, :]
        diff = pooled - dpad[c0 + D_PAD:c0 + D_PAD + CONV_CHUNK, :]
        od_ref[c0:c0 + CONV_CHUNK, :] = _bf(_dot(_bf(diff), bdd_ref[...]) * dsc_ref[...])


def _mix_cd(h1, w_cd, dw, dwb, lng, lnb, bdd, dsc, pmask, l, row0, rows, seq):
    b0 = row0 // seq
    pcnt = _pool_counts(seq)
    return pl.pallas_call(
        functools.partial(_mix_cd_kernel, seq=seq),
        grid=(rows // seq,),
        in_specs=[pl.BlockSpec((seq, D_MODEL), lambda b: (b + b0, 0))] + [
            _layer(a, l) for a in (w_cd, dw, dwb, lng, lnb, bdd, dsc)] + [_whole(pmask), _whole(pcnt)],
        out_specs=[pl.BlockSpec((seq, C_CH), lambda b: (b, 0)),
                   pl.BlockSpec((seq, D_CH), lambda b: (b, 0))],
        out_shape=[jax.ShapeDtypeStruct((rows, C_CH), BF16),
                   jax.ShapeDtypeStruct((rows, D_CH), BF16)],
        scratch_shapes=[pltpu.VMEM((seq + 2 * C_PAD, C_CH), F32),
                        pltpu.VMEM((seq + 2 * D_PAD, D_CH), F32)],
        compiler_params=_cparams(("parallel",)),
        name="mix_cd_%d" % seq,
    )(h1, w_cd, dw, dwb, lng, lnb, bdd, dsc, pmask, pcnt)


def _merge_kernel(*refs, moe, n_ctx_tiles):
    (x_ref, h_ref, oac_ref, oal_ref, obc_ref, obl_ref, occ_ref, ocl_ref, odc_ref, odl_ref,
     mod_ref, n2g_ref, wg_ref, wa_ref, wb_ref, wc_ref, wd_ref, wo_ref) = refs[:18]
    if moe:
        rhi_ref, rlo_ref, x1_ref, h2t_ref, ridx_ref, rw_ref, sel_ref = refs[18:]
    else:
        x1_ref, h2_ref = refs[18:]
    is_ctx = pl.program_id(0) < n_ctx_tiles
    h = h_ref[...]
    acc = None
    for i, (c_ref, l_ref, w_ref) in enumerate(((oac_ref, oal_ref, wa_ref), (obc_ref, obl_ref, wb_ref),
                                               (occ_ref, ocl_ref, wc_ref), (odc_ref, odl_ref, wd_ref))):
        gate = jax.nn.sigmoid(_dot(h, wg_ref[:, i * D_MODEL:(i + 1) * D_MODEL]))
        o = jnp.where(is_ctx, c_ref[...], l_ref[...])
        t = gate * _dot(o, w_ref[...])
        acc = t if acc is None else acc + t
    y = _dot(_bf(acc), wo_ref[...])
    x1 = x_ref[...] + mod_ref[:, 2 * D_MODEL:3 * D_MODEL] * y
    x1_ref[...] = x1
    h2 = (_rms(x1, n2g_ref[...]) * (1.0 + mod_ref[:, 4 * D_MODEL:5 * D_MODEL])
          + mod_ref[:, 3 * D_MODEL:4 * D_MODEL])
    if not moe:
        h2_ref[...] = _bf(h2)
        return
    tm = h2.shape[0]
    for j in range(SUB):
        h2t_ref[pl.ds(j, tm, stride=SUB), :] = h2[:, LANES * j:LANES * (j + 1)]
    hi = _bf(h2)
    lo = _bf(h2 - hi.astype(F32))
    logits = _dot(hi, rhi_ref[...]) + _dot(lo, rhi_ref[...]) + _dot(hi, rlo_ref[...])
    lane = lax.broadcasted_iota(I32, logits.shape, 1)
    lanef = lane.astype(F32)
    neg = jnp.float32(-jnp.inf)
    lg = jnp.where(lane < N_EXP, logits, neg)
    m0 = jnp.max(lg, axis=-1, keepdims=True)
    i0 = jnp.min(jnp.where(lg == m0, lanef, float(N_EXP_PAD)), axis=-1, keepdims=True)
    sel0 = lanef == i0
    lg1 = jnp.where(sel0, neg, lg)
    m1 = jnp.max(lg1, axis=-1, keepdims=True)
    i1 = jnp.min(jnp.where(lg1 == m1, lanef, float(N_EXP_PAD)), axis=-1, keepdims=True)
    sel1 = lanef == i1
    e = jnp.exp(m1 - m0)
    w0 = 1.0 / (1.0 + e)
    w1 = e / (1.0 + e)
    ridx_ref[...] = jnp.where(lane == 0, i0, jnp.where(lane == 1, i1, 0.0)).astype(I32)
    rw_ref[...] = jnp.where(lane == 0, w0, jnp.where(lane == 1, w1, 0.0))
    sel_ref[...] = jnp.where(sel0 | sel1, 1.0, 0.0).astype(BF16)


def _merge(x, h1, branches, mod4, n2g, wg, wa, wb, wc, wd, wo, router, l, j, t_ctx, dec_seq):
    T = x.shape[0]
    moe = router is not None
    nc = t_ctx // TM
    row = functools.partial(_mod_row, tm=TM, t_ctx=t_ctx, dec_seq=dec_seq)
    rowspec = lambda w: pl.BlockSpec((TM, w), lambda i: (i, 0))
    ctxspec = lambda w: pl.BlockSpec((TM, w), lambda i: (jnp.minimum(i, nc - 1), 0))
    latspec = lambda w: pl.BlockSpec((TM, w), lambda i: (jnp.maximum(i - nc, 0), 0))
    in_specs = [rowspec(D_MODEL), rowspec(D_MODEL)]
    args = [x, h1]
    for oc, ol in branches:
        in_specs += [ctxspec(oc.shape[1]), latspec(ol.shape[1])]
        args += [oc, ol]
    in_specs += [pl.BlockSpec((None, None, 1, 6 * D_MODEL), lambda i: (l, row(i), 0, 0))]
    in_specs += [_layer(a, l) for a in (n2g, wg, wa, wb, wc, wd, wo)]
    args += [mod4, n2g, wg, wa, wb, wc, wd, wo]
    out_shape = [jax.ShapeDtypeStruct((T, D_MODEL), F32)]
    out_specs = [rowspec(D_MODEL)]
    if moe:
        in_specs += [_layer(router[0], j), _layer(router[1], j)]
        args += list(router)
        out_shape += [jax.ShapeDtypeStruct((T * SUB, LANES), F32),
                      jax.ShapeDtypeStruct((T, N_EXP_PAD), I32),
                      jax.ShapeDtypeStruct((T, N_EXP_PAD), F32),
                      jax.ShapeDtypeStruct((T, N_EXP_PAD), BF16)]
        out_specs += [pl.BlockSpec((TM * SUB, LANES), lambda i: (i, 0)),
                      rowspec(N_EXP_PAD), rowspec(N_EXP_PAD), rowspec(N_EXP_PAD)]
    else:
        out_shape += [jax.ShapeDtypeStruct((T, D_MODEL), BF16)]
        out_specs += [rowspec(D_MODEL)]
    return pl.pallas_call(
        functools.partial(_merge_kernel, moe=moe, n_ctx_tiles=nc),
        grid=(T // TM,),
        in_specs=in_specs, out_specs=out_specs, out_shape=out_shape,
        compiler_params=_cparams(("parallel",)),
        name="merge_moe" if moe else "merge",
    )(*args)


def _swiglu_step(x, w1_ref, w3_ref, w2_ref, acc_ref, f):
    a = _dot(x, _bf(w1_ref[...]))
    b = _dot(x, _bf(w3_ref[...]))
    t = _dot(_bf(a * jax.nn.sigmoid(a) * b), _bf(w2_ref[...]))

    @pl.when(f == 0)
    def _():
        acc_ref[...] = t

    @pl.when(f > 0)
    def _():
        acc_ref[...] += t


def _pre_norm1(x, mod_ref, g_ref):
    return _bf(_rms(x, g_ref[...]) * (1.0 + mod_ref[:, D_MODEL:2 * D_MODEL]) + mod_ref[:, 0:D_MODEL])


def _ffn_dense_kernel(x_ref, w1_ref, w3_ref, w2_ref, x1_ref, mod_ref, modn_ref, gn_ref,
                      o_ref, hn_ref, acc_ref):
    f = pl.program_id(1)
    _swiglu_step(x_ref[...], w1_ref, w3_ref, w2_ref, acc_ref, f)

    @pl.when(f == pl.num_programs(1) - 1)
    def _():
        x2 = x1_ref[...] + mod_ref[:, 5 * D_MODEL:6 * D_MODEL] * acc_ref[...]
        o_ref[...] = x2
        hn_ref[...] = _pre_norm1(x2, modn_ref, gn_ref)


def _ffn_dense(h2, w1, w3, w2, x1, mod4, n1g, l, j, t_ctx, dec_seq):
    T = h2.shape[0]
    nf = w1.shape[2] // TF_DENSE
    row = functools.partial(_mod_row, tm=TG, t_ctx=t_ctx, dec_seq=dec_seq)
    rows = pl.BlockSpec((TG, D_MODEL), lambda i, f: (i, 0))
    return pl.pallas_call(
        _ffn_dense_kernel,
        grid=(T // TG, nf),
        in_specs=[rows,
                  pl.BlockSpec((None, D_MODEL, TF_DENSE), lambda i, f: (j, 0, f)),
                  pl.BlockSpec((None, D_MODEL, TF_DENSE), lambda i, f: (j, 0, f)),
                  pl.BlockSpec((None, TF_DENSE, D_MODEL), lambda i, f: (j, f, 0)),
                  rows,
                  pl.BlockSpec((None, None, 1, 6 * D_MODEL), lambda i, f: (l, row(i), 0, 0)),
                  pl.BlockSpec((None, None, 1, 6 * D_MODEL), lambda i, f: (l + 1, row(i), 0, 0)),
                  _layer(n1g, l + 1)],
        out_specs=[rows, rows],
        out_shape=[jax.ShapeDtypeStruct((T, D_MODEL), F32), jax.ShapeDtypeStruct((T, D_MODEL), BF16)],
        scratch_shapes=[pltpu.VMEM((TG, D_MODEL), F32)],
        compiler_params=_cparams(("parallel", "arbitrary")),
        name="ffn_dense",
    )(h2, w1, w3, w2, x1, mod4, mod4, n1g)


def _ffn_experts_kernel(te_ref, na_ref, x_ref, w1_ref, w3_ref, w2_ref, o_ref, xb_ref, acc_ref):
    i = pl.program_id(0)
    f = pl.program_id(1)
    active = i < na_ref[0]

    @pl.when(jnp.logical_and(active, f == 0))
    def _():
        for j in range(SUB):
            xb_ref[:, LANES * j:LANES * (j + 1)] = _bf(x_ref[pl.ds(j, TG, stride=SUB), :])

    @pl.when(active)
    def _():
        _swiglu_step(xb_ref[...], w1_ref, w3_ref, w2_ref, acc_ref, f)

    @pl.when(jnp.logical_and(active, f == pl.num_programs(1) - 1))
    def _():
        for j in range(SUB):
            o_ref[pl.ds(j, TG, stride=SUB), :] = acc_ref[:, LANES * j:LANES * (j + 1)]

    @pl.when(jnp.logical_and(jnp.logical_not(active), f == 0))
    def _():
        o_ref[...] = jnp.zeros(o_ref.shape, F32)


def _ffn_experts(xs, w1, w3, w2, tile_e, n_active, j):
    rows = xs.shape[0] // SUB
    nf = w1.shape[3] // TF_MOE

    def fidx(i, f, na):
        return jnp.where(i < na[0], f, nf - 1)

    def xidx(i, na):
        return jnp.minimum(i, na[0] - 1)

    return pl.pallas_call(
        _ffn_experts_kernel,
        grid_spec=pltpu.PrefetchScalarGridSpec(
            num_scalar_prefetch=2, grid=(rows // TG, nf),
            in_specs=[pl.BlockSpec((TG * SUB, LANES), lambda i, f, te, na: (xidx(i, na), 0)),
                      pl.BlockSpec((None, None, D_MODEL, TF_MOE),
                                   lambda i, f, te, na: (j, te[i], 0, fidx(i, f, na))),
                      pl.BlockSpec((None, None, D_MODEL, TF_MOE),
                                   lambda i, f, te, na: (j, te[i], 0, fidx(i, f, na))),
                      pl.BlockSpec((None, None, TF_MOE, D_MODEL),
                                   lambda i, f, te, na: (j, te[i], fidx(i, f, na), 0))],
            out_specs=pl.BlockSpec((TG * SUB, LANES), lambda i, f, te, na: (i, 0)),
            scratch_shapes=[pltpu.VMEM((TG, D_MODEL), BF16), pltpu.VMEM((TG, D_MODEL), F32)]),
        out_shape=jax.ShapeDtypeStruct((rows * SUB, LANES), F32),
        compiler_params=_cparams(("parallel", "arbitrary")),
        name="ffn_experts",
    )(tile_e, n_active, xs, w1, w3, w2)


def _rank_kernel(sel_ref, tri_ref, rank_ref, cnt_ref, carry):
    i = pl.program_id(0)

    @pl.when(i == 0)
    def _():
        carry[...] = jnp.zeros(carry.shape, F32)

    s = sel_ref[...]
    rank_ref[...] = _dot(tri_ref[...], s) + carry[...]
    carry[...] += jnp.sum(s.astype(F32), axis=0, keepdims=True)
    cnt_ref[...] = jnp.broadcast_to(carry[...], cnt_ref.shape)


def _rank(sel, tri):
    T = sel.shape[0]
    return pl.pallas_call(
        _rank_kernel,
        grid=(T // TR,),
        in_specs=[pl.BlockSpec((TR, N_EXP_PAD), lambda i: (i, 0)),
                  pl.BlockSpec((TR, TR), lambda i: (0, 0))],
        out_specs=[pl.BlockSpec((TR, N_EXP_PAD), lambda i: (i, 0)),
                   pl.BlockSpec((8, N_EXP_PAD), lambda i: (0, 0))],
        out_shape=[jax.ShapeDtypeStruct((T, N_EXP_PAD), F32),
                   jax.ShapeDtypeStruct((8, N_EXP_PAD), F32)],
        scratch_shapes=[pltpu.VMEM((1, N_EXP_PAD), F32)],
        compiler_params=_cparams(("arbitrary",)),
        name="route_rank",
    )(sel, tri)


TILE_ROWS = TR * SUB


def _token_tile(ref, tok):
    return ref.at[pl.ds(pl.multiple_of(tok * SUB, SUB), SUB)]


def _dispatch_kernel(pos0_ref, pos1_ref, ends_ref, padded_ref, h_ref, xs_hbm, stage, zbuf, sem, zsem):
    i = pl.program_id(0)

    @pl.when(i == 0)
    def _():
        zbuf[...] = jnp.zeros(zbuf.shape, F32)

        def fill_tile(first_slot):
            start = pl.multiple_of(first_slot * SUB, TG * SUB)
            fill = pltpu.make_async_copy(zbuf, xs_hbm.at[pl.ds(start, TG * SUB)], zsem.at[0])
            fill.start()
            fill.wait()

        n_slots = xs_hbm.shape[0] // SUB
        for e in range(N_EXP):
            @pl.when(padded_ref[e] > 0)
            def _():
                fill_tile(ends_ref[e] - TG)

            @pl.when(ends_ref[N_EXP - 1] + e * TG < n_slots)
            def _():
                fill_tile(ends_ref[N_EXP - 1] + e * TG)

    slot = i % 2
    base = pl.multiple_of(slot * TILE_ROWS, TILE_ROWS)
    stage[pl.ds(base, TILE_ROWS), :] = h_ref[...]

    def issue(r, c):
        t = i * TR + r
        src = _token_tile(stage, slot * TR + r)
        pltpu.make_async_copy(src, _token_tile(xs_hbm, pos0_ref[t]), sem.at[slot]).start()
        pltpu.make_async_copy(src, _token_tile(xs_hbm, pos1_ref[t]), sem.at[slot]).start()
        return c
    lax.fori_loop(0, TR, issue, 0, unroll=8)

    def wait_slot(s):
        b = pl.multiple_of(s * TILE_ROWS, TILE_ROWS)
        for _ in range(2):
            pltpu.make_async_copy(stage.at[pl.ds(b, TILE_ROWS)], xs_hbm.at[pl.ds(0, TILE_ROWS)],
                                  sem.at[s]).wait()

    @pl.when(i > 0)
    def _():
        wait_slot(1 - slot)

    @pl.when(i == pl.num_programs(0) - 1)
    def _():
        wait_slot(slot)


def _dispatch(pos0, pos1, ends, padded, h2t, n_rows):
    T = h2t.shape[0] // SUB
    assert n_rows - 2 * T <= N_EXP * TG
    return pl.pallas_call(
        _dispatch_kernel,
        grid_spec=pltpu.PrefetchScalarGridSpec(
            num_scalar_prefetch=4, grid=(T // TR,),
            in_specs=[pl.BlockSpec((TILE_ROWS, LANES), lambda i, *_: (i, 0))],
            out_specs=pl.BlockSpec(memory_space=pl.ANY),
            scratch_shapes=[pltpu.VMEM((2 * TILE_ROWS, LANES), F32),
                            pltpu.VMEM((TG * SUB, LANES), F32),
                            pltpu.SemaphoreType.DMA((2,)),
                            pltpu.SemaphoreType.DMA((1,))]),
        out_shape=jax.ShapeDtypeStruct((n_rows * SUB, LANES), F32),
        compiler_params=_cparams(("arbitrary",)),
        name="moe_dispatch",
    )(pos0, pos1, ends, padded, h2t)


def _combine_kernel(pos0_ref, pos1_ref, y_hbm, x1_ref, rw_ref, mod_ref, *rest, emit_next):
    if emit_next:
        modn_ref, gn_ref, o_ref, hn_ref, buf0, buf1, sem0, sem1 = rest
    else:
        o_ref, buf0, buf1, sem0, sem1 = rest
    i = pl.program_id(0)
    nt = pl.num_programs(0)

    def issue(tile, slot):
        def body(r, c):
            t = tile * TR + r
            pltpu.make_async_copy(_token_tile(y_hbm, pos0_ref[t]), _token_tile(buf0, slot * TR + r),
                                  sem0.at[slot]).start()
            pltpu.make_async_copy(_token_tile(y_hbm, pos1_ref[t]), _token_tile(buf1, slot * TR + r),
                                  sem1.at[slot]).start()
            return c
        lax.fori_loop(0, TR, body, 0, unroll=8)

    @pl.when(i == 0)
    def _():
        issue(0, 0)

    @pl.when(i + 1 < nt)
    def _():
        issue(i + 1, (i + 1) % 2)

    slot = i % 2
    base = pl.multiple_of(slot * TILE_ROWS, TILE_ROWS)
    for buf, sem in ((buf0, sem0), (buf1, sem1)):
        pltpu.make_async_copy(y_hbm.at[pl.ds(0, TILE_ROWS)], buf.at[pl.ds(base, TILE_ROWS)],
                              sem.at[slot]).wait()
    rw = rw_ref[...]
    w0 = rw[:, 0:1]
    w1 = rw[:, 1:2]
    for j in range(SUB):
        sl = slice(LANES * j, LANES * (j + 1))
        f = (w0 * buf0[pl.ds(base + j, TR, stride=SUB), :]
             + w1 * buf1[pl.ds(base + j, TR, stride=SUB), :])
        o_ref[:, sl] = x1_ref[:, sl] + mod_ref[:, 5 * D_MODEL + LANES * j:5 * D_MODEL + LANES * (j + 1)] * f
    if emit_next:
        hn_ref[...] = _pre_norm1(o_ref[...], modn_ref, gn_ref)


def _combine(pos0, pos1, ys, x1, rw, mod4, n1g, l, t_ctx, dec_seq, emit_next):
    T = x1.shape[0]
    row = functools.partial(_mod_row, tm=TR, t_ctx=t_ctx, dec_seq=dec_seq)
    rows = pl.BlockSpec((TR, D_MODEL), lambda i, p0, p1: (i, 0))
    in_specs = [pl.BlockSpec(memory_space=pl.ANY), rows,
                pl.BlockSpec((TR, N_EXP_PAD), lambda i, p0, p1: (i, 0)),
                pl.BlockSpec((None, None, 1, 6 * D_MODEL), lambda i, p0, p1: (l, row(i), 0, 0))]
    args = [pos0, pos1, ys, x1, rw, mod4]
    out_specs = [rows]
    out_shape = [jax.ShapeDtypeStruct((T, D_MODEL), F32)]
    if emit_next:
        in_specs += [pl.BlockSpec((None, None, 1, 6 * D_MODEL), lambda i, p0, p1: (l + 1, row(i), 0, 0)),
                     _layer(n1g, l + 1)]
        args += [mod4, n1g]
        out_specs += [rows]
        out_shape += [jax.ShapeDtypeStruct((T, D_MODEL), BF16)]
    return pl.pallas_call(
        functools.partial(_combine_kernel, emit_next=emit_next),
        grid_spec=pltpu.PrefetchScalarGridSpec(
            num_scalar_prefetch=2, grid=(T // TR,),
            in_specs=in_specs, out_specs=out_specs,
            scratch_shapes=[pltpu.VMEM((2 * TILE_ROWS, LANES), F32), pltpu.VMEM((2 * TILE_ROWS, LANES), F32),
                            pltpu.SemaphoreType.DMA((2,)), pltpu.SemaphoreType.DMA((2,))]),
        out_shape=out_shape,
        compiler_params=_cparams(("arbitrary",)),
        name="moe_combine",
    )(*args)


def _block_ones(n, seg):
    idx = np.arange(n) // seg
    return jnp.asarray((idx[:, None] == idx[None, :]).astype(np.float32), dtype=BF16)


def _rope_tables(seq, head_w, rot_dim, n_heads):
    rows = seq // GRID_W
    nf = rot_dim // 4
    freqs = ROPE_THETA ** (-np.arange(nf, dtype=np.float64) / nf)
    row = np.repeat(np.arange(rows, dtype=np.float64), GRID_W)
    col = np.tile(np.arange(GRID_W, dtype=np.float64), rows)
    ang = np.concatenate([row[:, None] * freqs, col[:, None] * freqs], axis=-1)
    ang = np.repeat(ang, 2, axis=-1)
    even = (np.arange(rot_dim) % 2 == 0)[None, :]
    c = np.ones((seq, head_w))
    se = np.zeros((seq, head_w))
    so = np.zeros((seq, head_w))
    c[:, :rot_dim] = np.cos(ang)
    se[:, :rot_dim] = np.where(even, -np.sin(ang), 0.0)
    so[:, :rot_dim] = np.where(even, 0.0, np.sin(ang))
    return tuple(jnp.asarray(np.tile(t, (1, n_heads)), dtype=F32) for t in (c, se, so))


def _pool_mask():
    m = np.zeros((2 * D_PAD, D_CH), np.float32)
    for gi, w in enumerate(POOL):
        left = w // 2
        right = w - 1 - left
        for off in range(-left, right + 1):
            m[off + D_PAD, gi * D_G:(gi + 1) * D_G] = 1.0
    return jnp.asarray(m)


def _pool_counts(seq):
    t = np.arange(seq)
    m = np.zeros((seq, D_CH), np.float32)
    for gi, w in enumerate(POOL):
        left = w // 2
        right = w - 1 - left
        m[:, gi * D_G:(gi + 1) * D_G] = (np.minimum(t + right + 1, seq) - np.maximum(t - left, 0))[:, None]
    return jnp.asarray(m)


def _b_heads(nope, rope):
    ref = nope if nope is not None else rope
    lead = ref.shape[:-1]
    z = lambda w: jnp.zeros(lead + (w,), ref.dtype)
    parts = [rope if rope is not None else z(B_ROPE), nope if nope is not None else z(B_NOPE),
             z(B_HEAD_PAD - B_QK)]
    out = jnp.concatenate(parts, axis=-1)
    return out.reshape(lead[:-1] + (lead[-1] * B_HEAD_PAD,))


def kernel(x_prompt, x_sample, cache_a_k, cache_a_v, cache_b_ckv, cache_b_kpe, c, c_ctx, ada_w, ada_b, norm1_g, norm2_g, w_in, a_q_norm, a_k_norm, a_lambda, a_sub_norm, a_w_o, b_q_lora_norm, b_kv_lora_norm, b_w_uq, b_w_ukv, b_q_norm, b_k_norm, b_w_o, c_dw, c_dw_b, c_ln_g, c_ln_b, c_w_o, d_w_group, d_scale, d_w_o, w_out, ffn_w1, ffn_w3, ffn_w2, moe_router, moe_w1, moe_w3, moe_w2):
    nb, seq, _ = x_prompt.shape
    db, dseq, _ = x_sample.shape
    L = w_in.shape[0]
    past = cache_a_k.shape[2]
    t_ctx = nb * seq
    t_lat = db * dseq
    T = t_ctx + t_lat
    na = 2 * A_HEADS * A_QK
    assert t_ctx % TG == 0 and dseq % TG == 0 and seq % TR == 0 and db + 1 <= 16

    x = jnp.concatenate([x_prompt.reshape(t_ctx, D_MODEL), x_sample.reshape(t_lat, D_MODEL)], axis=0)
    cv = jnp.concatenate([c_ctx[None, :], c, jnp.zeros((15 - db, D_MODEL), F32)], axis=0)
    mod4 = _ada_all(cv, ada_w, ada_b).reshape(L, 16, 1, 6 * D_MODEL)

    bd_a = _block_ones(na, A_QK)
    bd_b = _block_ones(B_HEADS * B_HEAD_PAD, B_HEAD_PAD)
    rope_a = _rope_tables(dseq, A_QK, A_QK, 2 * A_HEADS)
    rope_b = _rope_tables(dseq, B_HEAD_PAD, B_ROPE, B_HEADS)
    pmask = _pool_mask()
    tri = jnp.asarray(np.tril(np.ones((TR, TR), np.float32), -1), dtype=BF16)

    row1 = lambda a: a.reshape(L, 1, -1)
    w_a, w_b, w_cd, w_g = _repack(w_in)
    gq_a = row1(jnp.tile(a_q_norm, (1, 2 * A_HEADS)))
    gk_a = row1(jnp.tile(a_k_norm, (1, 2 * A_HEADS)))
    uq = b_w_uq.reshape(L, B_QL, B_HEADS, B_QK)
    wuq = _b_heads(uq[..., :B_NOPE], uq[..., B_NOPE:]).astype(BF16)
    ukv = b_w_ukv.reshape(L, B_KVL, B_HEADS, B_NOPE + B_V)
    wuk = _b_heads(ukv[..., :B_NOPE], None).astype(BF16)
    wuv = ukv[..., B_NOPE:].reshape(L, B_KVL, B_HEADS * B_V).astype(BF16)
    gq_b = row1(jnp.tile(_b_heads(b_q_norm[:, None, :B_NOPE], b_q_norm[:, None, B_NOPE:]), (1, B_HEADS)))
    gk_b = row1(jnp.tile(_b_heads(b_k_norm[:, None, :B_NOPE], b_k_norm[:, None, B_NOPE:]), (1, B_HEADS)))
    bdd = jnp.zeros((L, D_CH, D_CH), F32)
    for gi in range(len(POOL)):
        bdd = bdd.at[:, gi * D_G:(gi + 1) * D_G, gi * D_G:(gi + 1) * D_G].set(d_w_group[:, gi])
    bdd = bdd.astype(BF16)
    a_wo, b_wo, c_wo, d_wo, wo = (w.astype(BF16) for w in (a_w_o, b_w_o, c_w_o, d_w_o, w_out))
    ffn1, ffn3, ffn2 = (w.astype(BF16) for w in (ffn_w1, ffn_w3, ffn_w2))
    r_pad = jnp.pad(moe_router, ((0, 0), (0, 0), (0, N_EXP_PAD - N_EXP)))
    r_hi = r_pad.astype(BF16)
    r_lo = (r_pad - r_hi.astype(F32)).astype(BF16)
    n1g, n2g = row1(norm1_g), row1(norm2_g)
    g_sub = row1(a_sub_norm)
    gql, gkvl = row1(b_q_lora_norm), row1(b_kv_lora_norm)
    dwb, lng, lnb, dsc = row1(c_dw_b), row1(c_ln_g), row1(c_ln_b), row1(d_scale)

    ck_a = cache_a_k.reshape(db, L, past, na).astype(BF16)
    cv_a = cache_a_v.reshape(db, L, past, A_HEADS * A_V).astype(BF16)
    kpe_pad = jnp.pad(cache_b_kpe, ((0, 0), (0, 0), (0, 0), (0, LANES - B_ROPE)))
    ck_b, cv_b = _cache_b(cache_b_ckv, kpe_pad, wuk, wuv, gk_b, bd_b)

    n_rows = 2 * T + N_EXP * TG
    nt_g = n_rows // TG

    new_ak = jnp.zeros((nb, L, seq, na), F32)
    new_av = jnp.zeros((nb, L, seq, A_HEADS * A_V), F32)
    new_ckv = jnp.zeros((nb, L, seq, B_KVL), F32)
    new_kpe = jnp.zeros((nb, L, seq, B_ROPE), F32)
    h1 = _prep(x, mod4, n1g, 0, t_ctx, dseq)
    for l in range(L):
        last = l == L - 1

        q1c, q2c, kc, vc, new_ak, new_av = _proj_a(h1, w_a, gq_a, gk_a, bd_a, None, l, 0, t_ctx, False,
                                                   cache_out=(new_ak, new_av))
        q1l, q2l, kl, vl = _proj_a(h1, w_a, gq_a, gk_a, bd_a, rope_a, l, t_ctx, t_lat, True)
        oa = (_attn_a(q1c, q2c, kc, vc, None, None, a_lambda, g_sub, l, seq, False),
              _attn_a(q1l, q2l, kl, vl, ck_a, cv_a, a_lambda, g_sub, l, dseq, True))

        bargs = (w_b, gql, gkvl, wuq, wuk, wuv, gq_b, gk_b, bd_b)
        qc, kc, vc, new_ckv, new_kpe = _proj_b(h1, *bargs, None, l, 0, t_ctx, False,
                                               cache_out=(new_ckv, new_kpe))
        ql, kl, vl = _proj_b(h1, *bargs, rope_b, l, t_ctx, t_lat, True)
        ob = (_attn_b(qc, kc, vc, None, None, l, seq, False),
              _attn_b(ql, kl, vl, ck_b, cv_b, l, dseq, True))

        cdargs = (w_cd, c_dw, dwb, lng, lnb, bdd, dsc, pmask)
        occ, odc = _mix_cd(h1, *cdargs, l, 0, t_ctx, seq)
        ocl, odl = _mix_cd(h1, *cdargs, l, t_ctx, t_lat, dseq)

        j = l // 2
        moe = l % 2 == 1
        router = (r_hi, r_lo) if moe else None
        outs = _merge(x, h1, (oa, ob, (occ, ocl), (odc, odl)), mod4, n2g, w_g, a_wo, b_wo,
                      c_wo, d_wo, wo, router, l, j, t_ctx, dseq)
        if not moe:
            x1, h2 = outs
            assert not last, "a dense layer is always followed by another layer"
            x, h1 = _ffn_dense(h2, ffn1, ffn3, ffn2, x1, mod4, n1g, l, j, t_ctx, dseq)
        else:
            x1, h2t, ridx, rw, sel = outs
            rank, cnt = _rank(sel, tri)
            counts = cnt[0, :N_EXP].astype(I32)
            padded = ((counts + TG - 1) // TG) * TG
            ends = jnp.cumsum(padded)
            offs = ends - padded
            pos_all = offs[None, :] + rank[:, :N_EXP].astype(I32)
            pos0 = jnp.take_along_axis(pos_all, ridx[:, 0:1], axis=1)[:, 0]
            pos1 = jnp.take_along_axis(pos_all, ridx[:, 1:2], axis=1)[:, 0]
            tile_start = jnp.arange(nt_g, dtype=I32) * TG
            tile_e = jnp.minimum(jnp.sum(tile_start[:, None] >= ends[None, :], axis=1), N_EXP - 1).astype(I32)
            n_active = (ends[-1] // TG).astype(I32).reshape(1)
            last_e = tile_e[jnp.maximum(n_active[0] - 1, 0)]
            tile_e = jnp.where(tile_start < ends[-1], tile_e, last_e)
            xs = _dispatch(pos0, pos1, ends.astype(I32), padded, h2t, n_rows)
            ys = _ffn_experts(xs, moe_w1, moe_w3, moe_w2, tile_e, n_active, j)
            outs = _combine(pos0, pos1, ys, x1, rw, mod4, n1g, l, t_ctx, dseq, not last)
            x, h1 = (outs[0], None) if last else outs

    y_prompt = x[:t_ctx].reshape(nb, seq, D_MODEL)
    y_sample = x[t_ctx:].reshape(db, dseq, D_MODEL)
    new_a_k = new_ak.reshape(nb, L, seq, 2 * A_HEADS, A_QK)
    new_a_v = new_av.reshape(nb, L, seq, A_HEADS, A_V)
    return (y_prompt, y_sample, new_a_k, new_a_v, new_ckv, new_kpe)
```

```python
import functools
import math

import numpy as np
import jax
import jax.numpy as jnp
from jax import lax
from jax.experimental import pallas as pl
from jax.experimental.pallas import tpu as pltpu

F32 = jnp.float32
BF16 = jnp.bfloat16
I32 = jnp.int32

EPS = 1e-6
D_MODEL = 1024
GRID_W = 64
ROPE_THETA = 10000.0
A_HEADS = 4
A_QK = 64
A_V = 128
B_HEADS = 4
B_NOPE = 64
B_ROPE = 32
B_QK = B_NOPE + B_ROPE
B_V = 64
B_QL = 256
B_KVL = 128
B_HEAD_PAD = 128
C_CH = 256
C_W = 31
C_PAD = 16
POOL = (2, 4, 8, 16)
D_G = 64
D_CH = D_G * len(POOL)
D_PAD = 8
N_EXP = 8
N_EXP_PAD = 128
LANES = 128
SUB = 8

O_AQ, O_AK, O_AV, O_BQ, O_BKV, O_KPE, O_C, O_D, O_G, O_END = (
    0, 512, 1024, 1536, 1792, 1920, 1952, 2464, 2720, 6816)
W_B_COLS = (O_KPE - O_BQ) + LANES

TM = 1024
TM_MERGE = 512
TQ = 512
TG = 1024
TR = 256
TF_DENSE = 1408
TF_MOE = 512
VMEM_LIMIT = 56 * 1024 * 1024


def _cparams(sem):
    return pltpu.CompilerParams(dimension_semantics=sem, vmem_limit_bytes=VMEM_LIMIT)


def _dot(a, b):
    return jnp.dot(a, b, preferred_element_type=F32)


def _dot_nt(a, b):
    return lax.dot_general(a, b, (((1,), (1,)), ((), ())), preferred_element_type=F32)


def _bf(x):
    return x.astype(BF16)


def _rms(x, g):
    return x * lax.rsqrt(jnp.mean(x * x, axis=-1, keepdims=True) + EPS) * g


def _seg_sum_sq(x, bd_ref):
    return _dot(_bf(x * x), bd_ref[...])


def _rope(x, c_ref, se_ref, so_ref):
    n = x.shape[-1]
    return (x * c_ref[...] + pltpu.roll(x, n - 1, 1) * se_ref[...]
            + pltpu.roll(x, 1, 1) * so_ref[...])


def _mod_row(i, tm, t_ctx, dec_seq):
    r = i * tm
    return jnp.where(r < t_ctx, 0, 1 + (r - t_ctx) // dec_seq)


def _layer(a, l):
    nd = a.ndim
    return pl.BlockSpec((None,) + a.shape[1:], lambda *_: (l,) + (0,) * (nd - 1),
                        pipeline_mode=pl.Buffered(1))


def _whole(a):
    nd = a.ndim
    return pl.BlockSpec(a.shape, lambda *_: (0,) * nd, pipeline_mode=pl.Buffered(1))


def _ada_kernel(cv_ref, w_ref, b_ref, o_ref):
    cv = cv_ref[...]
    s = cv * jax.nn.sigmoid(cv)
    o_ref[...] = _dot(_bf(s), _bf(w_ref[...])) + b_ref[...]


def _ada_all(cv, ada_w, ada_b):
    L, d, n = ada_w.shape
    tn = 1536
    return pl.pallas_call(
        _ada_kernel,
        grid=(L, n // tn),
        in_specs=[pl.BlockSpec((16, d), lambda l, j: (0, 0)),
                  pl.BlockSpec((None, d, tn), lambda l, j: (l, 0, j)),
                  pl.BlockSpec((None, 1, tn), lambda l, j: (l, 0, j))],
        out_specs=pl.BlockSpec((None, 16, tn), lambda l, j: (l, 0, j)),
        out_shape=jax.ShapeDtypeStruct((L, 16, n), F32),
        compiler_params=_cparams(("parallel", "parallel")),
        name="ada_mod",
    )(cv, ada_w, ada_b.reshape(L, 1, n))


REPACK_ROWS = 1024
REPACK_GROUPS = ((O_AQ, O_BQ - O_AQ, 0, 0), (O_BQ, O_C - O_BQ, 1, 0),
                 (O_C, O_G - O_C, 2, 0), (O_G, O_END - O_G, 3, 0))


def _repack_kernel(w_ref, wa_ref, wb_ref, wcd_ref, wg_ref):
    outs = (wa_ref, wb_ref, wcd_ref, wg_ref)
    r = pl.program_id(1)
    for step in range(pl.cdiv(O_END, REPACK_ROWS)):
        lo, hi = step * REPACK_ROWS, (step + 1) * REPACK_ROWS

        @pl.when(r == step)
        def _():
            for first, rows, dst, dst_first in REPACK_GROUPS:
                a, b = max(first, lo), min(first + rows, hi)
                if a < b:
                    outs[dst][dst_first + a - first:dst_first + b - first, :] = _bf(w_ref[a - lo:b - lo, :])

    @pl.when(r == 0)
    def _():
        wb_ref[O_C - O_BQ:W_B_COLS, :] = jnp.zeros((W_B_COLS - (O_C - O_BQ), D_MODEL), BF16)


def _repack(w_t):
    L, n, d = w_t.shape
    heights = (O_BQ - O_AQ, W_B_COLS, O_G - O_C, O_END - O_G)
    return pl.pallas_call(
        _repack_kernel,
        grid=(L, pl.cdiv(n, REPACK_ROWS)),
        in_specs=[pl.BlockSpec((None, REPACK_ROWS, d), lambda l, r: (l, r, 0))],
        out_specs=[pl.BlockSpec((None, h, d), lambda l, r: (l, 0, 0)) for h in heights],
        out_shape=[jax.ShapeDtypeStruct((L, h, d), BF16) for h in heights],
        compiler_params=_cparams(("parallel", "arbitrary")),
        name="repack_w_in",
    )(w_t)


def _prep_kernel(x_ref, mod_ref, g_ref, h_ref):
    h_ref[...] = _pre_norm1(x_ref[...], mod_ref, g_ref)


def _prep(x, mod4, g, l, t_ctx, dec_seq):
    T = x.shape[0]
    row = functools.partial(_mod_row, tm=TM, t_ctx=t_ctx, dec_seq=dec_seq)
    return pl.pallas_call(
        _prep_kernel,
        grid=(T // TM,),
        in_specs=[pl.BlockSpec((TM, D_MODEL), lambda i: (i, 0)),
                  pl.BlockSpec((None, None, 1, 6 * D_MODEL), lambda i: (l, row(i), 0, 0)),
                  _layer(g, l)],
        out_specs=pl.BlockSpec((TM, D_MODEL), lambda i: (i, 0)),
        out_shape=jax.ShapeDtypeStruct((T, D_MODEL), BF16),
        compiler_params=_cparams(("parallel",)),
        name="prep",
    )(x, mod4, g)


def _proj_a_kernel(*refs, latent):
    if latent:
        (h_ref, w_ref, gq_ref, gk_ref, bd_ref, c_ref, se_ref, so_ref,
         q1_ref, q2_ref, k_ref, v_ref) = refs
    else:
        (h_ref, w_ref, gq_ref, gk_ref, bd_ref, _, _,
         q1_ref, q2_ref, k_ref, v_ref, nk_ref, nv_ref) = refs
    p = _dot_nt(h_ref[...], w_ref[...])
    n = 2 * A_HEADS * A_QK
    q = p[:, 0:n]
    k = p[:, n:2 * n]
    v = p[:, 2 * n:3 * n]
    q = q * lax.rsqrt(_seg_sum_sq(q, bd_ref) * (1.0 / A_QK) + EPS) * gq_ref[...]
    k = k * lax.rsqrt(_seg_sum_sq(k, bd_ref) * (1.0 / A_QK) + EPS) * gk_ref[...]
    if latent:
        q = _rope(q, c_ref, se_ref, so_ref)
        k = _rope(k, c_ref, se_ref, so_ref)
    else:
        nk_ref[...] = k.reshape(nk_ref.shape)
        nv_ref[...] = v.reshape(nv_ref.shape)
    q = q * (A_QK ** -0.5 * LOG2E)
    lane = lax.broadcasted_iota(I32, q.shape, 1)
    first = (lane % (2 * A_QK)) < A_QK
    q1_ref[...] = _bf(jnp.where(first, q, 0.0))
    q2_ref[...] = _bf(jnp.where(first, 0.0, q))
    k_ref[...] = _bf(k)
    v_ref[...] = _bf(v)


def _cache_out_spec(buf, l):
    nb, _, seq, w = buf.shape
    return pl.BlockSpec((TM // seq, None, seq, w), lambda i: (i, l, 0, 0))


def _proj_a(h1, w_a, gq, gk, bd, rope_tabs, l, row0, rows, latent, cache_out=None):
    n = 2 * A_HEADS * A_QK
    b0 = row0 // TM
    tab_blocks = rope_tabs[0].shape[0] // TM if latent else 1
    in_specs = [pl.BlockSpec((TM, D_MODEL), lambda i: (i + b0, 0)),
                _layer(w_a, l), _layer(gq, l), _layer(gk, l), _whole(bd)]
    args = [h1, w_a, gq, gk, bd]
    out_spec = pl.BlockSpec((TM, n), lambda i: (i, 0))
    out_shape = [jax.ShapeDtypeStruct((rows, n), BF16)] * 4
    out_specs = [out_spec] * 4
    aliases = {}
    if latent:
        in_specs += [pl.BlockSpec((TM, n), lambda i: (i % tab_blocks, 0))] * 3
        args += list(rope_tabs)
    else:
        for buf in cache_out:
            aliases[len(args)] = len(out_shape)
            in_specs.append(pl.BlockSpec(memory_space=pl.ANY))
            args.append(buf)
            out_shape.append(jax.ShapeDtypeStruct(buf.shape, buf.dtype))
            out_specs.append(_cache_out_spec(buf, l))
    return pl.pallas_call(
        functools.partial(_proj_a_kernel, latent=latent),
        grid=(rows // TM,),
        in_specs=in_specs, out_specs=out_specs, out_shape=out_shape,
        input_output_aliases=aliases,
        compiler_params=_cparams(("parallel",)),
        name="proj_a_lat" if latent else "proj_a_ctx",
    )(*args)


LOG2E = math.log2(math.e)


def _attend(qs, kss, vs):
    es, rs = [], []
    for q, ks in zip(qs, kss):
        scores = [_dot_nt(q, k) for k in ks]
        m = functools.reduce(jnp.maximum, [jnp.max(s, axis=-1, keepdims=True) for s in scores])
        e = [jnp.exp2(s - m) for s in scores]
        l = functools.reduce(lambda a, b: a + b, [jnp.sum(x, axis=-1, keepdims=True) for x in e])
        es.append([_bf(x) for x in e])
        rs.append(1.0 / l)
    o = None
    for p, v in enumerate(vs):
        stacked = es[0][p] if len(qs) == 1 else jnp.concatenate([e[p] for e in es], axis=0)
        t = _dot(stacked, v)
        o = t if o is None else o + t
    outs, r0 = [], 0
    for q, r in zip(qs, rs):
        outs.append(o[r0:r0 + q.shape[0]] * r)
        r0 += q.shape[0]
    return outs


def _attn_a_kernel(*refs, latent, lam_init):
    if latent:
        q1_ref, q2_ref, k_ref, v_ref, kc_ref, vc_ref, lam_ref, g_ref, o_ref = refs
    else:
        q1_ref, q2_ref, k_ref, v_ref, lam_ref, g_ref, o_ref = refs
    lm = lam_ref[...]
    lam = (jnp.exp(jnp.sum(lm[0:1] * lm[1:2], axis=-1, keepdims=True))
           - jnp.exp(jnp.sum(lm[2:3] * lm[3:4], axis=-1, keepdims=True)) + lam_init)
    for h in range(A_HEADS):
        sl = slice(A_V * h, A_V * (h + 1))
        ks = [k_ref[:, sl]]
        vs = [v_ref[:, sl]]
        if latent:
            ks.append(kc_ref[:, sl])
            vs.append(vc_ref[:, sl])
        tq = q1_ref.shape[0]
        oc, = _attend([jnp.concatenate([q1_ref[:, sl], q2_ref[:, sl]], axis=0)], [ks], vs)
        o = oc[0:tq] - lam * oc[tq:2 * tq]
        o = _rms(o, g_ref[...]) * (1.0 - lam_init)
        o_ref[:, sl] = _bf(o)


def _attn_a(q1, q2, k, v, cache_k, cache_v, a_lambda, g_sub, layer, seq, latent):
    rows, n = q1.shape
    nb = rows // seq
    lam_init = 0.8 - 0.6 * math.exp(-0.3 * layer)
    kern = functools.partial(_attn_a_kernel, latent=latent, lam_init=lam_init)
    if latent:
        nq = seq // TQ
        past = cache_k.shape[2]
        grid = (nb, nq)
        qs = pl.BlockSpec((TQ, n), lambda b, j: (b * nq + j, 0))
        kv = pl.BlockSpec((seq, n), lambda b, j: (b, 0))
        cs = pl.BlockSpec((None, None, past, n), lambda b, j: (b, layer, 0, 0))
        in_specs = [qs, qs, kv, kv, cs, cs, _layer(a_lambda, layer), _layer(g_sub, layer)]
        args = (q1, q2, k, v, cache_k, cache_v, a_lambda, g_sub)
        sem = ("parallel", "parallel")
        out_spec = qs
    else:
        grid = (nb,)
        bs = pl.BlockSpec((seq, n), lambda b: (b, 0))
        in_specs = [bs, bs, bs, bs, _layer(a_lambda, layer), _layer(g_sub, layer)]
        args = (q1, q2, k, v, a_lambda, g_sub)
        sem = ("parallel",)
        out_spec = bs
    return pl.pallas_call(
        kern, grid=grid, in_specs=in_specs, out_specs=out_spec,
        out_shape=jax.ShapeDtypeStruct((rows, n), BF16),
        compiler_params=_cparams(sem),
        name="attn_a_lat" if latent else "attn_a_ctx",
    )(*args)


def _mla_keys(ckv, kpe, wuk_ref, wuv_ref, gk_ref, bd_ref):
    cb = _bf(ckv)
    kn = _dot(cb, wuk_ref[...]) + jnp.concatenate([kpe] * B_HEADS, axis=1)
    k = kn * lax.rsqrt(_seg_sum_sq(kn, bd_ref) * (1.0 / B_QK) + EPS) * gk_ref[...]
    v = _dot(cb, wuv_ref[...])
    return k, v


def _proj_b_kernel(*refs, latent):
    if latent:
        (h_ref, w_ref, gql_ref, gkvl_ref, wuq_ref, wuk_ref, wuv_ref, gq_ref, gk_ref, bd_ref,
         c_ref, se_ref, so_ref, q_ref, k_ref, v_ref) = refs
    else:
        (h_ref, w_ref, gql_ref, gkvl_ref, wuq_ref, wuk_ref, wuv_ref, gq_ref, gk_ref, bd_ref, _, _,
         q_ref, k_ref, v_ref, nckv_ref, nkpe_ref) = refs
    p = _dot_nt(h_ref[...], w_ref[...])
    bq = p[:, 0:B_QL]
    bkv = p[:, B_QL:B_QL + B_KVL]
    kpe = p[:, B_QL + B_KVL:B_QL + B_KVL + LANES]
    q = _dot(_bf(_rms(bq, gql_ref[...])), wuq_ref[...])
    q = q * lax.rsqrt(_seg_sum_sq(q, bd_ref) * (1.0 / B_QK) + EPS) * gq_ref[...]
    ckv = _rms(bkv, gkvl_ref[...])
    k, v = _mla_keys(ckv, kpe, wuk_ref, wuv_ref, gk_ref, bd_ref)
    if latent:
        q = _rope(q, c_ref, se_ref, so_ref)
        k = _rope(k, c_ref, se_ref, so_ref)
    else:
        nckv_ref[...] = ckv.reshape(nckv_ref.shape)
        nkpe_ref[...] = kpe[:, 0:B_ROPE].reshape(nkpe_ref.shape)
    q_ref[...] = _bf(q * (B_QK ** -0.5 * LOG2E))
    k_ref[...] = _bf(k)
    v_ref[...] = _bf(v)


def _proj_b(h1, w_b, gql, gkvl, wuq, wuk, wuv, gq, gk, bd, rope_tabs, l, row0, rows, latent,
            cache_out=None):
    n = B_HEADS * B_HEAD_PAD
    nv = B_HEADS * B_V
    b0 = row0 // TM
    tab_blocks = rope_tabs[0].shape[0] // TM if latent else 1
    in_specs = [pl.BlockSpec((TM, D_MODEL), lambda i: (i + b0, 0))] + [
        _layer(a, l) for a in (w_b, gql, gkvl, wuq, wuk, wuv, gq, gk)] + [_whole(bd)]
    args = [h1, w_b, gql, gkvl, wuq, wuk, wuv, gq, gk, bd]
    row = lambda w: pl.BlockSpec((TM, w), lambda i: (i, 0))
    out_shape = [jax.ShapeDtypeStruct((rows, n), BF16), jax.ShapeDtypeStruct((rows, n), BF16),
                 jax.ShapeDtypeStruct((rows, nv), BF16)]
    out_specs = [row(n), row(n), row(nv)]
    aliases = {}
    if latent:
        in_specs += [pl.BlockSpec((TM, n), lambda i: (i % tab_blocks, 0))] * 3
        args += list(rope_tabs)
    else:
        for buf in cache_out:
            aliases[len(args)] = len(out_shape)
            in_specs.append(pl.BlockSpec(memory_space=pl.ANY))
            args.append(buf)
            out_shape.append(jax.ShapeDtypeStruct(buf.shape, buf.dtype))
            out_specs.append(_cache_out_spec(buf, l))
    return pl.pallas_call(
        functools.partial(_proj_b_kernel, latent=latent),
        grid=(rows // TM,),
        in_specs=in_specs, out_specs=out_specs, out_shape=out_shape,
        input_output_aliases=aliases,
        compiler_params=_cparams(("parallel",)),
        name="proj_b_lat" if latent else "proj_b_ctx",
    )(*args)


def _cache_b_kernel(ckv_ref, kpe_ref, wuk_ref, wuv_ref, gk_ref, bd_ref, k_ref, v_ref):
    k, v = _mla_keys(ckv_ref[...], kpe_ref[...], wuk_ref, wuv_ref, gk_ref, bd_ref)
    k_ref[...] = _bf(k)
    v_ref[...] = _bf(v)


def _cache_b(ckv, kpe_pad, wuk, wuv, gk, bd):
    db, L, past, _ = ckv.shape
    n = B_HEADS * B_HEAD_PAD
    nv = B_HEADS * B_V
    blk = lambda w: pl.BlockSpec((None, None, past, w), lambda l, b: (b, l, 0, 0))
    wl = lambda a: pl.BlockSpec((None,) + a.shape[1:], lambda l, b: (l, 0, 0))
    return pl.pallas_call(
        _cache_b_kernel,
        grid=(L, db),
        in_specs=[blk(B_KVL), blk(LANES), wl(wuk), wl(wuv), wl(gk),
                  pl.BlockSpec(bd.shape, lambda l, b: (0, 0))],
        out_specs=[blk(n), blk(nv)],
        out_shape=[jax.ShapeDtypeStruct((db, L, past, n), BF16),
                   jax.ShapeDtypeStruct((db, L, past, nv), BF16)],
        compiler_params=_cparams(("parallel", "parallel")),
        name="cache_b_expand",
    )(ckv, kpe_pad, wuk, wuv, gk, bd)


def _attn_b_kernel(*refs, latent):
    if latent:
        q_ref, k_ref, v_ref, kc_ref, vc_ref, o_ref = refs
    else:
        q_ref, k_ref, v_ref, o_ref = refs
    lane = lax.broadcasted_iota(I32, (q_ref.shape[0], 2 * B_V), 1)
    for hp in range(B_HEADS // 2):
        vsl = slice(2 * B_V * hp, 2 * B_V * (hp + 1))
        vs = [v_ref[:, vsl]] + ([vc_ref[:, vsl]] if latent else [])
        qs, kss = [], []
        for h in (2 * hp, 2 * hp + 1):
            sl = slice(B_HEAD_PAD * h, B_HEAD_PAD * (h + 1))
            qs.append(q_ref[:, sl])
            kss.append([k_ref[:, sl]] + ([kc_ref[:, sl]] if latent else []))
        outs = _attend(qs, kss, vs)
        o_ref[:, vsl] = _bf(jnp.where(lane < B_V, outs[0], outs[1]))


def _attn_b(q, k, v, cache_k, cache_v, layer, seq, latent):
    rows, n = q.shape
    nv = v.shape[1]
    nb = rows // seq
    kern = functools.partial(_attn_b_kernel, latent=latent)
    if latent:
        nq = seq // TQ
        past = cache_k.shape[2]
        grid = (nb, nq)
        in_specs = [pl.BlockSpec((TQ, n), lambda b, j: (b * nq + j, 0)),
                    pl.BlockSpec((seq, n), lambda b, j: (b, 0)),
                    pl.BlockSpec((seq, nv), lambda b, j: (b, 0)),
                    pl.BlockSpec((None, None, past, n), lambda b, j: (b, layer, 0, 0)),
                    pl.BlockSpec((None, None, past, nv), lambda b, j: (b, layer, 0, 0))]
        args = (q, k, v, cache_k, cache_v)
        out_spec = pl.BlockSpec((TQ, nv), lambda b, j: (b * nq + j, 0))
        sem = ("parallel", "parallel")
    else:
        grid = (nb,)
        in_specs = [pl.BlockSpec((seq, n), lambda b: (b, 0)),
                    pl.BlockSpec((seq, n), lambda b: (b, 0)),
                    pl.BlockSpec((seq, nv), lambda b: (b, 0))]
        args = (q, k, v)
        out_spec = pl.BlockSpec((seq, nv), lambda b: (b, 0))
        sem = ("parallel",)
    return pl.pallas_call(
        kern, grid=grid, in_specs=in_specs, out_specs=out_spec,
        out_shape=jax.ShapeDtypeStruct((rows, nv), BF16),
        compiler_params=_cparams(sem),
        name="attn_b_lat" if latent else "attn_b_ctx",
    )(*args)


CONV_CHUNK = 64


def _mix_cd_kernel(h_ref, w_ref, dw_ref, dwb_ref, lng_ref, lnb_ref, bdd_ref, dsc_ref, pm_ref, pcnt_ref,
                   oc_ref, od_ref, gpad, dpad, gsh, dsh, *, seq):
    p = _dot_nt(h_ref[...], w_ref[...])
    glu = p[:, 0:C_CH] * jax.nn.sigmoid(p[:, C_CH:2 * C_CH])
    gpad[0:C_PAD, :] = jnp.zeros((C_PAD, C_CH), F32)
    gpad[C_PAD + seq:2 * C_PAD + seq, :] = jnp.zeros((C_PAD, C_CH), F32)
    gpad[C_PAD:C_PAD + seq, :] = glu
    half = C_W // 2
    span = seq + 2 * C_PAD - SUB
    for r in range(1, SUB):
        gsh[(r - 1) * span:r * span, :] = gpad[r:r + span, :]
    for c0 in range(0, seq, CONV_CHUNK):
        acc = jnp.zeros((CONV_CHUNK, C_CH), F32) + dwb_ref[...]
        for j in range(C_W):
            s = c0 + C_PAD - half + j
            r = s % SUB
            if r == 0:
                tap = gpad[s:s + CONV_CHUNK, :]
            else:
                tap = gsh[(r - 1) * span + s - r:(r - 1) * span + s - r + CONV_CHUNK, :]
            acc = acc + tap * dw_ref[j:j + 1, :]
        mu = jnp.mean(acc, axis=-1, keepdims=True)
        xc = acc - mu
        y = xc * lax.rsqrt(jnp.mean(xc * xc, axis=-1, keepdims=True) + EPS)
        y = y * lng_ref[...] + lnb_ref[...]
        oc_ref[c0:c0 + CONV_CHUNK, :] = _bf(y * jax.nn.sigmoid(y))
    d = p[:, 2 * C_CH:2 * C_CH + D_CH]
    dpad[0:D_PAD, :] = jnp.zeros((D_PAD, D_CH), F32)
    dpad[D_PAD + seq:2 * D_PAD + seq, :] = jnp.zeros((D_PAD, D_CH), F32)
    dpad[D_PAD:D_PAD + seq, :] = d
    dspan = seq + 2 * D_PAD - SUB
    for r in range(1, SUB):
        dsh[(r - 1) * dspan:r * dspan, :] = dpad[r:r + dspan, :]
    for c0 in range(0, seq, CONV_CHUNK):
        acc = jnp.zeros((CONV_CHUNK, D_CH), F32)
        for j in range(2 * D_PAD):
            s = c0 + j
            r = s % SUB
            if r == 0:
                tap = dpad[s:s + CONV_CHUNK, :]
            else:
                tap = dsh[(r - 1) * dspan + s - r:(r - 1) * dspan + s - r + CONV_CHUNK, :]
            acc = acc + tap * pm_ref[j:j + 1, :]
        pooled = acc / pcnt_ref[c0:c0 + CONV_CHUNK, :]
        diff = pooled - dpad[c0 + D_PAD:c0 + D_PAD + CONV_CHUNK, :]
        od_ref[c0:c0 + CONV_CHUNK, :] = _bf(_dot(_bf(diff), bdd_ref[...]) * dsc_ref[...])


def _mix_cd(h1, w_cd, dw, dwb, lng, lnb, bdd, dsc, pmask, l, row0, rows, seq):
    b0 = row0 // seq
    pcnt = _pool_counts(seq)
    return pl.pallas_call(
        functools.partial(_mix_cd_kernel, seq=seq),
        grid=(rows // seq,),
        in_specs=[pl.BlockSpec((seq, D_MODEL), lambda b: (b + b0, 0))] + [
            _layer(a, l) for a in (w_cd, dw, dwb, lng, lnb, bdd, dsc)] + [_whole(pmask), _whole(pcnt)],
        out_specs=[pl.BlockSpec((seq, C_CH), lambda b: (b, 0)),
                   pl.BlockSpec((seq, D_CH), lambda b: (b, 0))],
        out_shape=[jax.ShapeDtypeStruct((rows, C_CH), BF16),
                   jax.ShapeDtypeStruct((rows, D_CH), BF16)],
        scratch_shapes=[pltpu.VMEM((seq + 2 * C_PAD, C_CH), F32),
                        pltpu.VMEM((seq + 2 * D_PAD, D_CH), F32),
                        pltpu.VMEM(((SUB - 1) * (seq + 2 * C_PAD - SUB), C_CH), F32),
                        pltpu.VMEM(((SUB - 1) * (seq + 2 * D_PAD - SUB), D_CH), F32)],
        compiler_params=_cparams(("parallel",)),
        name="mix_cd_%d" % seq,
    )(h1, w_cd, dw, dwb, lng, lnb, bdd, dsc, pmask, pcnt)


def _merge_kernel(*refs, moe, n_ctx_tiles):
    (x_ref, h_ref, oac_ref, oal_ref, obc_ref, obl_ref, occ_ref, ocl_ref, odc_ref, odl_ref,
     mod_ref, n2g_ref, wg_ref, wa_ref, wb_ref, wc_ref, wd_ref, wo_ref) = refs[:18]
    if moe:
        rhi_ref, rlo_ref, x1_ref, h2t_ref, ridx_ref, rw_ref, sel_ref = refs[18:]
    else:
        x1_ref, h2_ref = refs[18:]
    is_ctx = pl.program_id(0) < n_ctx_tiles
    h = h_ref[...]
    acc = None
    for i, (c_ref, l_ref, w_ref) in enumerate(((oac_ref, oal_ref, wa_ref), (obc_ref, obl_ref, wb_ref),
                                               (occ_ref, ocl_ref, wc_ref), (odc_ref, odl_ref, wd_ref))):
        gate = jax.nn.sigmoid(_dot_nt(h, wg_ref[i * D_MODEL:(i + 1) * D_MODEL, :]))
        o = jnp.where(is_ctx, c_ref[...], l_ref[...])
        t = gate * _dot(o, w_ref[...])
        acc = t if acc is None else acc + t
    y = _dot(_bf(acc), wo_ref[...])
    x1 = x_ref[...] + mod_ref[:, 2 * D_MODEL:3 * D_MODEL] * y
    x1_ref[...] = x1
    h2 = (_rms(x1, n2g_ref[...]) * (1.0 + mod_ref[:, 4 * D_MODEL:5 * D_MODEL])
          + mod_ref[:, 3 * D_MODEL:4 * D_MODEL])
    if not moe:
        h2_ref[...] = _bf(h2)
        return
    tm = h2.shape[0]
    for j in range(SUB):
        h2t_ref[pl.ds(j, tm, stride=SUB), :] = h2[:, LANES * j:LANES * (j + 1)]
    hi = _bf(h2)
    lo = _bf(h2 - hi.astype(F32))
    logits = _dot(hi, rhi_ref[...]) + _dot(lo, rhi_ref[...]) + _dot(hi, rlo_ref[...])
    lane = lax.broadcasted_iota(I32, logits.shape, 1)
    lanef = lane.astype(F32)
    neg = jnp.float32(-jnp.inf)
    lg = jnp.where(lane < N_EXP, logits, neg)
    m0 = jnp.max(lg, axis=-1, keepdims=True)
    i0 = jnp.min(jnp.where(lg == m0, lanef, float(N_EXP_PAD)), axis=-1, keepdims=True)
    sel0 = lanef == i0
    lg1 = jnp.where(sel0, neg, lg)
    m1 = jnp.max(lg1, axis=-1, keepdims=True)
    i1 = jnp.min(jnp.where(lg1 == m1, lanef, float(N_EXP_PAD)), axis=-1, keepdims=True)
    sel1 = lanef == i1
    e = jnp.exp(m1 - m0)
    w0 = 1.0 / (1.0 + e)
    w1 = e / (1.0 + e)
    ridx_ref[...] = jnp.where(lane == 0, i0, jnp.where(lane == 1, i1, 0.0)).astype(I32)
    rw_ref[...] = jnp.where(lane == 0, w0, jnp.where(lane == 1, w1, 0.0))
    sel_ref[...] = jnp.where(sel0 | sel1, 1.0, 0.0).astype(BF16)


def _merge(x, h1, branches, mod4, n2g, wg, wa, wb, wc, wd, wo, router, l, j, t_ctx, dec_seq):
    T = x.shape[0]
    moe = router is not None
    tm = TM_MERGE
    nc = t_ctx // tm
    row = functools.partial(_mod_row, tm=tm, t_ctx=t_ctx, dec_seq=dec_seq)
    rowspec = lambda w: pl.BlockSpec((tm, w), lambda i: (i, 0))
    ctxspec = lambda w: pl.BlockSpec((tm, w), lambda i: (jnp.minimum(i, nc - 1), 0))
    latspec = lambda w: pl.BlockSpec((tm, w), lambda i: (jnp.maximum(i - nc, 0), 0))
    in_specs = [rowspec(D_MODEL), rowspec(D_MODEL)]
    args = [x, h1]
    for oc, ol in branches:
        in_specs += [ctxspec(oc.shape[1]), latspec(ol.shape[1])]
        args += [oc, ol]
    in_specs += [pl.BlockSpec((None, None, 1, 6 * D_MODEL), lambda i: (l, row(i), 0, 0))]
    in_specs += [_layer(a, l) for a in (n2g, wg, wa, wb, wc, wd, wo)]
    args += [mod4, n2g, wg, wa, wb, wc, wd, wo]
    out_shape = [jax.ShapeDtypeStruct((T, D_MODEL), F32)]
    out_specs = [rowspec(D_MODEL)]
    if moe:
        in_specs += [_layer(router[0], j), _layer(router[1], j)]
        args += list(router)
        out_shape += [jax.ShapeDtypeStruct((T * SUB, LANES), F32),
                      jax.ShapeDtypeStruct((T, N_EXP_PAD), I32),
                      jax.ShapeDtypeStruct((T, N_EXP_PAD), F32),
                      jax.ShapeDtypeStruct((T, N_EXP_PAD), BF16)]
        out_specs += [pl.BlockSpec((tm * SUB, LANES), lambda i: (i, 0)),
                      rowspec(N_EXP_PAD), rowspec(N_EXP_PAD), rowspec(N_EXP_PAD)]
    else:
        out_shape += [jax.ShapeDtypeStruct((T, D_MODEL), BF16)]
        out_specs += [rowspec(D_MODEL)]
    return pl.pallas_call(
        functools.partial(_merge_kernel, moe=moe, n_ctx_tiles=nc),
        grid=(T // tm,),
        in_specs=in_specs, out_specs=out_specs, out_shape=out_shape,
        compiler_params=_cparams(("parallel",)),
        name="merge_moe" if moe else "merge",
    )(*args)


def _swiglu_step(x, w1_ref, w3_ref, w2_ref, acc_ref, f):
    a = _dot(x, _bf(w1_ref[...]))
    b = _dot(x, _bf(w3_ref[...]))
    t = _dot(_bf(a * jax.nn.sigmoid(a) * b), _bf(w2_ref[...]))

    @pl.when(f == 0)
    def _():
        acc_ref[...] = t

    @pl.when(f > 0)
    def _():
        acc_ref[...] += t


def _pre_norm1(x, mod_ref, g_ref):
    return _bf(_rms(x, g_ref[...]) * (1.0 + mod_ref[:, D_MODEL:2 * D_MODEL]) + mod_ref[:, 0:D_MODEL])


def _ffn_dense_kernel(x_ref, w1_ref, w3_ref, w2_ref, x1_ref, mod_ref, modn_ref, gn_ref,
                      o_ref, hn_ref, acc_ref):
    f = pl.program_id(1)
    _swiglu_step(x_ref[...], w1_ref, w3_ref, w2_ref, acc_ref, f)

    @pl.when(f == pl.num_programs(1) - 1)
    def _():
        x2 = x1_ref[...] + mod_ref[:, 5 * D_MODEL:6 * D_MODEL] * acc_ref[...]
        o_ref[...] = x2
        hn_ref[...] = _pre_norm1(x2, modn_ref, gn_ref)


def _ffn_dense(h2, w1, w3, w2, x1, mod4, n1g, l, j, t_ctx, dec_seq):
    T = h2.shape[0]
    nf = w1.shape[2] // TF_DENSE
    row = functools.partial(_mod_row, tm=TG, t_ctx=t_ctx, dec_seq=dec_seq)
    rows = pl.BlockSpec((TG, D_MODEL), lambda i, f: (i, 0))
    return pl.pallas_call(
        _ffn_dense_kernel,
        grid=(T // TG, nf),
        in_specs=[rows,
                  pl.BlockSpec((None, D_MODEL, TF_DENSE), lambda i, f: (j, 0, f)),
                  pl.BlockSpec((None, D_MODEL, TF_DENSE), lambda i, f: (j, 0, f)),
                  pl.BlockSpec((None, TF_DENSE, D_MODEL), lambda i, f: (j, f, 0)),
                  rows,
                  pl.BlockSpec((None, None, 1, 6 * D_MODEL), lambda i, f: (l, row(i), 0, 0)),
                  pl.BlockSpec((None, None, 1, 6 * D_MODEL), lambda i, f: (l + 1, row(i), 0, 0)),
                  _layer(n1g, l + 1)],
        out_specs=[rows, rows],
        out_shape=[jax.ShapeDtypeStruct((T, D_MODEL), F32), jax.ShapeDtypeStruct((T, D_MODEL), BF16)],
        scratch_shapes=[pltpu.VMEM((TG, D_MODEL), F32)],
        compiler_params=_cparams(("parallel", "arbitrary")),
        name="ffn_dense",
    )(h2, w1, w3, w2, x1, mod4, mod4, n1g)


def _ffn_experts_kernel(te_ref, na_ref, x_ref, w1_ref, w3_ref, w2_ref, o_ref, xb_ref, acc_ref):
    i = pl.program_id(0)
    f = pl.program_id(1)
    active = i < na_ref[0]

    @pl.when(jnp.logical_and(active, f == 0))
    def _():
        for j in range(SUB):
            xb_ref[:, LANES * j:LANES * (j + 1)] = _bf(x_ref[pl.ds(j, TG, stride=SUB), :])

    @pl.when(active)
    def _():
        _swiglu_step(xb_ref[...], w1_ref, w3_ref, w2_ref, acc_ref, f)

    @pl.when(jnp.logical_and(active, f == pl.num_programs(1) - 1))
    def _():
        for j in range(SUB):
            o_ref[pl.ds(j, TG, stride=SUB), :] = acc_ref[:, LANES * j:LANES * (j + 1)]

    @pl.when(jnp.logical_and(jnp.logical_not(active), f == 0))
    def _():
        o_ref[...] = jnp.zeros(o_ref.shape, F32)


def _ffn_experts(xs, w1, w3, w2, tile_e, n_active, j):
    rows = xs.shape[0] // SUB
    nf = w1.shape[3] // TF_MOE

    def fidx(i, f, na):
        return jnp.where(i < na[0], f, nf - 1)

    def xidx(i, na):
        return jnp.minimum(i, na[0] - 1)

    return pl.pallas_call(
        _ffn_experts_kernel,
        grid_spec=pltpu.PrefetchScalarGridSpec(
            num_scalar_prefetch=2, grid=(rows // TG, nf),
            in_specs=[pl.BlockSpec((TG * SUB, LANES), lambda i, f, te, na: (xidx(i, na), 0)),
                      pl.BlockSpec((None, None, D_MODEL, TF_MOE),
                                   lambda i, f, te, na: (j, te[i], 0, fidx(i, f, na))),
                      pl.BlockSpec((None, None, D_MODEL, TF_MOE),
                                   lambda i, f, te, na: (j, te[i], 0, fidx(i, f, na))),
                      pl.BlockSpec((None, None, TF_MOE, D_MODEL),
                                   lambda i, f, te, na: (j, te[i], fidx(i, f, na), 0))],
            out_specs=pl.BlockSpec((TG * SUB, LANES), lambda i, f, te, na: (i, 0)),
            scratch_shapes=[pltpu.VMEM((TG, D_MODEL), BF16), pltpu.VMEM((TG, D_MODEL), F32)]),
        out_shape=jax.ShapeDtypeStruct((rows * SUB, LANES), F32),
        compiler_params=_cparams(("parallel", "arbitrary")),
        name="ffn_experts",
    )(tile_e, n_active, xs, w1, w3, w2)


def _rank_kernel(sel_ref, tri_ref, rank_ref, cnt_ref, carry):
    i = pl.program_id(0)

    @pl.when(i == 0)
    def _():
        carry[...] = jnp.zeros(carry.shape, F32)

    s = sel_ref[...]
    rank_ref[...] = _dot(tri_ref[...], s) + carry[...]
    carry[...] += jnp.sum(s.astype(F32), axis=0, keepdims=True)
    cnt_ref[...] = jnp.broadcast_to(carry[...], cnt_ref.shape)


def _rank(sel, tri):
    T = sel.shape[0]
    return pl.pallas_call(
        _rank_kernel,
        grid=(T // TR,),
        in_specs=[pl.BlockSpec((TR, N_EXP_PAD), lambda i: (i, 0)),
                  pl.BlockSpec((TR, TR), lambda i: (0, 0))],
        out_specs=[pl.BlockSpec((TR, N_EXP_PAD), lambda i: (i, 0)),
                   pl.BlockSpec((8, N_EXP_PAD), lambda i: (0, 0))],
        out_shape=[jax.ShapeDtypeStruct((T, N_EXP_PAD), F32),
                   jax.ShapeDtypeStruct((8, N_EXP_PAD), F32)],
        scratch_shapes=[pltpu.VMEM((1, N_EXP_PAD), F32)],
        compiler_params=_cparams(("arbitrary",)),
        name="route_rank",
    )(sel, tri)


TILE_ROWS = TR * SUB


def _token_tile(ref, tok):
    return ref.at[pl.ds(pl.multiple_of(tok * SUB, SUB), SUB)]


def _dispatch_kernel(pos0_ref, pos1_ref, ends_ref, padded_ref, h_ref, xs_hbm, stage, zbuf, sem, zsem):
    i = pl.program_id(0)

    @pl.when(i == 0)
    def _():
        zbuf[...] = jnp.zeros(zbuf.shape, F32)

        def fill_tile(first_slot):
            start = pl.multiple_of(first_slot * SUB, TG * SUB)
            fill = pltpu.make_async_copy(zbuf, xs_hbm.at[pl.ds(start, TG * SUB)], zsem.at[0])
            fill.start()
            fill.wait()

        n_slots = xs_hbm.shape[0] // SUB
        for e in range(N_EXP):
            @pl.when(padded_ref[e] > 0)
            def _():
                fill_tile(ends_ref[e] - TG)

            @pl.when(ends_ref[N_EXP - 1] + e * TG < n_slots)
            def _():
                fill_tile(ends_ref[N_EXP - 1] + e * TG)

    slot = i % 2
    base = pl.multiple_of(slot * TILE_ROWS, TILE_ROWS)
    stage[pl.ds(base, TILE_ROWS), :] = h_ref[...]

    def issue(r, c):
        t = i * TR + r
        src = _token_tile(stage, slot * TR + r)
        pltpu.make_async_copy(src, _token_tile(xs_hbm, pos0_ref[t]), sem.at[slot]).start()
        pltpu.make_async_copy(src, _token_tile(xs_hbm, pos1_ref[t]), sem.at[slot]).start()
        return c
    lax.fori_loop(0, TR, issue, 0, unroll=8)

    def wait_slot(s):
        b = pl.multiple_of(s * TILE_ROWS, TILE_ROWS)
        for _ in range(2):
            pltpu.make_async_copy(stage.at[pl.ds(b, TILE_ROWS)], xs_hbm.at[pl.ds(0, TILE_ROWS)],
                                  sem.at[s]).wait()

    @pl.when(i > 0)
    def _():
        wait_slot(1 - slot)

    @pl.when(i == pl.num_programs(0) - 1)
    def _():
        wait_slot(slot)


def _dispatch(pos0, pos1, ends, padded, h2t, n_rows):
    T = h2t.shape[0] // SUB
    assert n_rows - 2 * T <= N_EXP * TG
    return pl.pallas_call(
        _dispatch_kernel,
        grid_spec=pltpu.PrefetchScalarGridSpec(
            num_scalar_prefetch=4, grid=(T // TR,),
            in_specs=[pl.BlockSpec((TILE_ROWS, LANES), lambda i, *_: (i, 0))],
            out_specs=pl.BlockSpec(memory_space=pl.ANY),
            scratch_shapes=[pltpu.VMEM((2 * TILE_ROWS, LANES), F32),
                            pltpu.VMEM((TG * SUB, LANES), F32),
                            pltpu.SemaphoreType.DMA((2,)),
                            pltpu.SemaphoreType.DMA((1,))]),
        out_shape=jax.ShapeDtypeStruct((n_rows * SUB, LANES), F32),
        compiler_params=_cparams(("arbitrary",)),
        name="moe_dispatch",
    )(pos0, pos1, ends, padded, h2t)


def _combine_kernel(pos0_ref, pos1_ref, y_hbm, x1_ref, rw_ref, mod_ref, *rest, emit_next):
    if emit_next:
        modn_ref, gn_ref, o_ref, hn_ref, buf0, buf1, sem0, sem1 = rest
    else:
        o_ref, buf0, buf1, sem0, sem1 = rest
    i = pl.program_id(0)
    nt = pl.num_programs(0)

    def issue(tile, slot):
        def body(r, c):
            t = tile * TR + r
            pltpu.make_async_copy(_token_tile(y_hbm, pos0_ref[t]), _token_tile(buf0, slot * TR + r),
                                  sem0.at[slot]).start()
            pltpu.make_async_copy(_token_tile(y_hbm, pos1_ref[t]), _token_tile(buf1, slot * TR + r),
                                  sem1.at[slot]).start()
            return c
        lax.fori_loop(0, TR, body, 0, unroll=8)

    @pl.when(i == 0)
    def _():
        issue(0, 0)

    @pl.when(i + 1 < nt)
    def _():
        issue(i + 1, (i + 1) % 2)

    slot = i % 2
    base = pl.multiple_of(slot * TILE_ROWS, TILE_ROWS)
    for buf, sem in ((buf0, sem0), (buf1, sem1)):
        pltpu.make_async_copy(y_hbm.at[pl.ds(0, TILE_ROWS)], buf.at[pl.ds(base, TILE_ROWS)],
                              sem.at[slot]).wait()
    rw = rw_ref[...]
    w0 = rw[:, 0:1]
    w1 = rw[:, 1:2]
    for j in range(SUB):
        sl = slice(LANES * j, LANES * (j + 1))
        f = (w0 * buf0[pl.ds(base + j, TR, stride=SUB), :]
             + w1 * buf1[pl.ds(base + j, TR, stride=SUB), :])
        o_ref[:, sl] = x1_ref[:, sl] + mod_ref[:, 5 * D_MODEL + LANES * j:5 * D_MODEL + LANES * (j + 1)] * f
    if emit_next:
        hn_ref[...] = _pre_norm1(o_ref[...], modn_ref, gn_ref)


def _combine(pos0, pos1, ys, x1, rw, mod4, n1g, l, t_ctx, dec_seq, emit_next):
    T = x1.shape[0]
    row = functools.partial(_mod_row, tm=TR, t_ctx=t_ctx, dec_seq=dec_seq)
    rows = pl.BlockSpec((TR, D_MODEL), lambda i, p0, p1: (i, 0))
    in_specs = [pl.BlockSpec(memory_space=pl.ANY), rows,
                pl.BlockSpec((TR, N_EXP_PAD), lambda i, p0, p1: (i, 0)),
                pl.BlockSpec((None, None, 1, 6 * D_MODEL), lambda i, p0, p1: (l, row(i), 0, 0))]
    args = [pos0, pos1, ys, x1, rw, mod4]
    out_specs = [rows]
    out_shape = [jax.ShapeDtypeStruct((T, D_MODEL), F32)]
    if emit_next:
        in_specs += [pl.BlockSpec((None, None, 1, 6 * D_MODEL), lambda i, p0, p1: (l + 1, row(i), 0, 0)),
                     _layer(n1g, l + 1)]
        args += [mod4, n1g]
        out_specs += [rows]
        out_shape += [jax.ShapeDtypeStruct((T, D_MODEL), BF16)]
    return pl.pallas_call(
        functools.partial(_combine_kernel, emit_next=emit_next),
        grid_spec=pltpu.PrefetchScalarGridSpec(
            num_scalar_prefetch=2, grid=(T // TR,),
            in_specs=in_specs, out_specs=out_specs,
            scratch_shapes=[pltpu.VMEM((2 * TILE_ROWS, LANES), F32), pltpu.VMEM((2 * TILE_ROWS, LANES), F32),
                            pltpu.SemaphoreType.DMA((2,)), pltpu.SemaphoreType.DMA((2,))]),
        out_shape=out_shape,
        compiler_params=_cparams(("arbitrary",)),
        name="moe_combine",
    )(*args)


def _block_ones(n, seg):
    idx = np.arange(n) // seg
    return jnp.asarray((idx[:, None] == idx[None, :]).astype(np.float32), dtype=BF16)


def _rope_tables(seq, head_w, rot_dim, n_heads):
    rows = seq // GRID_W
    nf = rot_dim // 4
    freqs = ROPE_THETA ** (-np.arange(nf, dtype=np.float64) / nf)
    row = np.repeat(np.arange(rows, dtype=np.float64), GRID_W)
    col = np.tile(np.arange(GRID_W, dtype=np.float64), rows)
    ang = np.concatenate([row[:, None] * freqs, col[:, None] * freqs], axis=-1)
    ang = np.repeat(ang, 2, axis=-1)
    even = (np.arange(rot_dim) % 2 == 0)[None, :]
    c = np.ones((seq, head_w))
    se = np.zeros((seq, head_w))
    so = np.zeros((seq, head_w))
    c[:, :rot_dim] = np.cos(ang)
    se[:, :rot_dim] = np.where(even, -np.sin(ang), 0.0)
    so[:, :rot_dim] = np.where(even, 0.0, np.sin(ang))
    return tuple(jnp.asarray(np.tile(t, (1, n_heads)), dtype=F32) for t in (c, se, so))


def _pool_mask():
    m = np.zeros((2 * D_PAD, D_CH), np.float32)
    for gi, w in enumerate(POOL):
        left = w // 2
        right = w - 1 - left
        for off in range(-left, right + 1):
            m[off + D_PAD, gi * D_G:(gi + 1) * D_G] = 1.0
    return jnp.asarray(m)


def _pool_counts(seq):
    t = np.arange(seq)
    m = np.zeros((seq, D_CH), np.float32)
    for gi, w in enumerate(POOL):
        left = w // 2
        right = w - 1 - left
        m[:, gi * D_G:(gi + 1) * D_G] = (np.minimum(t + right + 1, seq) - np.maximum(t - left, 0))[:, None]
    return jnp.asarray(m)


def _b_heads(nope, rope):
    ref = nope if nope is not None else rope
    lead = ref.shape[:-1]
    z = lambda w: jnp.zeros(lead + (w,), ref.dtype)
    parts = [rope if rope is not None else z(B_ROPE), nope if nope is not None else z(B_NOPE),
             z(B_HEAD_PAD - B_QK)]
    out = jnp.concatenate(parts, axis=-1)
    return out.reshape(lead[:-1] + (lead[-1] * B_HEAD_PAD,))


def kernel(x_prompt, x_sample, cache_a_k, cache_a_v, cache_b_ckv, cache_b_kpe, c, c_ctx, ada_w, ada_b, norm1_g, norm2_g, w_in, a_q_norm, a_k_norm, a_lambda, a_sub_norm, a_w_o, b_q_lora_norm, b_kv_lora_norm, b_w_uq, b_w_ukv, b_q_norm, b_k_norm, b_w_o, c_dw, c_dw_b, c_ln_g, c_ln_b, c_w_o, d_w_group, d_scale, d_w_o, w_out, ffn_w1, ffn_w3, ffn_w2, moe_router, moe_w1, moe_w3, moe_w2):
    nb, seq, _ = x_prompt.shape
    db, dseq, _ = x_sample.shape
    L = w_in.shape[0]
    past = cache_a_k.shape[2]
    t_ctx = nb * seq
    t_lat = db * dseq
    T = t_ctx + t_lat
    na = 2 * A_HEADS * A_QK
    assert t_ctx % TG == 0 and dseq % TG == 0 and seq % TR == 0 and db + 1 <= 16

    x = jnp.concatenate([x_prompt.reshape(t_ctx, D_MODEL), x_sample.reshape(t_lat, D_MODEL)], axis=0)
    cv = jnp.concatenate([c_ctx[None, :], c, jnp.zeros((15 - db, D_MODEL), F32)], axis=0)
    mod4 = _ada_all(cv, ada_w, ada_b).reshape(L, 16, 1, 6 * D_MODEL)

    bd_a = _block_ones(na, A_QK)
    bd_b = _block_ones(B_HEADS * B_HEAD_PAD, B_HEAD_PAD)
    rope_a = _rope_tables(dseq, A_QK, A_QK, 2 * A_HEADS)
    rope_b = _rope_tables(dseq, B_HEAD_PAD, B_ROPE, B_HEADS)
    pmask = _pool_mask()
    tri = jnp.asarray(np.tril(np.ones((TR, TR), np.float32), -1), dtype=BF16)

    row1 = lambda a: a.reshape(L, 1, -1)
    w_a, w_b, w_cd, w_g = _repack(jnp.transpose(w_in, (0, 2, 1)))
    gq_a = row1(jnp.tile(a_q_norm, (1, 2 * A_HEADS)))
    gk_a = row1(jnp.tile(a_k_norm, (1, 2 * A_HEADS)))
    uq = b_w_uq.reshape(L, B_QL, B_HEADS, B_QK)
    wuq = _b_heads(uq[..., :B_NOPE], uq[..., B_NOPE:]).astype(BF16)
    ukv = b_w_ukv.reshape(L, B_KVL, B_HEADS, B_NOPE + B_V)
    wuk = _b_heads(ukv[..., :B_NOPE], None).astype(BF16)
    wuv = ukv[..., B_NOPE:].reshape(L, B_KVL, B_HEADS * B_V).astype(BF16)
    gq_b = row1(jnp.tile(_b_heads(b_q_norm[:, None, :B_NOPE], b_q_norm[:, None, B_NOPE:]), (1, B_HEADS)))
    gk_b = row1(jnp.tile(_b_heads(b_k_norm[:, None, :B_NOPE], b_k_norm[:, None, B_NOPE:]), (1, B_HEADS)))
    bdd = jnp.zeros((L, D_CH, D_CH), F32)
    for gi in range(len(POOL)):
        bdd = bdd.at[:, gi * D_G:(gi + 1) * D_G, gi * D_G:(gi + 1) * D_G].set(d_w_group[:, gi])
    bdd = bdd.astype(BF16)
    a_wo, b_wo, c_wo, d_wo, wo = (w.astype(BF16) for w in (a_w_o, b_w_o, c_w_o, d_w_o, w_out))
    ffn1, ffn3, ffn2 = (w.astype(BF16) for w in (ffn_w1, ffn_w3, ffn_w2))
    r_pad = jnp.pad(moe_router, ((0, 0), (0, 0), (0, N_EXP_PAD - N_EXP)))
    r_hi = r_pad.astype(BF16)
    r_lo = (r_pad - r_hi.astype(F32)).astype(BF16)
    n1g, n2g = row1(norm1_g), row1(norm2_g)
    g_sub = row1(a_sub_norm)
    gql, gkvl = row1(b_q_lora_norm), row1(b_kv_lora_norm)
    dwb, lng, lnb, dsc = row1(c_dw_b), row1(c_ln_g), row1(c_ln_b), row1(d_scale)

    ck_a = cache_a_k.reshape(db, L, past, na).astype(BF16)
    cv_a = cache_a_v.reshape(db, L, past, A_HEADS * A_V).astype(BF16)
    kpe_pad = jnp.pad(cache_b_kpe, ((0, 0), (0, 0), (0, 0), (0, LANES - B_ROPE)))
    ck_b, cv_b = _cache_b(cache_b_ckv, kpe_pad, wuk, wuv, gk_b, bd_b)

    n_rows = 2 * T + N_EXP * TG
    nt_g = n_rows // TG

    new_ak = jnp.zeros((nb, L, seq, na), F32)
    new_av = jnp.zeros((nb, L, seq, A_HEADS * A_V), F32)
    new_ckv = jnp.zeros((nb, L, seq, B_KVL), F32)
    new_kpe = jnp.zeros((nb, L, seq, B_ROPE), F32)
    h1 = _prep(x, mod4, n1g, 0, t_ctx, dseq)
    for l in range(L):
        last = l == L - 1

        q1c, q2c, kc, vc, new_ak, new_av = _proj_a(h1, w_a, gq_a, gk_a, bd_a, None, l, 0, t_ctx, False,
                                                   cache_out=(new_ak, new_av))
        q1l, q2l, kl, vl = _proj_a(h1, w_a, gq_a, gk_a, bd_a, rope_a, l, t_ctx, t_lat, True)
        oa = (_attn_a(q1c, q2c, kc, vc, None, None, a_lambda, g_sub, l, seq, False),
              _attn_a(q1l, q2l, kl, vl, ck_a, cv_a, a_lambda, g_sub, l, dseq, True))

        bargs = (w_b, gql, gkvl, wuq, wuk, wuv, gq_b, gk_b, bd_b)
        qc, kc, vc, new_ckv, new_kpe = _proj_b(h1, *bargs, None, l, 0, t_ctx, False,
                                               cache_out=(new_ckv, new_kpe))
        ql, kl, vl = _proj_b(h1, *bargs, rope_b, l, t_ctx, t_lat, True)
        ob = (_attn_b(qc, kc, vc, None, None, l, seq, False),
              _attn_b(ql, kl, vl, ck_b, cv_b, l, dseq, True))

        cdargs = (w_cd, c_dw, dwb, lng, lnb, bdd, dsc, pmask)
        occ, odc = _mix_cd(h1, *cdargs, l, 0, t_ctx, seq)
        ocl, odl = _mix_cd(h1, *cdargs, l, t_ctx, t_lat, dseq)

        j = l // 2
        moe = l % 2 == 1
        router = (r_hi, r_lo) if moe else None
        outs = _merge(x, h1, (oa, ob, (occ, ocl), (odc, odl)), mod4, n2g, w_g, a_wo, b_wo,
                      c_wo, d_wo, wo, router, l, j, t_ctx, dseq)
        if not moe:
            x1, h2 = outs
            assert not last, "a dense layer is always followed by another layer"
            x, h1 = _ffn_dense(h2, ffn1, ffn3, ffn2, x1, mod4, n1g, l, j, t_ctx, dseq)
        else:
            x1, h2t, ridx, rw, sel = outs
            rank, cnt = _rank(sel, tri)
            counts = cnt[0, :N_EXP].astype(I32)
            padded = ((counts + TG - 1) // TG) * TG
            ends = jnp.cumsum(padded)
            offs = ends - padded
            pos_all = offs[None, :] + rank[:, :N_EXP].astype(I32)
            pos0 = jnp.take_along_axis(pos_all, ridx[:, 0:1], axis=1)[:, 0]
            pos1 = jnp.take_along_axis(pos_all, ridx[:, 1:2], axis=1)[:, 0]
            tile_start = jnp.arange(nt_g, dtype=I32) * TG
            tile_e = jnp.minimum(jnp.sum(tile_start[:, None] >= ends[None, :], axis=1), N_EXP - 1).astype(I32)
            n_active = (ends[-1] // TG).astype(I32).reshape(1)
            last_e = tile_e[jnp.maximum(n_active[0] - 1, 0)]
            tile_e = jnp.where(tile_start < ends[-1], tile_e, last_e)
            xs = _dispatch(pos0, pos1, ends.astype(I32), padded, h2t, n_rows)
            ys = _ffn_experts(xs, moe_w1, moe_w3, moe_w2, tile_e, n_active, j)
            outs = _combine(pos0, pos1, ys, x1, rw, mod4, n1g, l, t_ctx, dseq, not last)
            x, h1 = (outs[0], None) if last else outs

    y_prompt = x[:t_ctx].reshape(nb, seq, D_MODEL)
    y_sample = x[t_ctx:].reshape(db, dseq, D_MODEL)
    new_a_k = new_ak.reshape(nb, L, seq, 2 * A_HEADS, A_QK)
    new_a_v = new_av.reshape(nb, L, seq, A_HEADS, A_V)
    return (y_prompt, y_sample, new_a_k, new_a_v, new_ckv, new_kpe)
```

```python
import functools
import math

import numpy as np
import jax
import jax.numpy as jnp
from jax import lax
from jax.experimental import pallas as pl
from jax.experimental.pallas import tpu as pltpu

F32 = jnp.float32
BF16 = jnp.bfloat16
I32 = jnp.int32

EPS = 1e-6
D_MODEL = 1024
GRID_W = 64
ROPE_THETA = 10000.0
A_HEADS = 4
A_QK = 64
A_V = 128
B_HEADS = 4
B_NOPE = 64
B_ROPE = 32
B_QK = B_NOPE + B_ROPE
B_V = 64
B_QL = 256
B_KVL = 128
B_HEAD_PAD = 128
C_CH = 256
C_W = 31
C_PAD = 16
POOL = (2, 4, 8, 16)
D_G = 64
D_CH = D_G * len(POOL)
D_PAD = 8
N_EXP = 8
N_EXP_PAD = 128
LANES = 128
SUB = 8

O_AQ, O_AK, O_AV, O_BQ, O_BKV, O_KPE, O_C, O_D, O_G, O_END = (
    0, 512, 1024, 1536, 1792, 1920, 1952, 2464, 2720, 6816)
W_B_COLS = (O_KPE - O_BQ) + LANES

TM = 1024
TM_MERGE = 512
TQ = 512
TG = 1024
TR = 256
TF_DENSE = 1408
TF_MOE = 512
VMEM_LIMIT = 56 * 1024 * 1024


def _cparams(sem):
    return pltpu.CompilerParams(dimension_semantics=sem, vmem_limit_bytes=VMEM_LIMIT)


def _dot(a, b):
    return jnp.dot(a, b, preferred_element_type=F32)


def _dot_nt(a, b):
    return lax.dot_general(a, b, (((1,), (1,)), ((), ())), preferred_element_type=F32)


def _bf(x):
    return x.astype(BF16)


def _rms(x, g):
    return x * lax.rsqrt(jnp.mean(x * x, axis=-1, keepdims=True) + EPS) * g


def _seg_sum_sq(x, bd_ref):
    return _dot(_bf(x * x), bd_ref[...])


def _rope(x, c_ref, se_ref, so_ref):
    n = x.shape[-1]
    return (x * c_ref[...] + pltpu.roll(x, n - 1, 1) * se_ref[...]
            + pltpu.roll(x, 1, 1) * so_ref[...])


def _mod_row(i, tm, t_ctx, dec_seq):
    r = i * tm
    return jnp.where(r < t_ctx, 0, 1 + (r - t_ctx) // dec_seq)


def _layer(a, l):
    nd = a.ndim
    return pl.BlockSpec((None,) + a.shape[1:], lambda *_: (l,) + (0,) * (nd - 1),
                        pipeline_mode=pl.Buffered(1))


def _whole(a):
    nd = a.ndim
    return pl.BlockSpec(a.shape, lambda *_: (0,) * nd, pipeline_mode=pl.Buffered(1))


def _ada_kernel(cv_ref, w_ref, b_ref, o_ref):
    cv = cv_ref[...]
    s = cv * jax.nn.sigmoid(cv)
    o_ref[...] = _dot(_bf(s), _bf(w_ref[...])) + b_ref[...]


def _ada_all(cv, ada_w, ada_b):
    L, d, n = ada_w.shape
    tn = 1536
    return pl.pallas_call(
        _ada_kernel,
        grid=(L, n // tn),
        in_specs=[pl.BlockSpec((16, d), lambda l, j: (0, 0)),
                  pl.BlockSpec((None, d, tn), lambda l, j: (l, 0, j)),
                  pl.BlockSpec((None, 1, tn), lambda l, j: (l, 0, j))],
        out_specs=pl.BlockSpec((None, 16, tn), lambda l, j: (l, 0, j)),
        out_shape=jax.ShapeDtypeStruct((L, 16, n), F32),
        compiler_params=_cparams(("parallel", "parallel")),
        name="ada_mod",
    )(cv, ada_w, ada_b.reshape(L, 1, n))


REPACK_ROWS = 1024
REPACK_GROUPS = ((O_AQ, O_BQ - O_AQ, 0, 0), (O_BQ, O_C - O_BQ, 1, 0),
                 (O_C, O_G - O_C, 2, 0), (O_G, O_END - O_G, 3, 0))


def _repack_kernel(w_ref, wa_ref, wb_ref, wcd_ref, wg_ref):
    outs = (wa_ref, wb_ref, wcd_ref, wg_ref)
    r = pl.program_id(1)
    for step in range(pl.cdiv(O_END, REPACK_ROWS)):
        lo, hi = step * REPACK_ROWS, (step + 1) * REPACK_ROWS

        @pl.when(r == step)
        def _():
            for first, rows, dst, dst_first in REPACK_GROUPS:
                a, b = max(first, lo), min(first + rows, hi)
                if a < b:
                    outs[dst][dst_first + a - first:dst_first + b - first, :] = _bf(w_ref[a - lo:b - lo, :])

    @pl.when(r == 0)
    def _():
        wb_ref[O_C - O_BQ:W_B_COLS, :] = jnp.zeros((W_B_COLS - (O_C - O_BQ), D_MODEL), BF16)


def _repack(w_t):
    L, n, d = w_t.shape
    heights = (O_BQ - O_AQ, W_B_COLS, O_G - O_C, O_END - O_G)
    return pl.pallas_call(
        _repack_kernel,
        grid=(L, pl.cdiv(n, REPACK_ROWS)),
        in_specs=[pl.BlockSpec((None, REPACK_ROWS, d), lambda l, r: (l, r, 0))],
        out_specs=[pl.BlockSpec((None, h, d), lambda l, r: (l, 0, 0)) for h in heights],
        out_shape=[jax.ShapeDtypeStruct((L, h, d), BF16) for h in heights],
        compiler_params=_cparams(("parallel", "arbitrary")),
        name="repack_w_in",
    )(w_t)


def _prep_kernel(x_ref, mod_ref, g_ref, h_ref):
    h_ref[...] = _pre_norm1(x_ref[...], mod_ref, g_ref)


def _prep(x, mod4, g, l, t_ctx, dec_seq):
    T = x.shape[0]
    row = functools.partial(_mod_row, tm=TM, t_ctx=t_ctx, dec_seq=dec_seq)
    return pl.pallas_call(
        _prep_kernel,
        grid=(T // TM,),
        in_specs=[pl.BlockSpec((TM, D_MODEL), lambda i: (i, 0)),
                  pl.BlockSpec((None, None, 1, 6 * D_MODEL), lambda i: (l, row(i), 0, 0)),
                  _layer(g, l)],
        out_specs=pl.BlockSpec((TM, D_MODEL), lambda i: (i, 0)),
        out_shape=jax.ShapeDtypeStruct((T, D_MODEL), BF16),
        compiler_params=_cparams(("parallel",)),
        name="prep",
    )(x, mod4, g)


def _proj_a_kernel(*refs, latent):
    if latent:
        (h_ref, w_ref, gq_ref, gk_ref, bd_ref, c_ref, se_ref, so_ref,
         q1_ref, q2_ref, k_ref, v_ref) = refs
    else:
        (h_ref, w_ref, gq_ref, gk_ref, bd_ref, _, _,
         q1_ref, q2_ref, k_ref, v_ref, nk_ref, nv_ref) = refs
    p = _dot_nt(h_ref[...], w_ref[...])
    n = 2 * A_HEADS * A_QK
    q = p[:, 0:n]
    k = p[:, n:2 * n]
    v = p[:, 2 * n:3 * n]
    q = q * lax.rsqrt(_seg_sum_sq(q, bd_ref) * (1.0 / A_QK) + EPS) * gq_ref[...]
    k = k * lax.rsqrt(_seg_sum_sq(k, bd_ref) * (1.0 / A_QK) + EPS) * gk_ref[...]
    if latent:
        q = _rope(q, c_ref, se_ref, so_ref)
        k = _rope(k, c_ref, se_ref, so_ref)
    else:
        nk_ref[...] = k.reshape(nk_ref.shape)
        nv_ref[...] = v.reshape(nv_ref.shape)
    q = q * (A_QK ** -0.5 * LOG2E)
    lane = lax.broadcasted_iota(I32, q.shape, 1)
    first = (lane % (2 * A_QK)) < A_QK
    q1_ref[...] = _bf(jnp.where(first, q, 0.0))
    q2_ref[...] = _bf(jnp.where(first, 0.0, q))
    k_ref[...] = _bf(k)
    v_ref[...] = _bf(v)


def _cache_out_spec(buf, l):
    nb, _, seq, w = buf.shape
    return pl.BlockSpec((TM // seq, None, seq, w), lambda i: (i, l, 0, 0))


def _proj_a(h1, w_a, gq, gk, bd, rope_tabs, l, row0, rows, latent, cache_out=None):
    n = 2 * A_HEADS * A_QK
    b0 = row0 // TM
    tab_blocks = rope_tabs[0].shape[0] // TM if latent else 1
    in_specs = [pl.BlockSpec((TM, D_MODEL), lambda i: (i + b0, 0)),
                _layer(w_a, l), _layer(gq, l), _layer(gk, l), _whole(bd)]
    args = [h1, w_a, gq, gk, bd]
    out_spec = pl.BlockSpec((TM, n), lambda i: (i, 0))
    out_shape = [jax.ShapeDtypeStruct((rows, n), BF16)] * 4
    out_specs = [out_spec] * 4
    aliases = {}
    if latent:
        in_specs += [pl.BlockSpec((TM, n), lambda i: (i % tab_blocks, 0))] * 3
        args += list(rope_tabs)
    else:
        for buf in cache_out:
            aliases[len(args)] = len(out_shape)
            in_specs.append(pl.BlockSpec(memory_space=pl.ANY))
            args.append(buf)
            out_shape.append(jax.ShapeDtypeStruct(buf.shape, buf.dtype))
            out_specs.append(_cache_out_spec(buf, l))
    return pl.pallas_call(
        functools.partial(_proj_a_kernel, latent=latent),
        grid=(rows // TM,),
        in_specs=in_specs, out_specs=out_specs, out_shape=out_shape,
        input_output_aliases=aliases,
        compiler_params=_cparams(("parallel",)),
        name="proj_a_lat" if latent else "proj_a_ctx",
    )(*args)


LOG2E = math.log2(math.e)


def _attend(qs, kss, vs):
    es, rs = [], []
    for q, ks in zip(qs, kss):
        scores = [_dot_nt(q, k) for k in ks]
        m = functools.reduce(jnp.maximum, [jnp.max(s, axis=-1, keepdims=True) for s in scores])
        e = [jnp.exp2(s - m) for s in scores]
        l = functools.reduce(lambda a, b: a + b, [jnp.sum(x, axis=-1, keepdims=True) for x in e])
        es.append([_bf(x) for x in e])
        rs.append(1.0 / l)
    o = None
    for p, v in enumerate(vs):
        stacked = es[0][p] if len(qs) == 1 else jnp.concatenate([e[p] for e in es], axis=0)
        t = _dot(stacked, v)
        o = t if o is None else o + t
    outs, r0 = [], 0
    for q, r in zip(qs, rs):
        outs.append(o[r0:r0 + q.shape[0]] * r)
        r0 += q.shape[0]
    return outs


def _attn_a_kernel(*refs, latent, lam_init):
    if latent:
        q1_ref, q2_ref, k_ref, v_ref, kc_ref, vc_ref, lam_ref, g_ref, o_ref = refs
    else:
        q1_ref, q2_ref, k_ref, v_ref, lam_ref, g_ref, o_ref = refs
    lm = lam_ref[...]
    lam = (jnp.exp(jnp.sum(lm[0:1] * lm[1:2], axis=-1, keepdims=True))
           - jnp.exp(jnp.sum(lm[2:3] * lm[3:4], axis=-1, keepdims=True)) + lam_init)
    for h in range(A_HEADS):
        sl = slice(A_V * h, A_V * (h + 1))
        ks = [k_ref[:, sl]]
        vs = [v_ref[:, sl]]
        if latent:
            ks.append(kc_ref[:, sl])
            vs.append(vc_ref[:, sl])
        tq = q1_ref.shape[0]
        oc, = _attend([jnp.concatenate([q1_ref[:, sl], q2_ref[:, sl]], axis=0)], [ks], vs)
        o = oc[0:tq] - lam * oc[tq:2 * tq]
        o = _rms(o, g_ref[...]) * (1.0 - lam_init)
        o_ref[:, sl] = _bf(o)


def _attn_a(q1, q2, k, v, cache_k, cache_v, a_lambda, g_sub, layer, seq, latent):
    rows, n = q1.shape
    nb = rows // seq
    lam_init = 0.8 - 0.6 * math.exp(-0.3 * layer)
    kern = functools.partial(_attn_a_kernel, latent=latent, lam_init=lam_init)
    if latent:
        nq = seq // TQ
        past = cache_k.shape[2]
        grid = (nb, nq)
        qs = pl.BlockSpec((TQ, n), lambda b, j: (b * nq + j, 0))
        kv = pl.BlockSpec((seq, n), lambda b, j: (b, 0))
        cs = pl.BlockSpec((None, None, past, n), lambda b, j: (b, layer, 0, 0))
        in_specs = [qs, qs, kv, kv, cs, cs, _layer(a_lambda, layer), _layer(g_sub, layer)]
        args = (q1, q2, k, v, cache_k, cache_v, a_lambda, g_sub)
        sem = ("parallel", "parallel")
        out_spec = qs
    else:
        grid = (nb,)
        bs = pl.BlockSpec((seq, n), lambda b: (b, 0))
        in_specs = [bs, bs, bs, bs, _layer(a_lambda, layer), _layer(g_sub, layer)]
        args = (q1, q2, k, v, a_lambda, g_sub)
        sem = ("parallel",)
        out_spec = bs
    return pl.pallas_call(
        kern, grid=grid, in_specs=in_specs, out_specs=out_spec,
        out_shape=jax.ShapeDtypeStruct((rows, n), BF16),
        compiler_params=_cparams(sem),
        name="attn_a_lat" if latent else "attn_a_ctx",
    )(*args)


def _mla_keys(ckv, kpe, wuk_ref, wuv_ref, gk_ref, bd_ref):
    cb = _bf(ckv)
    kn = _dot(cb, wuk_ref[...]) + jnp.concatenate([kpe] * B_HEADS, axis=1)
    k = kn * lax.rsqrt(_seg_sum_sq(kn, bd_ref) * (1.0 / B_QK) + EPS) * gk_ref[...]
    v = _dot(cb, wuv_ref[...])
    return k, v


def _proj_b_kernel(*refs, latent):
    if latent:
        (h_ref, w_ref, gql_ref, gkvl_ref, wuq_ref, wuk_ref, wuv_ref, gq_ref, gk_ref, bd_ref,
         c_ref, se_ref, so_ref, q_ref, k_ref, v_ref) = refs
    else:
        (h_ref, w_ref, gql_ref, gkvl_ref, wuq_ref, wuk_ref, wuv_ref, gq_ref, gk_ref, bd_ref, _, _,
         q_ref, k_ref, v_ref, nckv_ref, nkpe_ref) = refs
    p = _dot_nt(h_ref[...], w_ref[...])
    bq = p[:, 0:B_QL]
    bkv = p[:, B_QL:B_QL + B_KVL]
    kpe = p[:, B_QL + B_KVL:B_QL + B_KVL + LANES]
    q = _dot(_bf(_rms(bq, gql_ref[...])), wuq_ref[...])
    q = q * lax.rsqrt(_seg_sum_sq(q, bd_ref) * (1.0 / B_QK) + EPS) * gq_ref[...]
    ckv = _rms(bkv, gkvl_ref[...])
    k, v = _mla_keys(ckv, kpe, wuk_ref, wuv_ref, gk_ref, bd_ref)
    if latent:
        q = _rope(q, c_ref, se_ref, so_ref)
        k = _rope(k, c_ref, se_ref, so_ref)
    else:
        nckv_ref[...] = ckv.reshape(nckv_ref.shape)
        nkpe_ref[...] = kpe[:, 0:B_ROPE].reshape(nkpe_ref.shape)
    q_ref[...] = _bf(q * (B_QK ** -0.5 * LOG2E))
    k_ref[...] = _bf(k)
    v_ref[...] = _bf(v)


def _proj_b(h1, w_b, gql, gkvl, wuq, wuk, wuv, gq, gk, bd, rope_tabs, l, row0, rows, latent,
            cache_out=None):
    n = B_HEADS * B_HEAD_PAD
    nv = B_HEADS * B_V
    b0 = row0 // TM
    tab_blocks = rope_tabs[0].shape[0] // TM if latent else 1
    in_specs = [pl.BlockSpec((TM, D_MODEL), lambda i: (i + b0, 0))] + [
        _layer(a, l) for a in (w_b, gql, gkvl, wuq, wuk, wuv, gq, gk)] + [_whole(bd)]
    args = [h1, w_b, gql, gkvl, wuq, wuk, wuv, gq, gk, bd]
    row = lambda w: pl.BlockSpec((TM, w), lambda i: (i, 0))
    out_shape = [jax.ShapeDtypeStruct((rows, n), BF16), jax.ShapeDtypeStruct((rows, n), BF16),
                 jax.ShapeDtypeStruct((rows, nv), BF16)]
    out_specs = [row(n), row(n), row(nv)]
    aliases = {}
    if latent:
        in_specs += [pl.BlockSpec((TM, n), lambda i: (i % tab_blocks, 0))] * 3
        args += list(rope_tabs)
    else:
        for buf in cache_out:
            aliases[len(args)] = len(out_shape)
            in_specs.append(pl.BlockSpec(memory_space=pl.ANY))
            args.append(buf)
            out_shape.append(jax.ShapeDtypeStruct(buf.shape, buf.dtype))
            out_specs.append(_cache_out_spec(buf, l))
    return pl.pallas_call(
        functools.partial(_proj_b_kernel, latent=latent),
        grid=(rows // TM,),
        in_specs=in_specs, out_specs=out_specs, out_shape=out_shape,
        input_output_aliases=aliases,
        compiler_params=_cparams(("parallel",)),
        name="proj_b_lat" if latent else "proj_b_ctx",
    )(*args)


def _cache_b_kernel(ckv_ref, kpe_ref, wuk_ref, wuv_ref, gk_ref, bd_ref, k_ref, v_ref):
    k, v = _mla_keys(ckv_ref[...], kpe_ref[...], wuk_ref, wuv_ref, gk_ref, bd_ref)
    k_ref[...] = _bf(k)
    v_ref[...] = _bf(v)


def _cache_b(ckv, kpe_pad, wuk, wuv, gk, bd):
    db, L, past, _ = ckv.shape
    n = B_HEADS * B_HEAD_PAD
    nv = B_HEADS * B_V
    blk = lambda w: pl.BlockSpec((None, None, past, w), lambda l, b: (b, l, 0, 0))
    wl = lambda a: pl.BlockSpec((None,) + a.shape[1:], lambda l, b: (l, 0, 0))
    return pl.pallas_call(
        _cache_b_kernel,
        grid=(L, db),
        in_specs=[blk(B_KVL), blk(LANES), wl(wuk), wl(wuv), wl(gk),
                  pl.BlockSpec(bd.shape, lambda l, b: (0, 0))],
        out_specs=[blk(n), blk(nv)],
        out_shape=[jax.ShapeDtypeStruct((db, L, past, n), BF16),
                   jax.ShapeDtypeStruct((db, L, past, nv), BF16)],
        compiler_params=_cparams(("parallel", "parallel")),
        name="cache_b_expand",
    )(ckv, kpe_pad, wuk, wuv, gk, bd)


def _attn_b_kernel(*refs, latent):
    if latent:
        q_ref, k_ref, v_ref, kc_ref, vc_ref, o_ref = refs
    else:
        q_ref, k_ref, v_ref, o_ref = refs
    lane = lax.broadcasted_iota(I32, (q_ref.shape[0], 2 * B_V), 1)
    for hp in range(B_HEADS // 2):
        vsl = slice(2 * B_V * hp, 2 * B_V * (hp + 1))
        vs = [v_ref[:, vsl]] + ([vc_ref[:, vsl]] if latent else [])
        qs, kss = [], []
        for h in (2 * hp, 2 * hp + 1):
            sl = slice(B_HEAD_PAD * h, B_HEAD_PAD * (h + 1))
            qs.append(q_ref[:, sl])
            kss.append([k_ref[:, sl]] + ([kc_ref[:, sl]] if latent else []))
        outs = _attend(qs, kss, vs)
        o_ref[:, vsl] = _bf(jnp.where(lane < B_V, outs[0], outs[1]))


def _attn_b(q, k, v, cache_k, cache_v, layer, seq, latent):
    rows, n = q.shape
    nv = v.shape[1]
    nb = rows // seq
    kern = functools.partial(_attn_b_kernel, latent=latent)
    if latent:
        nq = seq // TQ
        past = cache_k.shape[2]
        grid = (nb, nq)
        in_specs = [pl.BlockSpec((TQ, n), lambda b, j: (b * nq + j, 0)),
                    pl.BlockSpec((seq, n), lambda b, j: (b, 0)),
                    pl.BlockSpec((seq, nv), lambda b, j: (b, 0)),
                    pl.BlockSpec((None, None, past, n), lambda b, j: (b, layer, 0, 0)),
                    pl.BlockSpec((None, None, past, nv), lambda b, j: (b, layer, 0, 0))]
        args = (q, k, v, cache_k, cache_v)
        out_spec = pl.BlockSpec((TQ, nv), lambda b, j: (b * nq + j, 0))
        sem = ("parallel", "parallel")
    else:
        grid = (nb,)
        in_specs = [pl.BlockSpec((seq, n), lambda b: (b, 0)),
                    pl.BlockSpec((seq, n), lambda b: (b, 0)),
                    pl.BlockSpec((seq, nv), lambda b: (b, 0))]
        args = (q, k, v)
        out_spec = pl.BlockSpec((seq, nv), lambda b: (b, 0))
        sem = ("parallel",)
    return pl.pallas_call(
        kern, grid=grid, in_specs=in_specs, out_specs=out_spec,
        out_shape=jax.ShapeDtypeStruct((rows, nv), BF16),
        compiler_params=_cparams(sem),
        name="attn_b_lat" if latent else "attn_b_ctx",
    )(*args)


CONV_CHUNK = 64


def _mix_cd_kernel(h_ref, w_ref, dw_ref, dwb_ref, lng_ref, lnb_ref, bdd_ref, dsc_ref, pm_ref, pcnt_ref,
                   oc_ref, od_ref, gpad, dpad, gsh, dsh, *, seq):
    p = _dot_nt(h_ref[...], w_ref[...])
    glu = p[:, 0:C_CH] * jax.nn.sigmoid(p[:, C_CH:2 * C_CH])
    gpad[0:C_PAD, :] = jnp.zeros((C_PAD, C_CH), F32)
    gpad[C_PAD + seq:2 * C_PAD + seq, :] = jnp.zeros((C_PAD, C_CH), F32)
    gpad[C_PAD:C_PAD + seq, :] = glu
    half = C_W // 2
    span = seq + 2 * C_PAD - SUB
    for r in range(1, SUB):
        gsh[(r - 1) * span:r * span, :] = gpad[r:r + span, :]
    for c0 in range(0, seq, CONV_CHUNK):
        acc = jnp.zeros((CONV_CHUNK, C_CH), F32) + dwb_ref[...]
        for j in range(C_W):
            s = c0 + C_PAD - half + j
            r = s % SUB
            if r == 0:
                tap = gpad[s:s + CONV_CHUNK, :]
            else:
                tap = gsh[(r - 1) * span + s - r:(r - 1) * span + s - r + CONV_CHUNK, :]
            acc = acc + tap * dw_ref[j:j + 1, :]
        mu = jnp.mean(acc, axis=-1, keepdims=True)
        xc = acc - mu
        y = xc * lax.rsqrt(jnp.mean(xc * xc, axis=-1, keepdims=True) + EPS)
        y = y * lng_ref[...] + lnb_ref[...]
        oc_ref[c0:c0 + CONV_CHUNK, :] = _bf(y * jax.nn.sigmoid(y))
    d = p[:, 2 * C_CH:2 * C_CH + D_CH]
    dpad[0:D_PAD, :] = jnp.zeros((D_PAD, D_CH), F32)
    dpad[D_PAD + seq:2 * D_PAD + seq, :] = jnp.zeros((D_PAD, D_CH), F32)
    dpad[D_PAD:D_PAD + seq, :] = d
    dspan = seq + 2 * D_PAD - SUB
    for r in range(1, SUB):
        dsh[(r - 1) * dspan:r * dspan, :] = dpad[r:r + dspan, :]
    for c0 in range(0, seq, CONV_CHUNK):
        acc = jnp.zeros((CONV_CHUNK, D_CH), F32)
        for j in range(2 * D_PAD):
            s = c0 + j
            r = s % SUB
            if r == 0:
                tap = dpad[s:s + CONV_CHUNK, :]
            else:
                tap = dsh[(r - 1) * dspan + s - r:(r - 1) * dspan + s - r + CONV_CHUNK, :]
            acc = acc + tap * pm_ref[j:j + 1, :]
        pooled = acc / pcnt_ref[c0:c0 + CONV_CHUNK, :]
        diff = pooled - dpad[c0 + D_PAD:c0 + D_PAD + CONV_CHUNK, :]
        od_ref[c0:c0 + CONV_CHUNK, :] = _bf(_dot(_bf(diff), bdd_ref[...]) * dsc_ref[...])


def _mix_cd(h1, w_cd, dw, dwb, lng, lnb, bdd, dsc, pmask, l, row0, rows, seq):
    b0 = row0 // seq
    pcnt = _pool_counts(seq)
    return pl.pallas_call(
        functools.partial(_mix_cd_kernel, seq=seq),
        grid=(rows // seq,),
        in_specs=[pl.BlockSpec((seq, D_MODEL), lambda b: (b + b0, 0))] + [
            _layer(a, l) for a in (w_cd, dw, dwb, lng, lnb, bdd, dsc)] + [_whole(pmask), _whole(pcnt)],
        out_specs=[pl.BlockSpec((seq, C_CH), lambda b: (b, 0)),
                   pl.BlockSpec((seq, D_CH), lambda b: (b, 0))],
        out_shape=[jax.ShapeDtypeStruct((rows, C_CH), BF16),
                   jax.ShapeDtypeStruct((rows, D_CH), BF16)],
        scratch_shapes=[pltpu.VMEM((seq + 2 * C_PAD, C_CH), F32),
                        pltpu.VMEM((seq + 2 * D_PAD, D_CH), F32),
                        pltpu.VMEM(((SUB - 1) * (seq + 2 * C_PAD - SUB), C_CH), F32),
                        pltpu.VMEM(((SUB - 1) * (seq + 2 * D_PAD - SUB), D_CH), F32)],
        compiler_params=_cparams(("parallel",)),
        name="mix_cd_%d" % seq,
    )(h1, w_cd, dw, dwb, lng, lnb, bdd, dsc, pmask, pcnt)


def _merge_kernel(*refs, moe, n_ctx_tiles):
    (x_ref, h_ref, oac_ref, oal_ref, obc_ref, obl_ref, occ_ref, ocl_ref, odc_ref, odl_ref,
     mod_ref, n2g_ref, wg_ref, wa_ref, wb_ref, wc_ref, wd_ref, wo_ref) = refs[:18]
    if moe:
        rhi_ref, rlo_ref, x1_ref, h2t_ref, ridx_ref, rw_ref, sel_ref = refs[18:]
    else:
        x1_ref, h2_ref = refs[18:]
    is_ctx = pl.program_id(0) < n_ctx_tiles
    h = h_ref[...]
    acc = None
    for i, (c_ref, l_ref, w_ref) in enumerate(((oac_ref, oal_ref, wa_ref), (obc_ref, obl_ref, wb_ref),
                                               (occ_ref, ocl_ref, wc_ref), (odc_ref, odl_ref, wd_ref))):
        gate = jax.nn.sigmoid(_dot_nt(h, wg_ref[i * D_MODEL:(i + 1) * D_MODEL, :]))
        o = jnp.where(is_ctx, c_ref[...], l_ref[...])
        t = gate * _dot(o, w_ref[...])
        acc = t if acc is None else acc + t
    y = _dot(_bf(acc), wo_ref[...])
    x1 = x_ref[...] + mod_ref[:, 2 * D_MODEL:3 * D_MODEL] * y
    x1_ref[...] = x1
    h2 = (_rms(x1, n2g_ref[...]) * (1.0 + mod_ref[:, 4 * D_MODEL:5 * D_MODEL])
          + mod_ref[:, 3 * D_MODEL:4 * D_MODEL])
    if not moe:
        h2_ref[...] = _bf(h2)
        return
    tm = h2.shape[0]
    for j in range(SUB):
        h2t_ref[pl.ds(j, tm, stride=SUB), :] = h2[:, LANES * j:LANES * (j + 1)]
    hi = _bf(h2)
    lo = _bf(h2 - hi.astype(F32))
    logits = _dot(hi, rhi_ref[...]) + _dot(lo, rhi_ref[...]) + _dot(hi, rlo_ref[...])
    lane = lax.broadcasted_iota(I32, logits.shape, 1)
    lanef = lane.astype(F32)
    neg = jnp.float32(-jnp.inf)
    lg = jnp.where(lane < N_EXP, logits, neg)
    m0 = jnp.max(lg, axis=-1, keepdims=True)
    i0 = jnp.min(jnp.where(lg == m0, lanef, float(N_EXP_PAD)), axis=-1, keepdims=True)
    sel0 = lanef == i0
    lg1 = jnp.where(sel0, neg, lg)
    m1 = jnp.max(lg1, axis=-1, keepdims=True)
    i1 = jnp.min(jnp.where(lg1 == m1, lanef, float(N_EXP_PAD)), axis=-1, keepdims=True)
    sel1 = lanef == i1
    e = jnp.exp(m1 - m0)
    w0 = 1.0 / (1.0 + e)
    w1 = e / (1.0 + e)
    ridx_ref[...] = jnp.where(lane == 0, i0, jnp.where(lane == 1, i1, 0.0)).astype(I32)
    rw_ref[...] = jnp.where(lane == 0, w0, jnp.where(lane == 1, w1, 0.0))
    sel_ref[...] = jnp.where(sel0 | sel1, 1.0, 0.0).astype(BF16)


def _merge(x, h1, branches, mod4, n2g, wg, wa, wb, wc, wd, wo, router, l, j, t_ctx, dec_seq):
    T = x.shape[0]
    moe = router is not None
    tm = TM_MERGE
    nc = t_ctx // tm
    row = functools.partial(_mod_row, tm=tm, t_ctx=t_ctx, dec_seq=dec_seq)
    rowspec = lambda w: pl.BlockSpec((tm, w), lambda i: (i, 0))
    ctxspec = lambda w: pl.BlockSpec((tm, w), lambda i: (jnp.minimum(i, nc - 1), 0))
    latspec = lambda w: pl.BlockSpec((tm, w), lambda i: (jnp.maximum(i - nc, 0), 0))
    in_specs = [rowspec(D_MODEL), rowspec(D_MODEL)]
    args = [x, h1]
    for oc, ol in branches:
        in_specs += [ctxspec(oc.shape[1]), latspec(ol.shape[1])]
        args += [oc, ol]
    in_specs += [pl.BlockSpec((None, None, 1, 6 * D_MODEL), lambda i: (l, row(i), 0, 0))]
    in_specs += [_layer(a, l) for a in (n2g, wg, wa, wb, wc, wd, wo)]
    args += [mod4, n2g, wg, wa, wb, wc, wd, wo]
    out_shape = [jax.ShapeDtypeStruct((T, D_MODEL), F32)]
    out_specs = [rowspec(D_MODEL)]
    if moe:
        in_specs += [_layer(router[0], j), _layer(router[1], j)]
        args += list(router)
        out_shape += [jax.ShapeDtypeStruct((T * SUB, LANES), F32),
                      jax.ShapeDtypeStruct((T, N_EXP_PAD), I32),
                      jax.ShapeDtypeStruct((T, N_EXP_PAD), F32),
                      jax.ShapeDtypeStruct((T, N_EXP_PAD), BF16)]
        out_specs += [pl.BlockSpec((tm * SUB, LANES), lambda i: (i, 0)),
                      rowspec(N_EXP_PAD), rowspec(N_EXP_PAD), rowspec(N_EXP_PAD)]
    else:
        out_shape += [jax.ShapeDtypeStruct((T, D_MODEL), BF16)]
        out_specs += [rowspec(D_MODEL)]
    return pl.pallas_call(
        functools.partial(_merge_kernel, moe=moe, n_ctx_tiles=nc),
        grid=(T // tm,),
        in_specs=in_specs, out_specs=out_specs, out_shape=out_shape,
        compiler_params=_cparams(("parallel",)),
        name="merge_moe" if moe else "merge",
    )(*args)


def _swiglu_step(x, w1_ref, w3_ref, w2_ref, acc_ref, f):
    a = _dot(x, _bf(w1_ref[...]))
    b = _dot(x, _bf(w3_ref[...]))
    t = _dot(_bf(a * jax.nn.sigmoid(a) * b), _bf(w2_ref[...]))

    @pl.when(f == 0)
    def _():
        acc_ref[...] = t

    @pl.when(f > 0)
    def _():
        acc_ref[...] += t


def _pre_norm1(x, mod_ref, g_ref):
    return _bf(_rms(x, g_ref[...]) * (1.0 + mod_ref[:, D_MODEL:2 * D_MODEL]) + mod_ref[:, 0:D_MODEL])


def _ffn_dense_kernel(x_ref, w1_ref, w3_ref, w2_ref, x1_ref, mod_ref, modn_ref, gn_ref,
                      o_ref, hn_ref, acc_ref):
    f = pl.program_id(1)
    _swiglu_step(x_ref[...], w1_ref, w3_ref, w2_ref, acc_ref, f)

    @pl.when(f == pl.num_programs(1) - 1)
    def _():
        x2 = x1_ref[...] + mod_ref[:, 5 * D_MODEL:6 * D_MODEL] * acc_ref[...]
        o_ref[...] = x2
        hn_ref[...] = _pre_norm1(x2, modn_ref, gn_ref)


def _ffn_dense(h2, w1, w3, w2, x1, mod4, n1g, l, j, t_ctx, dec_seq):
    T = h2.shape[0]
    nf = w1.shape[2] // TF_DENSE
    row = functools.partial(_mod_row, tm=TG, t_ctx=t_ctx, dec_seq=dec_seq)
    rows = pl.BlockSpec((TG, D_MODEL), lambda i, f: (i, 0))
    return pl.pallas_call(
        _ffn_dense_kernel,
        grid=(T // TG, nf),
        in_specs=[rows,
                  pl.BlockSpec((None, D_MODEL, TF_DENSE), lambda i, f: (j, 0, f)),
                  pl.BlockSpec((None, D_MODEL, TF_DENSE), lambda i, f: (j, 0, f)),
                  pl.BlockSpec((None, TF_DENSE, D_MODEL), lambda i, f: (j, f, 0)),
                  rows,
                  pl.BlockSpec((None, None, 1, 6 * D_MODEL), lambda i, f: (l, row(i), 0, 0)),
                  pl.BlockSpec((None, None, 1, 6 * D_MODEL), lambda i, f: (l + 1, row(i), 0, 0)),
                  _layer(n1g, l + 1)],
        out_specs=[rows, rows],
        out_shape=[jax.ShapeDtypeStruct((T, D_MODEL), F32), jax.ShapeDtypeStruct((T, D_MODEL), BF16)],
        scratch_shapes=[pltpu.VMEM((TG, D_MODEL), F32)],
        compiler_params=_cparams(("parallel", "arbitrary")),
        name="ffn_dense",
    )(h2, w1, w3, w2, x1, mod4, mod4, n1g)


def _ffn_experts_kernel(te_ref, na_ref, x_ref, w1_ref, w3_ref, w2_ref, o_ref, xb_ref, acc_ref):
    i = pl.program_id(0)
    f = pl.program_id(1)
    active = i < na_ref[0]

    @pl.when(jnp.logical_and(active, f == 0))
    def _():
        for j in range(SUB):
            xb_ref[:, LANES * j:LANES * (j + 1)] = _bf(x_ref[pl.ds(j, TG, stride=SUB), :])

    @pl.when(active)
    def _():
        _swiglu_step(xb_ref[...], w1_ref, w3_ref, w2_ref, acc_ref, f)

    @pl.when(jnp.logical_and(active, f == pl.num_programs(1) - 1))
    def _():
        for j in range(SUB):
            o_ref[pl.ds(j, TG, stride=SUB), :] = acc_ref[:, LANES * j:LANES * (j + 1)]

    @pl.when(jnp.logical_and(jnp.logical_not(active), f == 0))
    def _():
        o_ref[...] = jnp.zeros(o_ref.shape, F32)


def _ffn_experts(xs, w1, w3, w2, tile_e, n_active, j):
    rows = xs.shape[0] // SUB
    nf = w1.shape[3] // TF_MOE

    def fidx(i, f, na):
        return jnp.where(i < na[0], f, nf - 1)

    def xidx(i, na):
        return jnp.minimum(i, na[0] - 1)

    return pl.pallas_call(
        _ffn_experts_kernel,
        grid_spec=pltpu.PrefetchScalarGridSpec(
            num_scalar_prefetch=2, grid=(rows // TG, nf),
            in_specs=[pl.BlockSpec((TG * SUB, LANES), lambda i, f, te, na: (xidx(i, na), 0)),
                      pl.BlockSpec((None, None, D_MODEL, TF_MOE),
                                   lambda i, f, te, na: (j, te[i], 0, fidx(i, f, na))),
                      pl.BlockSpec((None, None, D_MODEL, TF_MOE),
                                   lambda i, f, te, na: (j, te[i], 0, fidx(i, f, na))),
                      pl.BlockSpec((None, None, TF_MOE, D_MODEL),
                                   lambda i, f, te, na: (j, te[i], fidx(i, f, na), 0))],
            out_specs=pl.BlockSpec((TG * SUB, LANES), lambda i, f, te, na: (i, 0)),
            scratch_shapes=[pltpu.VMEM((TG, D_MODEL), BF16), pltpu.VMEM((TG, D_MODEL), F32)]),
        out_shape=jax.ShapeDtypeStruct((rows * SUB, LANES), F32),
        compiler_params=_cparams(("parallel", "arbitrary")),
        name="ffn_experts",
    )(tile_e, n_active, xs, w1, w3, w2)


def _rank_kernel(sel_ref, tri_ref, rank_ref, cnt_ref, carry):
    i = pl.program_id(0)

    @pl.when(i == 0)
    def _():
        carry[...] = jnp.zeros(carry.shape, F32)

    s = sel_ref[...]
    rank_ref[...] = _dot(tri_ref[...], s) + carry[...]
    carry[...] += jnp.sum(s.astype(F32), axis=0, keepdims=True)
    cnt_ref[...] = jnp.broadcast_to(carry[...], cnt_ref.shape)


def _rank(sel, tri):
    T = sel.shape[0]
    tr = tri.shape[0]
    return pl.pallas_call(
        _rank_kernel,
        grid=(T // tr,),
        in_specs=[pl.BlockSpec((tr, N_EXP_PAD), lambda i: (i, 0)),
                  pl.BlockSpec((tr, tr), lambda i: (0, 0))],
        out_specs=[pl.BlockSpec((tr, N_EXP_PAD), lambda i: (i, 0)),
                   pl.BlockSpec((8, N_EXP_PAD), lambda i: (0, 0))],
        out_shape=[jax.ShapeDtypeStruct((T, N_EXP_PAD), F32),
                   jax.ShapeDtypeStruct((8, N_EXP_PAD), F32)],
        scratch_shapes=[pltpu.VMEM((1, N_EXP_PAD), F32)],
        compiler_params=_cparams(("arbitrary",)),
        name="route_rank",
    )(sel, tri)


TILE_ROWS = TR * SUB


def _token_tile(ref, tok):
    return ref.at[pl.ds(pl.multiple_of(tok * SUB, SUB), SUB)]


def _dispatch_kernel(pos0_ref, pos1_ref, ends_ref, padded_ref, h_ref, xs_hbm, stage, zbuf, sem, zsem):
    i = pl.program_id(0)

    @pl.when(i == 0)
    def _():
        zbuf[...] = jnp.zeros(zbuf.shape, F32)

        def fill_tile(first_slot):
            start = pl.multiple_of(first_slot * SUB, TG * SUB)
            return pltpu.make_async_copy(zbuf, xs_hbm.at[pl.ds(start, TG * SUB)], zsem.at[0])

        n_slots = xs_hbm.shape[0] // SUB
        for wait in (False, True):
            for e in range(N_EXP):
                for cond, first in ((padded_ref[e] > 0, ends_ref[e] - TG),
                                    (ends_ref[N_EXP - 1] + e * TG < n_slots, ends_ref[N_EXP - 1] + e * TG)):
                    @pl.when(cond)
                    def _():
                        fill = fill_tile(first)
                        if wait:
                            fill.wait()
                        else:
                            fill.start()

    slot = i % 2
    base = pl.multiple_of(slot * TILE_ROWS, TILE_ROWS)
    stage[pl.ds(base, TILE_ROWS), :] = h_ref[...]

    def issue(r, c):
        t = i * TR + r
        src = _token_tile(stage, slot * TR + r)
        pltpu.make_async_copy(src, _token_tile(xs_hbm, pos0_ref[t]), sem.at[slot]).start()
        pltpu.make_async_copy(src, _token_tile(xs_hbm, pos1_ref[t]), sem.at[slot]).start()
        return c
    lax.fori_loop(0, TR, issue, 0, unroll=8)

    def wait_slot(s):
        b = pl.multiple_of(s * TILE_ROWS, TILE_ROWS)
        for _ in range(2):
            pltpu.make_async_copy(stage.at[pl.ds(b, TILE_ROWS)], xs_hbm.at[pl.ds(0, TILE_ROWS)],
                                  sem.at[s]).wait()

    @pl.when(i > 0)
    def _():
        wait_slot(1 - slot)

    @pl.when(i == pl.num_programs(0) - 1)
    def _():
        wait_slot(slot)


def _dispatch(pos0, pos1, ends, padded, h2t, n_rows):
    T = h2t.shape[0] // SUB
    assert n_rows - 2 * T <= N_EXP * TG
    return pl.pallas_call(
        _dispatch_kernel,
        grid_spec=pltpu.PrefetchScalarGridSpec(
            num_scalar_prefetch=4, grid=(T // TR,),
            in_specs=[pl.BlockSpec((TILE_ROWS, LANES), lambda i, *_: (i, 0))],
            out_specs=pl.BlockSpec(memory_space=pl.ANY),
            scratch_shapes=[pltpu.VMEM((2 * TILE_ROWS, LANES), F32),
                            pltpu.VMEM((TG * SUB, LANES), F32),
                            pltpu.SemaphoreType.DMA((2,)),
                            pltpu.SemaphoreType.DMA((1,))]),
        out_shape=jax.ShapeDtypeStruct((n_rows * SUB, LANES), F32),
        compiler_params=_cparams(("arbitrary",)),
        name="moe_dispatch",
    )(pos0, pos1, ends, padded, h2t)


def _combine_kernel(pos0_ref, pos1_ref, y_hbm, x1_ref, rw_ref, mod_ref, *rest, emit_next, n_ctx_tiles):
    if emit_next:
        modn_ref, gn_ref, o_ref, hn_ref, buf0, buf1, sem0, sem1 = rest
    else:
        oc_ref, ol_ref, buf0, buf1, sem0, sem1 = rest
    i = pl.program_id(0)
    nt = pl.num_programs(0)

    def issue(tile, slot):
        def body(r, c):
            t = tile * TR + r
            pltpu.make_async_copy(_token_tile(y_hbm, pos0_ref[t]), _token_tile(buf0, slot * TR + r),
                                  sem0.at[slot]).start()
            pltpu.make_async_copy(_token_tile(y_hbm, pos1_ref[t]), _token_tile(buf1, slot * TR + r),
                                  sem1.at[slot]).start()
            return c
        lax.fori_loop(0, TR, body, 0, unroll=8)

    @pl.when(i == 0)
    def _():
        issue(0, 0)

    @pl.when(i + 1 < nt)
    def _():
        issue(i + 1, (i + 1) % 2)

    slot = i % 2
    base = pl.multiple_of(slot * TILE_ROWS, TILE_ROWS)
    for buf, sem in ((buf0, sem0), (buf1, sem1)):
        pltpu.make_async_copy(y_hbm.at[pl.ds(0, TILE_ROWS)], buf.at[pl.ds(base, TILE_ROWS)],
                              sem.at[slot]).wait()
    rw = rw_ref[...]
    w0 = rw[:, 0:1]
    w1 = rw[:, 1:2]

    def write(o_ref):
        for j in range(SUB):
            sl = slice(LANES * j, LANES * (j + 1))
            f = (w0 * buf0[pl.ds(base + j, TR, stride=SUB), :]
                 + w1 * buf1[pl.ds(base + j, TR, stride=SUB), :])
            o_ref[:, sl] = x1_ref[:, sl] + mod_ref[:, 5 * D_MODEL + LANES * j:5 * D_MODEL + LANES * (j + 1)] * f

    if emit_next:
        write(o_ref)
        hn_ref[...] = _pre_norm1(o_ref[...], modn_ref, gn_ref)
    else:
        @pl.when(i < n_ctx_tiles)
        def _():
            write(oc_ref)

        @pl.when(i >= n_ctx_tiles)
        def _():
            write(ol_ref)


def _combine(pos0, pos1, ys, x1, rw, mod4, n1g, l, t_ctx, dec_seq, emit_next):
    T = x1.shape[0]
    nc = t_ctx // TR
    row = functools.partial(_mod_row, tm=TR, t_ctx=t_ctx, dec_seq=dec_seq)
    rows = pl.BlockSpec((TR, D_MODEL), lambda i, p0, p1: (i, 0))
    in_specs = [pl.BlockSpec(memory_space=pl.ANY), rows,
                pl.BlockSpec((TR, N_EXP_PAD), lambda i, p0, p1: (i, 0)),
                pl.BlockSpec((None, None, 1, 6 * D_MODEL), lambda i, p0, p1: (l, row(i), 0, 0))]
    args = [pos0, pos1, ys, x1, rw, mod4]
    if emit_next:
        in_specs += [pl.BlockSpec((None, None, 1, 6 * D_MODEL), lambda i, p0, p1: (l + 1, row(i), 0, 0)),
                     _layer(n1g, l + 1)]
        args += [mod4, n1g]
        out_specs = [rows, rows]
        out_shape = [jax.ShapeDtypeStruct((T, D_MODEL), F32), jax.ShapeDtypeStruct((T, D_MODEL), BF16)]
    else:
        out_specs = [pl.BlockSpec((TR, D_MODEL), lambda i, p0, p1: (jnp.minimum(i, nc - 1), 0)),
                     pl.BlockSpec((TR, D_MODEL), lambda i, p0, p1: (jnp.maximum(i - nc, 0), 0))]
        out_shape = [jax.ShapeDtypeStruct((t_ctx, D_MODEL), F32),
                     jax.ShapeDtypeStruct((T - t_ctx, D_MODEL), F32)]
    return pl.pallas_call(
        functools.partial(_combine_kernel, emit_next=emit_next, n_ctx_tiles=nc),
        grid_spec=pltpu.PrefetchScalarGridSpec(
            num_scalar_prefetch=2, grid=(T // TR,),
            in_specs=in_specs, out_specs=out_specs,
            scratch_shapes=[pltpu.VMEM((2 * TILE_ROWS, LANES), F32), pltpu.VMEM((2 * TILE_ROWS, LANES), F32),
                            pltpu.SemaphoreType.DMA((2,)), pltpu.SemaphoreType.DMA((2,))]),
        out_shape=out_shape,
        compiler_params=_cparams(("arbitrary",)),
        name="moe_combine",
    )(*args)


def _block_ones(n, seg):
    idx = np.arange(n) // seg
    return jnp.asarray((idx[:, None] == idx[None, :]).astype(np.float32), dtype=BF16)


def _rope_tables(seq, head_w, rot_dim, n_heads):
    rows = seq // GRID_W
    nf = rot_dim // 4
    freqs = ROPE_THETA ** (-np.arange(nf, dtype=np.float64) / nf)
    row = np.repeat(np.arange(rows, dtype=np.float64), GRID_W)
    col = np.tile(np.arange(GRID_W, dtype=np.float64), rows)
    ang = np.concatenate([row[:, None] * freqs, col[:, None] * freqs], axis=-1)
    ang = np.repeat(ang, 2, axis=-1)
    even = (np.arange(rot_dim) % 2 == 0)[None, :]
    c = np.ones((seq, head_w))
    se = np.zeros((seq, head_w))
    so = np.zeros((seq, head_w))
    c[:, :rot_dim] = np.cos(ang)
    se[:, :rot_dim] = np.where(even, -np.sin(ang), 0.0)
    so[:, :rot_dim] = np.where(even, 0.0, np.sin(ang))
    return tuple(jnp.asarray(np.tile(t, (1, n_heads)), dtype=F32) for t in (c, se, so))


def _pool_mask():
    m = np.zeros((2 * D_PAD, D_CH), np.float32)
    for gi, w in enumerate(POOL):
        left = w // 2
        right = w - 1 - left
        for off in range(-left, right + 1):
            m[off + D_PAD, gi * D_G:(gi + 1) * D_G] = 1.0
    return jnp.asarray(m)


def _pool_counts(seq):
    t = np.arange(seq)
    m = np.zeros((seq, D_CH), np.float32)
    for gi, w in enumerate(POOL):
        left = w // 2
        right = w - 1 - left
        m[:, gi * D_G:(gi + 1) * D_G] = (np.minimum(t + right + 1, seq) - np.maximum(t - left, 0))[:, None]
    return jnp.asarray(m)


def _b_heads(nope, rope):
    ref = nope if nope is not None else rope
    lead = ref.shape[:-1]
    z = lambda w: jnp.zeros(lead + (w,), ref.dtype)
    parts = [rope if rope is not None else z(B_ROPE), nope if nope is not None else z(B_NOPE),
             z(B_HEAD_PAD - B_QK)]
    out = jnp.concatenate(parts, axis=-1)
    return out.reshape(lead[:-1] + (lead[-1] * B_HEAD_PAD,))


def kernel(x_prompt, x_sample, cache_a_k, cache_a_v, cache_b_ckv, cache_b_kpe, c, c_ctx, ada_w, ada_b, norm1_g, norm2_g, w_in, a_q_norm, a_k_norm, a_lambda, a_sub_norm, a_w_o, b_q_lora_norm, b_kv_lora_norm, b_w_uq, b_w_ukv, b_q_norm, b_k_norm, b_w_o, c_dw, c_dw_b, c_ln_g, c_ln_b, c_w_o, d_w_group, d_scale, d_w_o, w_out, ffn_w1, ffn_w3, ffn_w2, moe_router, moe_w1, moe_w3, moe_w2):
    nb, seq, _ = x_prompt.shape
    db, dseq, _ = x_sample.shape
    L = w_in.shape[0]
    past = cache_a_k.shape[2]
    t_ctx = nb * seq
    t_lat = db * dseq
    T = t_ctx + t_lat
    na = 2 * A_HEADS * A_QK
    assert t_ctx % TG == 0 and dseq % TG == 0 and seq % TR == 0 and db + 1 <= 16

    x = jnp.concatenate([x_prompt.reshape(t_ctx, D_MODEL), x_sample.reshape(t_lat, D_MODEL)], axis=0)
    cv = jnp.concatenate([c_ctx[None, :], c, jnp.zeros((15 - db, D_MODEL), F32)], axis=0)
    mod4 = _ada_all(cv, ada_w, ada_b).reshape(L, 16, 1, 6 * D_MODEL)

    bd_a = _block_ones(na, A_QK)
    bd_b = _block_ones(B_HEADS * B_HEAD_PAD, B_HEAD_PAD)
    rope_a = _rope_tables(dseq, A_QK, A_QK, 2 * A_HEADS)
    rope_b = _rope_tables(dseq, B_HEAD_PAD, B_ROPE, B_HEADS)
    pmask = _pool_mask()
    tri = jnp.asarray(np.tril(np.ones((TG, TG), np.float32), -1), dtype=BF16)

    row1 = lambda a: a.reshape(L, 1, -1)
    w_a, w_b, w_cd, w_g = _repack(jnp.transpose(w_in, (0, 2, 1)))
    gq_a = row1(jnp.tile(a_q_norm, (1, 2 * A_HEADS)))
    gk_a = row1(jnp.tile(a_k_norm, (1, 2 * A_HEADS)))
    uq = b_w_uq.reshape(L, B_QL, B_HEADS, B_QK)
    wuq = _b_heads(uq[..., :B_NOPE], uq[..., B_NOPE:]).astype(BF16)
    ukv = b_w_ukv.reshape(L, B_KVL, B_HEADS, B_NOPE + B_V)
    wuk = _b_heads(ukv[..., :B_NOPE], None).astype(BF16)
    wuv = ukv[..., B_NOPE:].reshape(L, B_KVL, B_HEADS * B_V).astype(BF16)
    gq_b = row1(jnp.tile(_b_heads(b_q_norm[:, None, :B_NOPE], b_q_norm[:, None, B_NOPE:]), (1, B_HEADS)))
    gk_b = row1(jnp.tile(_b_heads(b_k_norm[:, None, :B_NOPE], b_k_norm[:, None, B_NOPE:]), (1, B_HEADS)))
    bdd = jnp.zeros((L, D_CH, D_CH), F32)
    for gi in range(len(POOL)):
        bdd = bdd.at[:, gi * D_G:(gi + 1) * D_G, gi * D_G:(gi + 1) * D_G].set(d_w_group[:, gi])
    bdd = bdd.astype(BF16)
    a_wo, b_wo, c_wo, d_wo, wo = (w.astype(BF16) for w in (a_w_o, b_w_o, c_w_o, d_w_o, w_out))
    ffn1, ffn3, ffn2 = (w.astype(BF16) for w in (ffn_w1, ffn_w3, ffn_w2))
    r_pad = jnp.pad(moe_router, ((0, 0), (0, 0), (0, N_EXP_PAD - N_EXP)))
    r_hi = r_pad.astype(BF16)
    r_lo = (r_pad - r_hi.astype(F32)).astype(BF16)
    n1g, n2g = row1(norm1_g), row1(norm2_g)
    g_sub = row1(a_sub_norm)
    gql, gkvl = row1(b_q_lora_norm), row1(b_kv_lora_norm)
    dwb, lng, lnb, dsc = row1(c_dw_b), row1(c_ln_g), row1(c_ln_b), row1(d_scale)

    ck_a = cache_a_k.reshape(db, L, past, na).astype(BF16)
    cv_a = cache_a_v.reshape(db, L, past, A_HEADS * A_V).astype(BF16)
    kpe_pad = jnp.pad(cache_b_kpe, ((0, 0), (0, 0), (0, 0), (0, LANES - B_ROPE)))
    ck_b, cv_b = _cache_b(cache_b_ckv, kpe_pad, wuk, wuv, gk_b, bd_b)

    n_rows = 2 * T + N_EXP * TG
    nt_g = n_rows // TG

    new_ak = jnp.zeros((nb, L, seq, na), F32)
    new_av = jnp.zeros((nb, L, seq, A_HEADS * A_V), F32)
    new_ckv = jnp.zeros((nb, L, seq, B_KVL), F32)
    new_kpe = jnp.zeros((nb, L, seq, B_ROPE), F32)
    h1 = _prep(x, mod4, n1g, 0, t_ctx, dseq)
    for l in range(L):
        last = l == L - 1

        q1c, q2c, kc, vc, new_ak, new_av = _proj_a(h1, w_a, gq_a, gk_a, bd_a, None, l, 0, t_ctx, False,
                                                   cache_out=(new_ak, new_av))
        q1l, q2l, kl, vl = _proj_a(h1, w_a, gq_a, gk_a, bd_a, rope_a, l, t_ctx, t_lat, True)
        oa = (_attn_a(q1c, q2c, kc, vc, None, None, a_lambda, g_sub, l, seq, False),
              _attn_a(q1l, q2l, kl, vl, ck_a, cv_a, a_lambda, g_sub, l, dseq, True))

        bargs = (w_b, gql, gkvl, wuq, wuk, wuv, gq_b, gk_b, bd_b)
        qc, kc, vc, new_ckv, new_kpe = _proj_b(h1, *bargs, None, l, 0, t_ctx, False,
                                               cache_out=(new_ckv, new_kpe))
        ql, kl, vl = _proj_b(h1, *bargs, rope_b, l, t_ctx, t_lat, True)
        ob = (_attn_b(qc, kc, vc, None, None, l, seq, False),
              _attn_b(ql, kl, vl, ck_b, cv_b, l, dseq, True))

        cdargs = (w_cd, c_dw, dwb, lng, lnb, bdd, dsc, pmask)
        occ, odc = _mix_cd(h1, *cdargs, l, 0, t_ctx, seq)
        ocl, odl = _mix_cd(h1, *cdargs, l, t_ctx, t_lat, dseq)

        j = l // 2
        moe = l % 2 == 1
        router = (r_hi, r_lo) if moe else None
        outs = _merge(x, h1, (oa, ob, (occ, ocl), (odc, odl)), mod4, n2g, w_g, a_wo, b_wo,
                      c_wo, d_wo, wo, router, l, j, t_ctx, dseq)
        if not moe:
            x1, h2 = outs
            assert not last, "a dense layer is always followed by another layer"
            x, h1 = _ffn_dense(h2, ffn1, ffn3, ffn2, x1, mod4, n1g, l, j, t_ctx, dseq)
        else:
            x1, h2t, ridx, rw, sel = outs
            rank, cnt = _rank(sel, tri)
            counts = cnt[0, :N_EXP].astype(I32)
            padded = ((counts + TG - 1) // TG) * TG
            ends = jnp.cumsum(padded)
            offs = ends - padded
            pos_all = offs[None, :] + rank[:, :N_EXP].astype(I32)
            pos0 = jnp.take_along_axis(pos_all, ridx[:, 0:1], axis=1)[:, 0]
            pos1 = jnp.take_along_axis(pos_all, ridx[:, 1:2], axis=1)[:, 0]
            tile_start = jnp.arange(nt_g, dtype=I32) * TG
            tile_e = jnp.minimum(jnp.sum(tile_start[:, None] >= ends[None, :], axis=1), N_EXP - 1).astype(I32)
            n_active = (ends[-1] // TG).astype(I32).reshape(1)
            last_e = tile_e[jnp.maximum(n_active[0] - 1, 0)]
            tile_e = jnp.where(tile_start < ends[-1], tile_e, last_e)
            xs = _dispatch(pos0, pos1, ends.astype(I32), padded, h2t, n_rows)
            ys = _ffn_experts(xs, moe_w1, moe_w3, moe_w2, tile_e, n_active, j)
            outs = _combine(pos0, pos1, ys, x1, rw, mod4, n1g, l, t_ctx, dseq, not last)
            if last:
                y_ctx, y_lat = outs
            else:
                x, h1 = outs

    assert L % 2 == 0, "the last layer is a routed layer, whose combine step emits the two outputs"
    y_prompt = y_ctx.reshape(nb, seq, D_MODEL)
    y_sample = y_lat.reshape(db, dseq, D_MODEL)
    new_a_k = new_ak.reshape(nb, L, seq, 2 * A_HEADS, A_QK)
    new_a_v = new_av.reshape(nb, L, seq, A_HEADS, A_V)
    return (y_prompt, y_sample, new_a_k, new_a_v, new_ckv, new_kpe)
```

```python
import functools
import math

import numpy as np
import jax
import jax.numpy as jnp
from jax import lax
from jax.experimental import pallas as pl
from jax.experimental.pallas import tpu as pltpu

F32 = jnp.float32
BF16 = jnp.bfloat16
I32 = jnp.int32

EPS = 1e-6
D_MODEL = 1024
GRID_W = 64
ROPE_THETA = 10000.0
A_HEADS = 4
A_QK = 64
A_V = 128
B_HEADS = 4
B_NOPE = 64
B_ROPE = 32
B_QK = B_NOPE + B_ROPE
B_V = 64
B_QL = 256
B_KVL = 128
B_HEAD_PAD = 128
C_CH = 256
C_W = 31
C_PAD = 16
POOL = (2, 4, 8, 16)
D_G = 64
D_CH = D_G * len(POOL)
D_PAD = 8
N_EXP = 8
N_EXP_PAD = 128
LANES = 128
SUB = 8

O_AQ, O_AK, O_AV, O_BQ, O_BKV, O_KPE, O_C, O_D, O_G, O_END = (
    0, 512, 1024, 1536, 1792, 1920, 1952, 2464, 2720, 6816)
W_B_COLS = (O_KPE - O_BQ) + LANES

TM = 1024
TM_MERGE = 512
TQ = 512
TG = 1024
TR = 256
TF_DENSE = 1408
TF_MOE = 512
VMEM_LIMIT = 56 * 1024 * 1024


def _cparams(sem):
    return pltpu.CompilerParams(dimension_semantics=sem, vmem_limit_bytes=VMEM_LIMIT)


def _dot(a, b):
    return jnp.dot(a, b, preferred_element_type=F32)


def _dot_nt(a, b):
    return lax.dot_general(a, b, (((1,), (1,)), ((), ())), preferred_element_type=F32)


def _bf(x):
    return x.astype(BF16)


def _rms(x, g):
    return x * lax.rsqrt(jnp.mean(x * x, axis=-1, keepdims=True) + EPS) * g


def _seg_sum_sq(x, bd_ref):
    return _dot(_bf(x * x), bd_ref[...])


def _rope(x, c_ref, se_ref, so_ref):
    n = x.shape[-1]
    return (x * c_ref[...] + pltpu.roll(x, n - 1, 1) * se_ref[...]
            + pltpu.roll(x, 1, 1) * so_ref[...])


def _mod_row(i, tm, t_ctx, dec_seq):
    r = i * tm
    return jnp.where(r < t_ctx, 0, 1 + (r - t_ctx) // dec_seq)


def _layer(a, l):
    nd = a.ndim
    return pl.BlockSpec((None,) + a.shape[1:], lambda *_: (l,) + (0,) * (nd - 1),
                        pipeline_mode=pl.Buffered(1))


def _whole(a):
    nd = a.ndim
    return pl.BlockSpec(a.shape, lambda *_: (0,) * nd, pipeline_mode=pl.Buffered(1))


def _ada_kernel(cv_ref, w_ref, b_ref, o_ref):
    cv = cv_ref[...]
    s = cv * jax.nn.sigmoid(cv)
    o_ref[...] = _dot(_bf(s), _bf(w_ref[...])) + b_ref[...]


def _ada_all(cv, ada_w, ada_b):
    L, d, n = ada_w.shape
    tn = 1536
    return pl.pallas_call(
        _ada_kernel,
        grid=(L, n // tn),
        in_specs=[pl.BlockSpec((16, d), lambda l, j: (0, 0)),
                  pl.BlockSpec((None, d, tn), lambda l, j: (l, 0, j)),
                  pl.BlockSpec((None, 1, tn), lambda l, j: (l, 0, j))],
        out_specs=pl.BlockSpec((None, 16, tn), lambda l, j: (l, 0, j)),
        out_shape=jax.ShapeDtypeStruct((L, 16, n), F32),
        compiler_params=_cparams(("parallel", "parallel")),
        name="ada_mod",
    )(cv, ada_w, ada_b.reshape(L, 1, n))


REPACK_ROWS = 1024
REPACK_GROUPS = ((O_AQ, O_BQ - O_AQ, 0, 0), (O_BQ, O_C - O_BQ, 1, 0),
                 (O_C, O_G - O_C, 2, 0), (O_G, O_END - O_G, 3, 0))


def _repack_kernel(w_ref, wa_ref, wb_ref, wcd_ref, wg_ref):
    outs = (wa_ref, wb_ref, wcd_ref, wg_ref)
    r = pl.program_id(1)
    for step in range(pl.cdiv(O_END, REPACK_ROWS)):
        lo, hi = step * REPACK_ROWS, (step + 1) * REPACK_ROWS

        @pl.when(r == step)
        def _():
            for first, rows, dst, dst_first in REPACK_GROUPS:
                a, b = max(first, lo), min(first + rows, hi)
                if a < b:
                    outs[dst][dst_first + a - first:dst_first + b - first, :] = _bf(w_ref[a - lo:b - lo, :])

    @pl.when(r == 0)
    def _():
        wb_ref[O_C - O_BQ:W_B_COLS, :] = jnp.zeros((W_B_COLS - (O_C - O_BQ), D_MODEL), BF16)


def _repack(w_t):
    L, n, d = w_t.shape
    heights = (O_BQ - O_AQ, W_B_COLS, O_G - O_C, O_END - O_G)
    return pl.pallas_call(
        _repack_kernel,
        grid=(L, pl.cdiv(n, REPACK_ROWS)),
        in_specs=[pl.BlockSpec((None, REPACK_ROWS, d), lambda l, r: (l, r, 0))],
        out_specs=[pl.BlockSpec((None, h, d), lambda l, r: (l, 0, 0)) for h in heights],
        out_shape=[jax.ShapeDtypeStruct((L, h, d), BF16) for h in heights],
        compiler_params=_cparams(("parallel", "arbitrary")),
        name="repack_w_in",
    )(w_t)


def _ctx_lat_specs(tm, width, n_ctx_tiles):
    return [pl.BlockSpec((tm, width), lambda i: (jnp.minimum(i, n_ctx_tiles - 1), 0)),
            pl.BlockSpec((tm, width), lambda i: (jnp.maximum(i - n_ctx_tiles, 0), 0))]


def _prep_kernel(xc_ref, xl_ref, mod_ref, g_ref, h_ref, *, n_ctx_tiles):
    x = jnp.where(pl.program_id(0) < n_ctx_tiles, xc_ref[...], xl_ref[...])
    h_ref[...] = _pre_norm1(x, mod_ref, g_ref)


def _prep(x_ctx, x_lat, mod4, g, l, dec_seq):
    t_ctx = x_ctx.shape[0]
    T = t_ctx + x_lat.shape[0]
    nc = t_ctx // TM
    row = functools.partial(_mod_row, tm=TM, t_ctx=t_ctx, dec_seq=dec_seq)
    return pl.pallas_call(
        functools.partial(_prep_kernel, n_ctx_tiles=nc),
        grid=(T // TM,),
        in_specs=_ctx_lat_specs(TM, D_MODEL, nc) + [
            pl.BlockSpec((None, None, 1, 6 * D_MODEL), lambda i: (l, row(i), 0, 0)),
            _layer(g, l)],
        out_specs=pl.BlockSpec((TM, D_MODEL), lambda i: (i, 0)),
        out_shape=jax.ShapeDtypeStruct((T, D_MODEL), BF16),
        compiler_params=_cparams(("parallel",)),
        name="prep",
    )(x_ctx, x_lat, mod4, g)


def _proj_a_kernel(*refs, latent):
    if latent:
        (h_ref, w_ref, gq_ref, gk_ref, bd_ref, c_ref, se_ref, so_ref,
         q1_ref, q2_ref, k_ref, v_ref) = refs
    else:
        (h_ref, w_ref, gq_ref, gk_ref, bd_ref, _, _,
         q1_ref, q2_ref, k_ref, v_ref, nk_ref, nv_ref) = refs
    p = _dot_nt(h_ref[...], w_ref[...])
    n = 2 * A_HEADS * A_QK
    q = p[:, 0:n]
    k = p[:, n:2 * n]
    v = p[:, 2 * n:3 * n]
    q = q * lax.rsqrt(_seg_sum_sq(q, bd_ref) * (1.0 / A_QK) + EPS) * gq_ref[...]
    k = k * lax.rsqrt(_seg_sum_sq(k, bd_ref) * (1.0 / A_QK) + EPS) * gk_ref[...]
    if latent:
        q = _rope(q, c_ref, se_ref, so_ref)
        k = _rope(k, c_ref, se_ref, so_ref)
    else:
        nk_ref[...] = k.reshape(nk_ref.shape)
        nv_ref[...] = v.reshape(nv_ref.shape)
    q = q * (A_QK ** -0.5 * LOG2E)
    lane = lax.broadcasted_iota(I32, q.shape, 1)
    first = (lane % (2 * A_QK)) < A_QK
    q1_ref[...] = _bf(jnp.where(first, q, 0.0))
    q2_ref[...] = _bf(jnp.where(first, 0.0, q))
    k_ref[...] = _bf(k)
    v_ref[...] = _bf(v)


def _cache_out_spec(buf, l):
    nb, _, seq, w = buf.shape
    return pl.BlockSpec((TM // seq, None, seq, w), lambda i: (i, l, 0, 0))


def _proj_a(h1, w_a, gq, gk, bd, rope_tabs, l, row0, rows, latent, cache_out=None):
    n = 2 * A_HEADS * A_QK
    b0 = row0 // TM
    tab_blocks = rope_tabs[0].shape[0] // TM if latent else 1
    in_specs = [pl.BlockSpec((TM, D_MODEL), lambda i: (i + b0, 0)),
                _layer(w_a, l), _layer(gq, l), _layer(gk, l), _whole(bd)]
    args = [h1, w_a, gq, gk, bd]
    out_spec = pl.BlockSpec((TM, n), lambda i: (i, 0))
    out_shape = [jax.ShapeDtypeStruct((rows, n), BF16)] * 4
    out_specs = [out_spec] * 4
    aliases = {}
    if latent:
        in_specs += [pl.BlockSpec((TM, n), lambda i: (i % tab_blocks, 0))] * 3
        args += list(rope_tabs)
    else:
        for buf in cache_out:
            aliases[len(args)] = len(out_shape)
            in_specs.append(pl.BlockSpec(memory_space=pl.ANY))
            args.append(buf)
            out_shape.append(jax.ShapeDtypeStruct(buf.shape, buf.dtype))
            out_specs.append(_cache_out_spec(buf, l))
    return pl.pallas_call(
        functools.partial(_proj_a_kernel, latent=latent),
        grid=(rows // TM,),
        in_specs=in_specs, out_specs=out_specs, out_shape=out_shape,
        input_output_aliases=aliases,
        compiler_params=_cparams(("parallel",)),
        name="proj_a_lat" if latent else "proj_a_ctx",
    )(*args)


LOG2E = math.log2(math.e)


def _attend(qs, kss, vs):
    es, rs = [], []
    for q, ks in zip(qs, kss):
        scores = [_dot_nt(q, k) for k in ks]
        m = functools.reduce(jnp.maximum, [jnp.max(s, axis=-1, keepdims=True) for s in scores])
        e = [jnp.exp2(s - m) for s in scores]
        l = functools.reduce(lambda a, b: a + b, [jnp.sum(x, axis=-1, keepdims=True) for x in e])
        es.append([_bf(x) for x in e])
        rs.append(1.0 / l)
    o = None
    for p, v in enumerate(vs):
        stacked = es[0][p] if len(qs) == 1 else jnp.concatenate([e[p] for e in es], axis=0)
        t = _dot(stacked, v)
        o = t if o is None else o + t
    outs, r0 = [], 0
    for q, r in zip(qs, rs):
        outs.append(o[r0:r0 + q.shape[0]] * r)
        r0 += q.shape[0]
    return outs


def _attn_a_kernel(*refs, latent, lam_init):
    if latent:
        q1_ref, q2_ref, k_ref, v_ref, kc_ref, vc_ref, lam_ref, g_ref, o_ref = refs
    else:
        q1_ref, q2_ref, k_ref, v_ref, lam_ref, g_ref, o_ref = refs
    lm = lam_ref[...]
    lam = (jnp.exp(jnp.sum(lm[0:1] * lm[1:2], axis=-1, keepdims=True))
           - jnp.exp(jnp.sum(lm[2:3] * lm[3:4], axis=-1, keepdims=True)) + lam_init)
    for h in range(A_HEADS):
        sl = slice(A_V * h, A_V * (h + 1))
        ks = [k_ref[:, sl]]
        vs = [v_ref[:, sl]]
        if latent:
            ks.append(_bf(kc_ref[:, sl]))
            vs.append(_bf(vc_ref[:, sl]))
        tq = q1_ref.shape[0]
        oc, = _attend([jnp.concatenate([q1_ref[:, sl], q2_ref[:, sl]], axis=0)], [ks], vs)
        o = oc[0:tq] - lam * oc[tq:2 * tq]
        o = _rms(o, g_ref[...]) * (1.0 - lam_init)
        o_ref[:, sl] = _bf(o)


def _attn_a(q1, q2, k, v, cache_k, cache_v, a_lambda, g_sub, layer, seq, latent):
    rows, n = q1.shape
    nb = rows // seq
    lam_init = 0.8 - 0.6 * math.exp(-0.3 * layer)
    kern = functools.partial(_attn_a_kernel, latent=latent, lam_init=lam_init)
    if latent:
        nq = seq // TQ
        past = cache_k.shape[2]
        grid = (nb, nq)
        qs = pl.BlockSpec((TQ, n), lambda b, j: (b * nq + j, 0))
        kv = pl.BlockSpec((seq, n), lambda b, j: (b, 0))
        cs = pl.BlockSpec((None, None, past, n), lambda b, j: (b, layer, 0, 0))
        in_specs = [qs, qs, kv, kv, cs, cs, _layer(a_lambda, layer), _layer(g_sub, layer)]
        args = (q1, q2, k, v, cache_k, cache_v, a_lambda, g_sub)
        sem = ("parallel", "parallel")
        out_spec = qs
    else:
        grid = (nb,)
        bs = pl.BlockSpec((seq, n), lambda b: (b, 0))
        in_specs = [bs, bs, bs, bs, _layer(a_lambda, layer), _layer(g_sub, layer)]
        args = (q1, q2, k, v, a_lambda, g_sub)
        sem = ("parallel",)
        out_spec = bs
    return pl.pallas_call(
        kern, grid=grid, in_specs=in_specs, out_specs=out_spec,
        out_shape=jax.ShapeDtypeStruct((rows, n), BF16),
        compiler_params=_cparams(sem),
        name="attn_a_lat" if latent else "attn_a_ctx",
    )(*args)


def _mla_keys(ckv, kpe, wuk_ref, wuv_ref, gk_ref, bd_ref):
    cb = _bf(ckv)
    kn = _dot(cb, wuk_ref[...]) + jnp.concatenate([kpe] * B_HEADS, axis=1)
    k = kn * lax.rsqrt(_seg_sum_sq(kn, bd_ref) * (1.0 / B_QK) + EPS) * gk_ref[...]
    v = _dot(cb, wuv_ref[...])
    return k, v


def _proj_b_kernel(*refs, latent):
    if latent:
        (h_ref, w_ref, gql_ref, gkvl_ref, wuq_ref, wuk_ref, wuv_ref, gq_ref, gk_ref, bd_ref,
         c_ref, se_ref, so_ref, q_ref, k_ref, v_ref) = refs
    else:
        (h_ref, w_ref, gql_ref, gkvl_ref, wuq_ref, wuk_ref, wuv_ref, gq_ref, gk_ref, bd_ref, _, _,
         q_ref, k_ref, v_ref, nckv_ref, nkpe_ref) = refs
    p = _dot_nt(h_ref[...], w_ref[...])
    bq = p[:, 0:B_QL]
    bkv = p[:, B_QL:B_QL + B_KVL]
    kpe = p[:, B_QL + B_KVL:B_QL + B_KVL + LANES]
    q = _dot(_bf(_rms(bq, gql_ref[...])), wuq_ref[...])
    q = q * lax.rsqrt(_seg_sum_sq(q, bd_ref) * (1.0 / B_QK) + EPS) * gq_ref[...]
    ckv = _rms(bkv, gkvl_ref[...])
    k, v = _mla_keys(ckv, kpe, wuk_ref, wuv_ref, gk_ref, bd_ref)
    if latent:
        q = _rope(q, c_ref, se_ref, so_ref)
        k = _rope(k, c_ref, se_ref, so_ref)
    else:
        nckv_ref[...] = ckv.reshape(nckv_ref.shape)
        nkpe_ref[...] = kpe[:, 0:B_ROPE].reshape(nkpe_ref.shape)
    q_ref[...] = _bf(q * (B_QK ** -0.5 * LOG2E))
    k_ref[...] = _bf(k)
    v_ref[...] = _bf(v)


def _proj_b(h1, w_b, gql, gkvl, wuq, wuk, wuv, gq, gk, bd, rope_tabs, l, row0, rows, latent,
            cache_out=None):
    n = B_HEADS * B_HEAD_PAD
    nv = B_HEADS * B_V
    b0 = row0 // TM
    tab_blocks = rope_tabs[0].shape[0] // TM if latent else 1
    in_specs = [pl.BlockSpec((TM, D_MODEL), lambda i: (i + b0, 0))] + [
        _layer(a, l) for a in (w_b, gql, gkvl, wuq, wuk, wuv, gq, gk)] + [_whole(bd)]
    args = [h1, w_b, gql, gkvl, wuq, wuk, wuv, gq, gk, bd]
    row = lambda w: pl.BlockSpec((TM, w), lambda i: (i, 0))
    out_shape = [jax.ShapeDtypeStruct((rows, n), BF16), jax.ShapeDtypeStruct((rows, n), BF16),
                 jax.ShapeDtypeStruct((rows, nv), BF16)]
    out_specs = [row(n), row(n), row(nv)]
    aliases = {}
    if latent:
        in_specs += [pl.BlockSpec((TM, n), lambda i: (i % tab_blocks, 0))] * 3
        args += list(rope_tabs)
    else:
        for buf in cache_out:
            aliases[len(args)] = len(out_shape)
            in_specs.append(pl.BlockSpec(memory_space=pl.ANY))
            args.append(buf)
            out_shape.append(jax.ShapeDtypeStruct(buf.shape, buf.dtype))
            out_specs.append(_cache_out_spec(buf, l))
    return pl.pallas_call(
        functools.partial(_proj_b_kernel, latent=latent),
        grid=(rows // TM,),
        in_specs=in_specs, out_specs=out_specs, out_shape=out_shape,
        input_output_aliases=aliases,
        compiler_params=_cparams(("parallel",)),
        name="proj_b_lat" if latent else "proj_b_ctx",
    )(*args)


def _cache_b_kernel(ckv_ref, kpe_ref, wuk_ref, wuv_ref, gk_ref, bd_ref, k_ref, v_ref):
    k, v = _mla_keys(ckv_ref[...], kpe_ref[...], wuk_ref, wuv_ref, gk_ref, bd_ref)
    k_ref[...] = _bf(k)
    v_ref[...] = _bf(v)


def _cache_b(ckv, kpe_pad, wuk, wuv, gk, bd):
    db, L, past, _ = ckv.shape
    n = B_HEADS * B_HEAD_PAD
    nv = B_HEADS * B_V
    blk = lambda w: pl.BlockSpec((None, None, past, w), lambda l, b: (b, l, 0, 0))
    wl = lambda a: pl.BlockSpec((None,) + a.shape[1:], lambda l, b: (l, 0, 0))
    return pl.pallas_call(
        _cache_b_kernel,
        grid=(L, db),
        in_specs=[blk(B_KVL), blk(LANES), wl(wuk), wl(wuv), wl(gk),
                  pl.BlockSpec(bd.shape, lambda l, b: (0, 0))],
        out_specs=[blk(n), blk(nv)],
        out_shape=[jax.ShapeDtypeStruct((db, L, past, n), BF16),
                   jax.ShapeDtypeStruct((db, L, past, nv), BF16)],
        compiler_params=_cparams(("parallel", "parallel")),
        name="cache_b_expand",
    )(ckv, kpe_pad, wuk, wuv, gk, bd)


def _attn_b_kernel(*refs, latent):
    if latent:
        q_ref, k_ref, v_ref, kc_ref, vc_ref, o_ref = refs
    else:
        q_ref, k_ref, v_ref, o_ref = refs
    lane = lax.broadcasted_iota(I32, (q_ref.shape[0], 2 * B_V), 1)
    for hp in range(B_HEADS // 2):
        vsl = slice(2 * B_V * hp, 2 * B_V * (hp + 1))
        vs = [v_ref[:, vsl]] + ([vc_ref[:, vsl]] if latent else [])
        qs, kss = [], []
        for h in (2 * hp, 2 * hp + 1):
            sl = slice(B_HEAD_PAD * h, B_HEAD_PAD * (h + 1))
            qs.append(q_ref[:, sl])
            kss.append([k_ref[:, sl]] + ([kc_ref[:, sl]] if latent else []))
        outs = _attend(qs, kss, vs)
        o_ref[:, vsl] = _bf(jnp.where(lane < B_V, outs[0], outs[1]))


def _attn_b(q, k, v, cache_k, cache_v, layer, seq, latent):
    rows, n = q.shape
    nv = v.shape[1]
    nb = rows // seq
    kern = functools.partial(_attn_b_kernel, latent=latent)
    if latent:
        nq = seq // TQ
        past = cache_k.shape[2]
        grid = (nb, nq)
        in_specs = [pl.BlockSpec((TQ, n), lambda b, j: (b * nq + j, 0)),
                    pl.BlockSpec((seq, n), lambda b, j: (b, 0)),
                    pl.BlockSpec((seq, nv), lambda b, j: (b, 0)),
                    pl.BlockSpec((None, None, past, n), lambda b, j: (b, layer, 0, 0)),
                    pl.BlockSpec((None, None, past, nv), lambda b, j: (b, layer, 0, 0))]
        args = (q, k, v, cache_k, cache_v)
        out_spec = pl.BlockSpec((TQ, nv), lambda b, j: (b * nq + j, 0))
        sem = ("parallel", "parallel")
    else:
        grid = (nb,)
        in_specs = [pl.BlockSpec((seq, n), lambda b: (b, 0)),
                    pl.BlockSpec((seq, n), lambda b: (b, 0)),
                    pl.BlockSpec((seq, nv), lambda b: (b, 0))]
        args = (q, k, v)
        out_spec = pl.BlockSpec((seq, nv), lambda b: (b, 0))
        sem = ("parallel",)
    return pl.pallas_call(
        kern, grid=grid, in_specs=in_specs, out_specs=out_spec,
        out_shape=jax.ShapeDtypeStruct((rows, nv), BF16),
        compiler_params=_cparams(sem),
        name="attn_b_lat" if latent else "attn_b_ctx",
    )(*args)


CONV_CHUNK = 64


def _mix_cd_kernel(h_ref, w_ref, dw_ref, dwb_ref, lng_ref, lnb_ref, bdd_ref, dsc_ref, pm_ref, pcnt_ref,
                   oc_ref, od_ref, gpad, dpad, gsh, dsh, *, seq):
    p = _dot_nt(h_ref[...], w_ref[...])
    glu = p[:, 0:C_CH] * jax.nn.sigmoid(p[:, C_CH:2 * C_CH])
    gpad[0:C_PAD, :] = jnp.zeros((C_PAD, C_CH), F32)
    gpad[C_PAD + seq:2 * C_PAD + seq, :] = jnp.zeros((C_PAD, C_CH), F32)
    gpad[C_PAD:C_PAD + seq, :] = glu
    half = C_W // 2
    span = seq + 2 * C_PAD - SUB
    for r in range(1, SUB):
        gsh[(r - 1) * span:r * span, :] = gpad[r:r + span, :]
    for c0 in range(0, seq, CONV_CHUNK):
        acc = jnp.zeros((CONV_CHUNK, C_CH), F32) + dwb_ref[...]
        for j in range(C_W):
            s = c0 + C_PAD - half + j
            r = s % SUB
            if r == 0:
                tap = gpad[s:s + CONV_CHUNK, :]
            else:
                tap = gsh[(r - 1) * span + s - r:(r - 1) * span + s - r + CONV_CHUNK, :]
            acc = acc + tap * dw_ref[j:j + 1, :]
        mu = jnp.mean(acc, axis=-1, keepdims=True)
        xc = acc - mu
        y = xc * lax.rsqrt(jnp.mean(xc * xc, axis=-1, keepdims=True) + EPS)
        y = y * lng_ref[...] + lnb_ref[...]
        oc_ref[c0:c0 + CONV_CHUNK, :] = _bf(y * jax.nn.sigmoid(y))
    d = p[:, 2 * C_CH:2 * C_CH + D_CH]
    dpad[0:D_PAD, :] = jnp.zeros((D_PAD, D_CH), F32)
    dpad[D_PAD + seq:2 * D_PAD + seq, :] = jnp.zeros((D_PAD, D_CH), F32)
    dpad[D_PAD:D_PAD + seq, :] = d
    dspan = seq + 2 * D_PAD - SUB
    for r in range(1, SUB):
        dsh[(r - 1) * dspan:r * dspan, :] = dpad[r:r + dspan, :]
    for c0 in range(0, seq, CONV_CHUNK):
        acc = jnp.zeros((CONV_CHUNK, D_CH), F32)
        for j in range(2 * D_PAD):
            s = c0 + j
            r = s % SUB
            if r == 0:
                tap = dpad[s:s + CONV_CHUNK, :]
            else:
                tap = dsh[(r - 1) * dspan + s - r:(r - 1) * dspan + s - r + CONV_CHUNK, :]
            acc = acc + tap * pm_ref[j:j + 1, :]
        pooled = acc / pcnt_ref[c0:c0 + CONV_CHUNK, :]
        diff = pooled - dpad[c0 + D_PAD:c0 + D_PAD + CONV_CHUNK, :]
        od_ref[c0:c0 + CONV_CHUNK, :] = _bf(_dot(_bf(diff), bdd_ref[...]) * dsc_ref[...])


def _mix_cd(h1, w_cd, dw, dwb, lng, lnb, bdd, dsc, pmask, l, row0, rows, seq):
    b0 = row0 // seq
    pcnt = _pool_counts(seq)
    return pl.pallas_call(
        functools.partial(_mix_cd_kernel, seq=seq),
        grid=(rows // seq,),
        in_specs=[pl.BlockSpec((seq, D_MODEL), lambda b: (b + b0, 0))] + [
            _layer(a, l) for a in (w_cd, dw, dwb, lng, lnb, bdd, dsc)] + [_whole(pmask), _whole(pcnt)],
        out_specs=[pl.BlockSpec((seq, C_CH), lambda b: (b, 0)),
                   pl.BlockSpec((seq, D_CH), lambda b: (b, 0))],
        out_shape=[jax.ShapeDtypeStruct((rows, C_CH), BF16),
                   jax.ShapeDtypeStruct((rows, D_CH), BF16)],
        scratch_shapes=[pltpu.VMEM((seq + 2 * C_PAD, C_CH), F32),
                        pltpu.VMEM((seq + 2 * D_PAD, D_CH), F32),
                        pltpu.VMEM(((SUB - 1) * (seq + 2 * C_PAD - SUB), C_CH), F32),
                        pltpu.VMEM(((SUB - 1) * (seq + 2 * D_PAD - SUB), D_CH), F32)],
        compiler_params=_cparams(("parallel",)),
        name="mix_cd_%d" % seq,
    )(h1, w_cd, dw, dwb, lng, lnb, bdd, dsc, pmask, pcnt)


def _merge_kernel(*refs, moe, split_x, n_ctx_tiles):
    is_ctx = pl.program_id(0) < n_ctx_tiles
    if split_x:
        x_in = jnp.where(is_ctx, refs[0][...], refs[1][...])
        refs = refs[2:]
    else:
        x_in = refs[0][...]
        refs = refs[1:]
    (h_ref, oac_ref, oal_ref, obc_ref, obl_ref, occ_ref, ocl_ref, odc_ref, odl_ref,
     mod_ref, n2g_ref, wg_ref, wa_ref, wb_ref, wc_ref, wd_ref, wo_ref) = refs[:17]
    if moe:
        rhi_ref, rlo_ref, x1_ref, h2t_ref, ridx_ref, rw_ref, sel_ref = refs[17:]
    else:
        x1_ref, h2_ref = refs[17:]
    h = h_ref[...]
    acc = None
    for i, (c_ref, l_ref, w_ref) in enumerate(((oac_ref, oal_ref, wa_ref), (obc_ref, obl_ref, wb_ref),
                                               (occ_ref, ocl_ref, wc_ref), (odc_ref, odl_ref, wd_ref))):
        gate = jax.nn.sigmoid(_dot_nt(h, wg_ref[i * D_MODEL:(i + 1) * D_MODEL, :]))
        o = jnp.where(is_ctx, c_ref[...], l_ref[...])
        t = gate * _dot(o, w_ref[...])
        acc = t if acc is None else acc + t
    y = _dot(_bf(acc), wo_ref[...])
    x1 = x_in + mod_ref[:, 2 * D_MODEL:3 * D_MODEL] * y
    x1_ref[...] = x1
    h2 = (_rms(x1, n2g_ref[...]) * (1.0 + mod_ref[:, 4 * D_MODEL:5 * D_MODEL])
          + mod_ref[:, 3 * D_MODEL:4 * D_MODEL])
    if not moe:
        h2_ref[...] = _bf(h2)
        return
    tm = h2.shape[0]
    for j in range(SUB):
        h2t_ref[pl.ds(j, tm, stride=SUB), :] = h2[:, LANES * j:LANES * (j + 1)]
    hi = _bf(h2)
    lo = _bf(h2 - hi.astype(F32))
    logits = _dot(hi, rhi_ref[...]) + _dot(lo, rhi_ref[...]) + _dot(hi, rlo_ref[...])
    lane = lax.broadcasted_iota(I32, logits.shape, 1)
    lanef = lane.astype(F32)
    neg = jnp.float32(-jnp.inf)
    lg = jnp.where(lane < N_EXP, logits, neg)
    m0 = jnp.max(lg, axis=-1, keepdims=True)
    i0 = jnp.min(jnp.where(lg == m0, lanef, float(N_EXP_PAD)), axis=-1, keepdims=True)
    sel0 = lanef == i0
    lg1 = jnp.where(sel0, neg, lg)
    m1 = jnp.max(lg1, axis=-1, keepdims=True)
    i1 = jnp.min(jnp.where(lg1 == m1, lanef, float(N_EXP_PAD)), axis=-1, keepdims=True)
    sel1 = lanef == i1
    e = jnp.exp(m1 - m0)
    w0 = 1.0 / (1.0 + e)
    w1 = e / (1.0 + e)
    ridx_ref[...] = jnp.where(lane == 0, i0, jnp.where(lane == 1, i1, 0.0)).astype(I32)
    rw_ref[...] = jnp.where(lane == 0, w0, jnp.where(lane == 1, w1, 0.0))
    sel_ref[...] = jnp.where(sel0 | sel1, 1.0, 0.0).astype(BF16)


def _merge(x, h1, branches, mod4, n2g, wg, wa, wb, wc, wd, wo, router, l, j, t_ctx, dec_seq):
    T = h1.shape[0]
    moe = router is not None
    split_x = isinstance(x, tuple)
    tm = TM_MERGE
    nc = t_ctx // tm
    row = functools.partial(_mod_row, tm=tm, t_ctx=t_ctx, dec_seq=dec_seq)
    rowspec = lambda w: pl.BlockSpec((tm, w), lambda i: (i, 0))
    if split_x:
        in_specs = _ctx_lat_specs(tm, D_MODEL, nc) + [rowspec(D_MODEL)]
        args = [x[0], x[1], h1]
    else:
        in_specs = [rowspec(D_MODEL), rowspec(D_MODEL)]
        args = [x, h1]
    for oc, ol in branches:
        in_specs += _ctx_lat_specs(tm, oc.shape[1], nc)
        args += [oc, ol]
    in_specs += [pl.BlockSpec((None, None, 1, 6 * D_MODEL), lambda i: (l, row(i), 0, 0))]
    in_specs += [_layer(a, l) for a in (n2g, wg, wa, wb, wc, wd, wo)]
    args += [mod4, n2g, wg, wa, wb, wc, wd, wo]
    out_shape = [jax.ShapeDtypeStruct((T, D_MODEL), F32)]
    out_specs = [rowspec(D_MODEL)]
    if moe:
        in_specs += [_layer(router[0], j), _layer(router[1], j)]
        args += list(router)
        out_shape += [jax.ShapeDtypeStruct((T * SUB, LANES), F32),
                      jax.ShapeDtypeStruct((T, N_EXP_PAD), I32),
                      jax.ShapeDtypeStruct((T, N_EXP_PAD), F32),
                      jax.ShapeDtypeStruct((T, N_EXP_PAD), BF16)]
        out_specs += [pl.BlockSpec((tm * SUB, LANES), lambda i: (i, 0)),
                      rowspec(N_EXP_PAD), rowspec(N_EXP_PAD), rowspec(N_EXP_PAD)]
    else:
        out_shape += [jax.ShapeDtypeStruct((T, D_MODEL), BF16)]
        out_specs += [rowspec(D_MODEL)]
    return pl.pallas_call(
        functools.partial(_merge_kernel, moe=moe, split_x=split_x, n_ctx_tiles=nc),
        grid=(T // tm,),
        in_specs=in_specs, out_specs=out_specs, out_shape=out_shape,
        compiler_params=_cparams(("parallel",)),
        name="merge_moe" if moe else "merge",
    )(*args)


def _swiglu_step(x, w1_ref, w3_ref, w2_ref, acc_ref, f):
    a = _dot(x, _bf(w1_ref[...]))
    b = _dot(x, _bf(w3_ref[...]))
    t = _dot(_bf(a * jax.nn.sigmoid(a) * b), _bf(w2_ref[...]))

    @pl.when(f == 0)
    def _():
        acc_ref[...] = t

    @pl.when(f > 0)
    def _():
        acc_ref[...] += t


def _pre_norm1(x, mod_ref, g_ref):
    return _bf(_rms(x, g_ref[...]) * (1.0 + mod_ref[:, D_MODEL:2 * D_MODEL]) + mod_ref[:, 0:D_MODEL])


def _ffn_dense_kernel(x_ref, w1_ref, w3_ref, w2_ref, x1_ref, mod_ref, modn_ref, gn_ref,
                      o_ref, hn_ref, acc_ref):
    f = pl.program_id(1)
    _swiglu_step(x_ref[...], w1_ref, w3_ref, w2_ref, acc_ref, f)

    @pl.when(f == pl.num_programs(1) - 1)
    def _():
        x2 = x1_ref[...] + mod_ref[:, 5 * D_MODEL:6 * D_MODEL] * acc_ref[...]
        o_ref[...] = x2
        hn_ref[...] = _pre_norm1(x2, modn_ref, gn_ref)


def _ffn_dense(h2, w1, w3, w2, x1, mod4, n1g, l, j, t_ctx, dec_seq):
    T = h2.shape[0]
    nf = w1.shape[2] // TF_DENSE
    row = functools.partial(_mod_row, tm=TG, t_ctx=t_ctx, dec_seq=dec_seq)
    rows = pl.BlockSpec((TG, D_MODEL), lambda i, f: (i, 0))
    return pl.pallas_call(
        _ffn_dense_kernel,
        grid=(T // TG, nf),
        in_specs=[rows,
                  pl.BlockSpec((None, D_MODEL, TF_DENSE), lambda i, f: (j, 0, f)),
                  pl.BlockSpec((None, D_MODEL, TF_DENSE), lambda i, f: (j, 0, f)),
                  pl.BlockSpec((None, TF_DENSE, D_MODEL), lambda i, f: (j, f, 0)),
                  rows,
                  pl.BlockSpec((None, None, 1, 6 * D_MODEL), lambda i, f: (l, row(i), 0, 0)),
                  pl.BlockSpec((None, None, 1, 6 * D_MODEL), lambda i, f: (l + 1, row(i), 0, 0)),
                  _layer(n1g, l + 1)],
        out_specs=[rows, rows],
        out_shape=[jax.ShapeDtypeStruct((T, D_MODEL), F32), jax.ShapeDtypeStruct((T, D_MODEL), BF16)],
        scratch_shapes=[pltpu.VMEM((TG, D_MODEL), F32)],
        compiler_params=_cparams(("parallel", "arbitrary")),
        name="ffn_dense",
    )(h2, w1, w3, w2, x1, mod4, mod4, n1g)


def _ffn_experts_kernel(te_ref, na_ref, x_ref, w1_ref, w3_ref, w2_ref, o_ref, xb_ref, acc_ref):
    i = pl.program_id(0)
    f = pl.program_id(1)
    active = i < na_ref[0]

    @pl.when(jnp.logical_and(active, f == 0))
    def _():
        for j in range(SUB):
            xb_ref[:, LANES * j:LANES * (j + 1)] = _bf(x_ref[pl.ds(j, TG, stride=SUB), :])

    @pl.when(active)
    def _():
        _swiglu_step(xb_ref[...], w1_ref, w3_ref, w2_ref, acc_ref, f)

    @pl.when(jnp.logical_and(active, f == pl.num_programs(1) - 1))
    def _():
        for j in range(SUB):
            o_ref[pl.ds(j, TG, stride=SUB), :] = acc_ref[:, LANES * j:LANES * (j + 1)]

    @pl.when(jnp.logical_and(jnp.logical_not(active), f == 0))
    def _():
        o_ref[...] = jnp.zeros(o_ref.shape, F32)


def _ffn_experts(xs, w1, w3, w2, tile_e, n_active, j):
    rows = xs.shape[0] // SUB
    nf = w1.shape[3] // TF_MOE

    def fidx(i, f, na):
        return jnp.where(i < na[0], f, nf - 1)

    def xidx(i, na):
        return jnp.minimum(i, na[0] - 1)

    return pl.pallas_call(
        _ffn_experts_kernel,
        grid_spec=pltpu.PrefetchScalarGridSpec(
            num_scalar_prefetch=2, grid=(rows // TG, nf),
            in_specs=[pl.BlockSpec((TG * SUB, LANES), lambda i, f, te, na: (xidx(i, na), 0)),
                      pl.BlockSpec((None, None, D_MODEL, TF_MOE),
                                   lambda i, f, te, na: (j, te[i], 0, fidx(i, f, na))),
                      pl.BlockSpec((None, None, D_MODEL, TF_MOE),
                                   lambda i, f, te, na: (j, te[i], 0, fidx(i, f, na))),
                      pl.BlockSpec((None, None, TF_MOE, D_MODEL),
                                   lambda i, f, te, na: (j, te[i], fidx(i, f, na), 0))],
            out_specs=pl.BlockSpec((TG * SUB, LANES), lambda i, f, te, na: (i, 0)),
            scratch_shapes=[pltpu.VMEM((TG, D_MODEL), BF16), pltpu.VMEM((TG, D_MODEL), F32)]),
        out_shape=jax.ShapeDtypeStruct((rows * SUB, LANES), F32),
        compiler_params=_cparams(("parallel", "arbitrary")),
        name="ffn_experts",
    )(tile_e, n_active, xs, w1, w3, w2)


def _rank_kernel(sel_ref, tri_ref, rank_ref, cnt_ref, carry):
    i = pl.program_id(0)

    @pl.when(i == 0)
    def _():
        carry[...] = jnp.zeros(carry.shape, F32)

    s = sel_ref[...]
    rank_ref[...] = _dot(tri_ref[...], s) + carry[...]
    carry[...] += jnp.sum(s.astype(F32), axis=0, keepdims=True)
    cnt_ref[...] = jnp.broadcast_to(carry[...], cnt_ref.shape)


def _rank(sel, tri):
    T = sel.shape[0]
    tr = tri.shape[0]
    return pl.pallas_call(
        _rank_kernel,
        grid=(T // tr,),
        in_specs=[pl.BlockSpec((tr, N_EXP_PAD), lambda i: (i, 0)),
                  pl.BlockSpec((tr, tr), lambda i: (0, 0))],
        out_specs=[pl.BlockSpec((tr, N_EXP_PAD), lambda i: (i, 0)),
                   pl.BlockSpec((8, N_EXP_PAD), lambda i: (0, 0))],
        out_shape=[jax.ShapeDtypeStruct((T, N_EXP_PAD), F32),
                   jax.ShapeDtypeStruct((8, N_EXP_PAD), F32)],
        scratch_shapes=[pltpu.VMEM((1, N_EXP_PAD), F32)],
        compiler_params=_cparams(("arbitrary",)),
        name="route_rank",
    )(sel, tri)


TILE_ROWS = TR * SUB


def _token_tile(ref, tok):
    return ref.at[pl.ds(pl.multiple_of(tok * SUB, SUB), SUB)]


def _dispatch_kernel(pos0_ref, pos1_ref, ends_ref, padded_ref, h_ref, xs_hbm, stage, zbuf, sem, zsem):
    i = pl.program_id(0)

    @pl.when(i == 0)
    def _():
        zbuf[...] = jnp.zeros(zbuf.shape, F32)

        def fill_tile(first_slot):
            start = pl.multiple_of(first_slot * SUB, TG * SUB)
            return pltpu.make_async_copy(zbuf, xs_hbm.at[pl.ds(start, TG * SUB)], zsem.at[0])

        n_slots = xs_hbm.shape[0] // SUB
        for wait in (False, True):
            for e in range(N_EXP):
                for cond, first in ((padded_ref[e] > 0, ends_ref[e] - TG),
                                    (ends_ref[N_EXP - 1] + e * TG < n_slots, ends_ref[N_EXP - 1] + e * TG)):
                    @pl.when(cond)
                    def _():
                        fill = fill_tile(first)
                        if wait:
                            fill.wait()
                        else:
                            fill.start()

    slot = i % 2
    base = pl.multiple_of(slot * TILE_ROWS, TILE_ROWS)
    stage[pl.ds(base, TILE_ROWS), :] = h_ref[...]

    def issue(r, c):
        t = i * TR + r
        src = _token_tile(stage, slot * TR + r)
        pltpu.make_async_copy(src, _token_tile(xs_hbm, pos0_ref[t]), sem.at[slot]).start()
        pltpu.make_async_copy(src, _token_tile(xs_hbm, pos1_ref[t]), sem.at[slot]).start()
        return c
    lax.fori_loop(0, TR, issue, 0, unroll=8)

    def wait_slot(s):
        b = pl.multiple_of(s * TILE_ROWS, TILE_ROWS)
        for _ in range(2):
            pltpu.make_async_copy(stage.at[pl.ds(b, TILE_ROWS)], xs_hbm.at[pl.ds(0, TILE_ROWS)],
                                  sem.at[s]).wait()

    @pl.when(i > 0)
    def _():
        wait_slot(1 - slot)

    @pl.when(i == pl.num_programs(0) - 1)
    def _():
        wait_slot(slot)


def _dispatch(pos0, pos1, ends, padded, h2t, n_rows):
    T = h2t.shape[0] // SUB
    assert n_rows - 2 * T <= N_EXP * TG
    return pl.pallas_call(
        _dispatch_kernel,
        grid_spec=pltpu.PrefetchScalarGridSpec(
            num_scalar_prefetch=4, grid=(T // TR,),
            in_specs=[pl.BlockSpec((TILE_ROWS, LANES), lambda i, *_: (i, 0))],
            out_specs=pl.BlockSpec(memory_space=pl.ANY),
            scratch_shapes=[pltpu.VMEM((2 * TILE_ROWS, LANES), F32),
                            pltpu.VMEM((TG * SUB, LANES), F32),
                            pltpu.SemaphoreType.DMA((2,)),
                            pltpu.SemaphoreType.DMA((1,))]),
        out_shape=jax.ShapeDtypeStruct((n_rows * SUB, LANES), F32),
        compiler_params=_cparams(("arbitrary",)),
        name="moe_dispatch",
    )(pos0, pos1, ends, padded, h2t)


def _combine_kernel(pos0_ref, pos1_ref, y_hbm, x1_ref, rw_ref, mod_ref, *rest, emit_next, n_ctx_tiles):
    if emit_next:
        modn_ref, gn_ref, o_ref, hn_ref, buf0, buf1, sem0, sem1 = rest
    else:
        oc_ref, ol_ref, buf0, buf1, sem0, sem1 = rest
    i = pl.program_id(0)
    nt = pl.num_programs(0)

    def issue(tile, slot):
        def body(r, c):
            t = tile * TR + r
            pltpu.make_async_copy(_token_tile(y_hbm, pos0_ref[t]), _token_tile(buf0, slot * TR + r),
                                  sem0.at[slot]).start()
            pltpu.make_async_copy(_token_tile(y_hbm, pos1_ref[t]), _token_tile(buf1, slot * TR + r),
                                  sem1.at[slot]).start()
            return c
        lax.fori_loop(0, TR, body, 0, unroll=8)

    @pl.when(i == 0)
    def _():
        issue(0, 0)

    @pl.when(i + 1 < nt)
    def _():
        issue(i + 1, (i + 1) % 2)

    slot = i % 2
    base = pl.multiple_of(slot * TILE_ROWS, TILE_ROWS)
    for buf, sem in ((buf0, sem0), (buf1, sem1)):
        pltpu.make_async_copy(y_hbm.at[pl.ds(0, TILE_ROWS)], buf.at[pl.ds(base, TILE_ROWS)],
                              sem.at[slot]).wait()
    rw = rw_ref[...]
    w0 = rw[:, 0:1]
    w1 = rw[:, 1:2]

    def write(o_ref):
        for j in range(SUB):
            sl = slice(LANES * j, LANES * (j + 1))
            f = (w0 * buf0[pl.ds(base + j, TR, stride=SUB), :]
                 + w1 * buf1[pl.ds(base + j, TR, stride=SUB), :])
            o_ref[:, sl] = x1_ref[:, sl] + mod_ref[:, 5 * D_MODEL + LANES * j:5 * D_MODEL + LANES * (j + 1)] * f

    if emit_next:
        write(o_ref)
        hn_ref[...] = _pre_norm1(o_ref[...], modn_ref, gn_ref)
    else:
        @pl.when(i < n_ctx_tiles)
        def _():
            write(oc_ref)

        @pl.when(i >= n_ctx_tiles)
        def _():
            write(ol_ref)


def _combine(pos0, pos1, ys, x1, rw, mod4, n1g, l, t_ctx, dec_seq, emit_next):
    T = x1.shape[0]
    nc = t_ctx // TR
    row = functools.partial(_mod_row, tm=TR, t_ctx=t_ctx, dec_seq=dec_seq)
    rows = pl.BlockSpec((TR, D_MODEL), lambda i, p0, p1: (i, 0))
    in_specs = [pl.BlockSpec(memory_space=pl.ANY), rows,
                pl.BlockSpec((TR, N_EXP_PAD), lambda i, p0, p1: (i, 0)),
                pl.BlockSpec((None, None, 1, 6 * D_MODEL), lambda i, p0, p1: (l, row(i), 0, 0))]
    args = [pos0, pos1, ys, x1, rw, mod4]
    if emit_next:
        in_specs += [pl.BlockSpec((None, None, 1, 6 * D_MODEL), lambda i, p0, p1: (l + 1, row(i), 0, 0)),
                     _layer(n1g, l + 1)]
        args += [mod4, n1g]
        out_specs = [rows, rows]
        out_shape = [jax.ShapeDtypeStruct((T, D_MODEL), F32), jax.ShapeDtypeStruct((T, D_MODEL), BF16)]
    else:
        out_specs = [pl.BlockSpec((TR, D_MODEL), lambda i, p0, p1: (jnp.minimum(i, nc - 1), 0)),
                     pl.BlockSpec((TR, D_MODEL), lambda i, p0, p1: (jnp.maximum(i - nc, 0), 0))]
        out_shape = [jax.ShapeDtypeStruct((t_ctx, D_MODEL), F32),
                     jax.ShapeDtypeStruct((T - t_ctx, D_MODEL), F32)]
    return pl.pallas_call(
        functools.partial(_combine_kernel, emit_next=emit_next, n_ctx_tiles=nc),
        grid_spec=pltpu.PrefetchScalarGridSpec(
            num_scalar_prefetch=2, grid=(T // TR,),
            in_specs=in_specs, out_specs=out_specs,
            scratch_shapes=[pltpu.VMEM((2 * TILE_ROWS, LANES), F32), pltpu.VMEM((2 * TILE_ROWS, LANES), F32),
                            pltpu.SemaphoreType.DMA((2,)), pltpu.SemaphoreType.DMA((2,))]),
        out_shape=out_shape,
        compiler_params=_cparams(("arbitrary",)),
        name="moe_combine",
    )(*args)


def _block_ones(n, seg):
    idx = np.arange(n) // seg
    return jnp.asarray((idx[:, None] == idx[None, :]).astype(np.float32), dtype=BF16)


def _rope_tables(seq, head_w, rot_dim, n_heads):
    rows = seq // GRID_W
    nf = rot_dim // 4
    freqs = ROPE_THETA ** (-np.arange(nf, dtype=np.float64) / nf)
    row = np.repeat(np.arange(rows, dtype=np.float64), GRID_W)
    col = np.tile(np.arange(GRID_W, dtype=np.float64), rows)
    ang = np.concatenate([row[:, None] * freqs, col[:, None] * freqs], axis=-1)
    ang = np.repeat(ang, 2, axis=-1)
    even = (np.arange(rot_dim) % 2 == 0)[None, :]
    c = np.ones((seq, head_w))
    se = np.zeros((seq, head_w))
    so = np.zeros((seq, head_w))
    c[:, :rot_dim] = np.cos(ang)
    se[:, :rot_dim] = np.where(even, -np.sin(ang), 0.0)
    so[:, :rot_dim] = np.where(even, 0.0, np.sin(ang))
    return tuple(jnp.asarray(np.tile(t, (1, n_heads)), dtype=F32) for t in (c, se, so))


def _pool_mask():
    m = np.zeros((2 * D_PAD, D_CH), np.float32)
    for gi, w in enumerate(POOL):
        left = w // 2
        right = w - 1 - left
        for off in range(-left, right + 1):
            m[off + D_PAD, gi * D_G:(gi + 1) * D_G] = 1.0
    return jnp.asarray(m)


def _pool_counts(seq):
    t = np.arange(seq)
    m = np.zeros((seq, D_CH), np.float32)
    for gi, w in enumerate(POOL):
        left = w // 2
        right = w - 1 - left
        m[:, gi * D_G:(gi + 1) * D_G] = (np.minimum(t + right + 1, seq) - np.maximum(t - left, 0))[:, None]
    return jnp.asarray(m)


def _b_heads(nope, rope):
    ref = nope if nope is not None else rope
    lead = ref.shape[:-1]
    z = lambda w: jnp.zeros(lead + (w,), ref.dtype)
    parts = [rope if rope is not None else z(B_ROPE), nope if nope is not None else z(B_NOPE),
             z(B_HEAD_PAD - B_QK)]
    out = jnp.concatenate(parts, axis=-1)
    return out.reshape(lead[:-1] + (lead[-1] * B_HEAD_PAD,))


def kernel(x_prompt, x_sample, cache_a_k, cache_a_v, cache_b_ckv, cache_b_kpe, c, c_ctx, ada_w, ada_b, norm1_g, norm2_g, w_in, a_q_norm, a_k_norm, a_lambda, a_sub_norm, a_w_o, b_q_lora_norm, b_kv_lora_norm, b_w_uq, b_w_ukv, b_q_norm, b_k_norm, b_w_o, c_dw, c_dw_b, c_ln_g, c_ln_b, c_w_o, d_w_group, d_scale, d_w_o, w_out, ffn_w1, ffn_w3, ffn_w2, moe_router, moe_w1, moe_w3, moe_w2):
    nb, seq, _ = x_prompt.shape
    db, dseq, _ = x_sample.shape
    L = w_in.shape[0]
    past = cache_a_k.shape[2]
    t_ctx = nb * seq
    t_lat = db * dseq
    T = t_ctx + t_lat
    na = 2 * A_HEADS * A_QK
    assert t_ctx % TG == 0 and dseq % TG == 0 and seq % TR == 0 and db + 1 <= 16

    x = (x_prompt.reshape(t_ctx, D_MODEL), x_sample.reshape(t_lat, D_MODEL))
    cv = jnp.concatenate([c_ctx[None, :], c, jnp.zeros((15 - db, D_MODEL), F32)], axis=0)
    mod4 = _ada_all(cv, ada_w, ada_b).reshape(L, 16, 1, 6 * D_MODEL)

    bd_a = _block_ones(na, A_QK)
    bd_b = _block_ones(B_HEADS * B_HEAD_PAD, B_HEAD_PAD)
    rope_a = _rope_tables(dseq, A_QK, A_QK, 2 * A_HEADS)
    rope_b = _rope_tables(dseq, B_HEAD_PAD, B_ROPE, B_HEADS)
    pmask = _pool_mask()
    tri = jnp.asarray(np.tril(np.ones((TG, TG), np.float32), -1), dtype=BF16)

    row1 = lambda a: a.reshape(L, 1, -1)
    w_a, w_b, w_cd, w_g = _repack(jnp.transpose(w_in, (0, 2, 1)))
    gq_a = row1(jnp.tile(a_q_norm, (1, 2 * A_HEADS)))
    gk_a = row1(jnp.tile(a_k_norm, (1, 2 * A_HEADS)))
    uq = b_w_uq.reshape(L, B_QL, B_HEADS, B_QK)
    wuq = _b_heads(uq[..., :B_NOPE], uq[..., B_NOPE:]).astype(BF16)
    ukv = b_w_ukv.reshape(L, B_KVL, B_HEADS, B_NOPE + B_V)
    wuk = _b_heads(ukv[..., :B_NOPE], None).astype(BF16)
    wuv = ukv[..., B_NOPE:].reshape(L, B_KVL, B_HEADS * B_V).astype(BF16)
    gq_b = row1(jnp.tile(_b_heads(b_q_norm[:, None, :B_NOPE], b_q_norm[:, None, B_NOPE:]), (1, B_HEADS)))
    gk_b = row1(jnp.tile(_b_heads(b_k_norm[:, None, :B_NOPE], b_k_norm[:, None, B_NOPE:]), (1, B_HEADS)))
    bdd = jnp.zeros((L, D_CH, D_CH), F32)
    for gi in range(len(POOL)):
        bdd = bdd.at[:, gi * D_G:(gi + 1) * D_G, gi * D_G:(gi + 1) * D_G].set(d_w_group[:, gi])
    bdd = bdd.astype(BF16)
    a_wo, b_wo, c_wo, d_wo, wo = (w.astype(BF16) for w in (a_w_o, b_w_o, c_w_o, d_w_o, w_out))
    ffn1, ffn3, ffn2 = (w.astype(BF16) for w in (ffn_w1, ffn_w3, ffn_w2))
    r_pad = jnp.pad(moe_router, ((0, 0), (0, 0), (0, N_EXP_PAD - N_EXP)))
    r_hi = r_pad.astype(BF16)
    r_lo = (r_pad - r_hi.astype(F32)).astype(BF16)
    n1g, n2g = row1(norm1_g), row1(norm2_g)
    g_sub = row1(a_sub_norm)
    gql, gkvl = row1(b_q_lora_norm), row1(b_kv_lora_norm)
    dwb, lng, lnb, dsc = row1(c_dw_b), row1(c_ln_g), row1(c_ln_b), row1(d_scale)

    ck_a = cache_a_k.reshape(db, L, past, na)
    cv_a = cache_a_v.reshape(db, L, past, A_HEADS * A_V)
    kpe_pad = jnp.pad(cache_b_kpe, ((0, 0), (0, 0), (0, 0), (0, LANES - B_ROPE)))
    ck_b, cv_b = _cache_b(cache_b_ckv, kpe_pad, wuk, wuv, gk_b, bd_b)

    n_rows = 2 * T + N_EXP * TG
    nt_g = n_rows // TG

    new_ak = jnp.zeros((nb, L, seq, na), F32)
    new_av = jnp.zeros((nb, L, seq, A_HEADS * A_V), F32)
    new_ckv = jnp.zeros((nb, L, seq, B_KVL), F32)
    new_kpe = jnp.zeros((nb, L, seq, B_ROPE), F32)
    h1 = _prep(x[0], x[1], mod4, n1g, 0, dseq)
    for l in range(L):
        last = l == L - 1

        q1c, q2c, kc, vc, new_ak, new_av = _proj_a(h1, w_a, gq_a, gk_a, bd_a, None, l, 0, t_ctx, False,
                                                   cache_out=(new_ak, new_av))
        q1l, q2l, kl, vl = _proj_a(h1, w_a, gq_a, gk_a, bd_a, rope_a, l, t_ctx, t_lat, True)
        oa = (_attn_a(q1c, q2c, kc, vc, None, None, a_lambda, g_sub, l, seq, False),
              _attn_a(q1l, q2l, kl, vl, ck_a, cv_a, a_lambda, g_sub, l, dseq, True))

        bargs = (w_b, gql, gkvl, wuq, wuk, wuv, gq_b, gk_b, bd_b)
        qc, kc, vc, new_ckv, new_kpe = _proj_b(h1, *bargs, None, l, 0, t_ctx, False,
                                               cache_out=(new_ckv, new_kpe))
        ql, kl, vl = _proj_b(h1, *bargs, rope_b, l, t_ctx, t_lat, True)
        ob = (_attn_b(qc, kc, vc, None, None, l, seq, False),
              _attn_b(ql, kl, vl, ck_b, cv_b, l, dseq, True))

        cdargs = (w_cd, c_dw, dwb, lng, lnb, bdd, dsc, pmask)
        occ, odc = _mix_cd(h1, *cdargs, l, 0, t_ctx, seq)
        ocl, odl = _mix_cd(h1, *cdargs, l, t_ctx, t_lat, dseq)

        j = l // 2
        moe = l % 2 == 1
        router = (r_hi, r_lo) if moe else None
        outs = _merge(x, h1, (oa, ob, (occ, ocl), (odc, odl)), mod4, n2g, w_g, a_wo, b_wo,
                      c_wo, d_wo, wo, router, l, j, t_ctx, dseq)
        if not moe:
            x1, h2 = outs
            assert not last, "a dense layer is always followed by another layer"
            x, h1 = _ffn_dense(h2, ffn1, ffn3, ffn2, x1, mod4, n1g, l, j, t_ctx, dseq)
        else:
            x1, h2t, ridx, rw, sel = outs
            rank, cnt = _rank(sel, tri)
            counts = cnt[0, :N_EXP].astype(I32)
            padded = ((counts + TG - 1) // TG) * TG
            ends = jnp.cumsum(padded)
            offs = ends - padded
            pos_all = offs[None, :] + rank[:, :N_EXP].astype(I32)
            lane8 = jnp.arange(N_EXP, dtype=I32)[None, :]
            pos0 = jnp.sum(jnp.where(lane8 == ridx[:, 0:1], pos_all, 0), axis=1)
            pos1 = jnp.sum(jnp.where(lane8 == ridx[:, 1:2], pos_all, 0), axis=1)
            tile_start = jnp.arange(nt_g, dtype=I32) * TG
            tile_e = jnp.minimum(jnp.sum(tile_start[:, None] >= ends[None, :], axis=1), N_EXP - 1).astype(I32)
            n_active = (ends[-1] // TG).astype(I32).reshape(1)
            last_e = tile_e[jnp.maximum(n_active[0] - 1, 0)]
            tile_e = jnp.where(tile_start < ends[-1], tile_e, last_e)
            xs = _dispatch(pos0, pos1, ends.astype(I32), padded, h2t, n_rows)
            ys = _ffn_experts(xs, moe_w1, moe_w3, moe_w2, tile_e, n_active, j)
            outs = _combine(pos0, pos1, ys, x1, rw, mod4, n1g, l, t_ctx, dseq, not last)
            if last:
                y_ctx, y_lat = outs
            else:
                x, h1 = outs

    assert L % 2 == 0, "the last layer is a routed layer, whose combine step emits the two outputs"
    y_prompt = y_ctx.reshape(nb, seq, D_MODEL)
    y_sample = y_lat.reshape(db, dseq, D_MODEL)
    new_a_k = new_ak.reshape(nb, L, seq, 2 * A_HEADS, A_QK)
    new_a_v = new_av.reshape(nb, L, seq, A_HEADS, A_V)
    return (y_prompt, y_sample, new_a_k, new_a_v, new_ckv, new_kpe)
```

```python
import functools
import math

import numpy as np
import jax
import jax.numpy as jnp
from jax import lax
from jax.experimental import pallas as pl
from jax.experimental.pallas import tpu as pltpu

F32 = jnp.float32
BF16 = jnp.bfloat16
I32 = jnp.int32

EPS = 1e-6
D_MODEL = 1024
GRID_W = 64
ROPE_THETA = 10000.0
A_HEADS = 4
A_QK = 64
A_V = 128
B_HEADS = 4
B_NOPE = 64
B_ROPE = 32
B_QK = B_NOPE + B_ROPE
B_V = 64
B_QL = 256
B_KVL = 128
B_HEAD_PAD = 128
C_CH = 256
C_W = 31
C_PAD = 16
POOL = (2, 4, 8, 16)
D_G = 64
D_CH = D_G * len(POOL)
D_PAD = 8
N_EXP = 8
N_EXP_PAD = 128
LANES = 128
SUB = 8

O_AQ, O_AK, O_AV, O_BQ, O_BKV, O_KPE, O_C, O_D, O_G, O_END = (
    0, 512, 1024, 1536, 1792, 1920, 1952, 2464, 2720, 6816)
W_B_COLS = (O_KPE - O_BQ) + LANES

TM = 1024
TM_MERGE = 512
TQ = 512
TG = 1024
TR = 256
TF_DENSE = 1408
TF_MOE = 512
VMEM_LIMIT = 56 * 1024 * 1024


def _cparams(sem):
    return pltpu.CompilerParams(dimension_semantics=sem, vmem_limit_bytes=VMEM_LIMIT)


def _dot(a, b):
    return jnp.dot(a, b, preferred_element_type=F32)


def _dot_nt(a, b):
    return lax.dot_general(a, b, (((1,), (1,)), ((), ())), preferred_element_type=F32)


def _bf(x):
    return x.astype(BF16)


def _rms(x, g):
    return x * lax.rsqrt(jnp.mean(x * x, axis=-1, keepdims=True) + EPS) * g


def _seg_sum_sq(x, bd_ref):
    return _dot(_bf(x * x), bd_ref[...])


def _rope(x, c_ref, se_ref, so_ref):
    n = x.shape[-1]
    return (x * c_ref[...] + pltpu.roll(x, n - 1, 1) * se_ref[...]
            + pltpu.roll(x, 1, 1) * so_ref[...])


def _mod_row(i, tm, t_ctx, dec_seq):
    r = i * tm
    return jnp.where(r < t_ctx, 0, 1 + (r - t_ctx) // dec_seq)


def _layer(a, l):
    nd = a.ndim
    return pl.BlockSpec((None,) + a.shape[1:], lambda *_: (l,) + (0,) * (nd - 1),
                        pipeline_mode=pl.Buffered(1))


def _whole(a):
    nd = a.ndim
    return pl.BlockSpec(a.shape, lambda *_: (0,) * nd, pipeline_mode=pl.Buffered(1))


def _ada_kernel(cv_ref, w_ref, b_ref, o_ref):
    cv = cv_ref[...]
    s = cv * jax.nn.sigmoid(cv)
    o_ref[...] = _dot(_bf(s), _bf(w_ref[...])) + b_ref[...]


def _ada_all(cv, ada_w, ada_b):
    L, d, n = ada_w.shape
    tn = 1536
    return pl.pallas_call(
        _ada_kernel,
        grid=(L, n // tn),
        in_specs=[pl.BlockSpec((16, d), lambda l, j: (0, 0)),
                  pl.BlockSpec((None, d, tn), lambda l, j: (l, 0, j)),
                  pl.BlockSpec((None, 1, tn), lambda l, j: (l, 0, j))],
        out_specs=pl.BlockSpec((None, 16, tn), lambda l, j: (l, 0, j)),
        out_shape=jax.ShapeDtypeStruct((L, 16, n), F32),
        compiler_params=_cparams(("parallel", "parallel")),
        name="ada_mod",
    )(cv, ada_w, ada_b.reshape(L, 1, n))


REPACK_ROWS = 1024
REPACK_GROUPS = ((O_AQ, O_BQ - O_AQ, 0, 0), (O_BQ, O_C - O_BQ, 1, 0),
                 (O_C, O_G - O_C, 2, 0), (O_G, O_END - O_G, 3, 0))


def _repack_kernel(w_ref, wa_ref, wb_ref, wcd_ref, wg_ref):
    outs = (wa_ref, wb_ref, wcd_ref, wg_ref)
    r = pl.program_id(1)
    for step in range(pl.cdiv(O_END, REPACK_ROWS)):
        lo, hi = step * REPACK_ROWS, (step + 1) * REPACK_ROWS

        @pl.when(r == step)
        def _():
            for first, rows, dst, dst_first in REPACK_GROUPS:
                a, b = max(first, lo), min(first + rows, hi)
                if a < b:
                    outs[dst][dst_first + a - first:dst_first + b - first, :] = _bf(w_ref[a - lo:b - lo, :])

    @pl.when(r == 0)
    def _():
        wb_ref[O_C - O_BQ:W_B_COLS, :] = jnp.zeros((W_B_COLS - (O_C - O_BQ), D_MODEL), BF16)


def _repack(w_t):
    L, n, d = w_t.shape
    heights = (O_BQ - O_AQ, W_B_COLS, O_G - O_C, O_END - O_G)
    return pl.pallas_call(
        _repack_kernel,
        grid=(L, pl.cdiv(n, REPACK_ROWS)),
        in_specs=[pl.BlockSpec((None, REPACK_ROWS, d), lambda l, r: (l, r, 0))],
        out_specs=[pl.BlockSpec((None, h, d), lambda l, r: (l, 0, 0)) for h in heights],
        out_shape=[jax.ShapeDtypeStruct((L, h, d), BF16) for h in heights],
        compiler_params=_cparams(("parallel", "arbitrary")),
        name="repack_w_in",
    )(w_t)


def _ctx_lat_specs(tm, width, n_ctx_tiles):
    return [pl.BlockSpec((tm, width), lambda i: (jnp.minimum(i, n_ctx_tiles - 1), 0)),
            pl.BlockSpec((tm, width), lambda i: (jnp.maximum(i - n_ctx_tiles, 0), 0))]


def _prep_kernel(xc_ref, xl_ref, mod_ref, g_ref, h_ref, *, n_ctx_tiles):
    x = jnp.where(pl.program_id(0) < n_ctx_tiles, xc_ref[...], xl_ref[...])
    h_ref[...] = _pre_norm1(x, mod_ref, g_ref)


def _prep(x_ctx, x_lat, mod4, g, l, dec_seq):
    t_ctx = x_ctx.shape[0]
    T = t_ctx + x_lat.shape[0]
    nc = t_ctx // TM
    row = functools.partial(_mod_row, tm=TM, t_ctx=t_ctx, dec_seq=dec_seq)
    return pl.pallas_call(
        functools.partial(_prep_kernel, n_ctx_tiles=nc),
        grid=(T // TM,),
        in_specs=_ctx_lat_specs(TM, D_MODEL, nc) + [
            pl.BlockSpec((None, None, 1, 6 * D_MODEL), lambda i: (l, row(i), 0, 0)),
            _layer(g, l)],
        out_specs=pl.BlockSpec((TM, D_MODEL), lambda i: (i, 0)),
        out_shape=jax.ShapeDtypeStruct((T, D_MODEL), BF16),
        compiler_params=_cparams(("parallel",)),
        name="prep",
    )(x_ctx, x_lat, mod4, g)


def _proj_a_kernel(*refs, latent):
    if latent:
        (h_ref, w_ref, gq_ref, gk_ref, bd_ref, c_ref, se_ref, so_ref,
         q1_ref, q2_ref, k_ref, v_ref) = refs
    else:
        (h_ref, w_ref, gq_ref, gk_ref, bd_ref, _, _,
         q1_ref, q2_ref, k_ref, v_ref, nk_ref, nv_ref) = refs
    p = _dot_nt(h_ref[...], w_ref[...])
    n = 2 * A_HEADS * A_QK
    q = p[:, 0:n]
    k = p[:, n:2 * n]
    v = p[:, 2 * n:3 * n]
    q = q * lax.rsqrt(_seg_sum_sq(q, bd_ref) * (1.0 / A_QK) + EPS) * gq_ref[...]
    k = k * lax.rsqrt(_seg_sum_sq(k, bd_ref) * (1.0 / A_QK) + EPS) * gk_ref[...]
    if latent:
        q = _rope(q, c_ref, se_ref, so_ref)
        k = _rope(k, c_ref, se_ref, so_ref)
    else:
        nk_ref[...] = k.reshape(nk_ref.shape)
        nv_ref[...] = v.reshape(nv_ref.shape)
    q = q * (A_QK ** -0.5 * LOG2E)
    lane = lax.broadcasted_iota(I32, q.shape, 1)
    first = (lane % (2 * A_QK)) < A_QK
    q1_ref[...] = _bf(jnp.where(first, q, 0.0))
    q2_ref[...] = _bf(jnp.where(first, 0.0, q))
    k_ref[...] = _bf(k)
    v_ref[...] = _bf(v)


def _cache_out_spec(buf, l):
    nb, _, seq, w = buf.shape
    return pl.BlockSpec((TM // seq, None, seq, w), lambda i: (i, l, 0, 0))


def _proj_a(h1, w_a, gq, gk, bd, rope_tabs, l, row0, rows, latent, cache_out=None):
    n = 2 * A_HEADS * A_QK
    b0 = row0 // TM
    tab_blocks = rope_tabs[0].shape[0] // TM if latent else 1
    in_specs = [pl.BlockSpec((TM, D_MODEL), lambda i: (i + b0, 0)),
                _layer(w_a, l), _layer(gq, l), _layer(gk, l), _whole(bd)]
    args = [h1, w_a, gq, gk, bd]
    out_spec = pl.BlockSpec((TM, n), lambda i: (i, 0))
    out_shape = [jax.ShapeDtypeStruct((rows, n), BF16)] * 4
    out_specs = [out_spec] * 4
    aliases = {}
    if latent:
        in_specs += [pl.BlockSpec((TM, n), lambda i: (i % tab_blocks, 0))] * 3
        args += list(rope_tabs)
    else:
        for buf in cache_out:
            aliases[len(args)] = len(out_shape)
            in_specs.append(pl.BlockSpec(memory_space=pl.ANY))
            args.append(buf)
            out_shape.append(jax.ShapeDtypeStruct(buf.shape, buf.dtype))
            out_specs.append(_cache_out_spec(buf, l))
    return pl.pallas_call(
        functools.partial(_proj_a_kernel, latent=latent),
        grid=(rows // TM,),
        in_specs=in_specs, out_specs=out_specs, out_shape=out_shape,
        input_output_aliases=aliases,
        compiler_params=_cparams(("parallel",)),
        name="proj_a_lat" if latent else "proj_a_ctx",
    )(*args)


LOG2E = math.log2(math.e)


def _attend(qs, kss, vs):
    es, rs = [], []
    for q, ks in zip(qs, kss):
        scores = [_dot_nt(q, k) for k in ks]
        m = functools.reduce(jnp.maximum, [jnp.max(s, axis=-1, keepdims=True) for s in scores])
        e = [jnp.exp2(s - m) for s in scores]
        l = functools.reduce(lambda a, b: a + b, [jnp.sum(x, axis=-1, keepdims=True) for x in e])
        es.append([_bf(x) for x in e])
        rs.append(1.0 / l)
    o = None
    for p, v in enumerate(vs):
        stacked = es[0][p] if len(qs) == 1 else jnp.concatenate([e[p] for e in es], axis=0)
        t = _dot(stacked, v)
        o = t if o is None else o + t
    outs, r0 = [], 0
    for q, r in zip(qs, rs):
        outs.append(o[r0:r0 + q.shape[0]] * r)
        r0 += q.shape[0]
    return outs


def _attn_a_kernel(*refs, latent, lam_init):
    if latent:
        q1_ref, q2_ref, k_ref, v_ref, kc_ref, vc_ref, lam_ref, g_ref, o_ref = refs
    else:
        q1_ref, q2_ref, k_ref, v_ref, lam_ref, g_ref, o_ref = refs
    lm = lam_ref[...]
    lam = (jnp.exp(jnp.sum(lm[0:1] * lm[1:2], axis=-1, keepdims=True))
           - jnp.exp(jnp.sum(lm[2:3] * lm[3:4], axis=-1, keepdims=True)) + lam_init)
    for h in range(A_HEADS):
        sl = slice(A_V * h, A_V * (h + 1))
        ks = [k_ref[:, sl]]
        vs = [v_ref[:, sl]]
        if latent:
            ks.append(_bf(kc_ref[:, sl]))
            vs.append(_bf(vc_ref[:, sl]))
        tq = q1_ref.shape[0]
        oc, = _attend([jnp.concatenate([q1_ref[:, sl], q2_ref[:, sl]], axis=0)], [ks], vs)
        o = oc[0:tq] - lam * oc[tq:2 * tq]
        o = _rms(o, g_ref[...]) * (1.0 - lam_init)
        o_ref[:, sl] = _bf(o)


def _attn_a(q1, q2, k, v, cache_k, cache_v, a_lambda, g_sub, layer, seq, latent):
    rows, n = q1.shape
    nb = rows // seq
    lam_init = 0.8 - 0.6 * math.exp(-0.3 * layer)
    kern = functools.partial(_attn_a_kernel, latent=latent, lam_init=lam_init)
    if latent:
        nq = seq // TQ
        past = cache_k.shape[2]
        grid = (nb, nq)
        qs = pl.BlockSpec((TQ, n), lambda b, j: (b * nq + j, 0))
        kv = pl.BlockSpec((seq, n), lambda b, j: (b, 0))
        cs = pl.BlockSpec((None, None, past, n), lambda b, j: (b, layer, 0, 0))
        in_specs = [qs, qs, kv, kv, cs, cs, _layer(a_lambda, layer), _layer(g_sub, layer)]
        args = (q1, q2, k, v, cache_k, cache_v, a_lambda, g_sub)
        sem = ("parallel", "parallel")
        out_spec = qs
    else:
        grid = (nb,)
        bs = pl.BlockSpec((seq, n), lambda b: (b, 0))
        in_specs = [bs, bs, bs, bs, _layer(a_lambda, layer), _layer(g_sub, layer)]
        args = (q1, q2, k, v, a_lambda, g_sub)
        sem = ("parallel",)
        out_spec = bs
    return pl.pallas_call(
        kern, grid=grid, in_specs=in_specs, out_specs=out_spec,
        out_shape=jax.ShapeDtypeStruct((rows, n), BF16),
        compiler_params=_cparams(sem),
        name="attn_a_lat" if latent else "attn_a_ctx",
    )(*args)


def _mla_keys(ckv, kpe, wuk_ref, wuv_ref, gk_ref, bd_ref):
    cb = _bf(ckv)
    kn = _dot(cb, wuk_ref[...]) + jnp.concatenate([kpe] * B_HEADS, axis=1)
    k = kn * lax.rsqrt(_seg_sum_sq(kn, bd_ref) * (1.0 / B_QK) + EPS) * gk_ref[...]
    v = _dot(cb, wuv_ref[...])
    return k, v


def _proj_b_kernel(*refs, latent):
    if latent:
        (h_ref, w_ref, gql_ref, gkvl_ref, wuq_ref, wuk_ref, wuv_ref, gq_ref, gk_ref, bd_ref,
         c_ref, se_ref, so_ref, q_ref, k_ref, v_ref) = refs
    else:
        (h_ref, w_ref, gql_ref, gkvl_ref, wuq_ref, wuk_ref, wuv_ref, gq_ref, gk_ref, bd_ref, _, _,
         q_ref, k_ref, v_ref, nckv_ref, nkpe_ref) = refs
    p = _dot_nt(h_ref[...], w_ref[...])
    bq = p[:, 0:B_QL]
    bkv = p[:, B_QL:B_QL + B_KVL]
    kpe = p[:, B_QL + B_KVL:B_QL + B_KVL + LANES]
    q = _dot(_bf(_rms(bq, gql_ref[...])), wuq_ref[...])
    q = q * lax.rsqrt(_seg_sum_sq(q, bd_ref) * (1.0 / B_QK) + EPS) * gq_ref[...]
    ckv = _rms(bkv, gkvl_ref[...])
    k, v = _mla_keys(ckv, kpe, wuk_ref, wuv_ref, gk_ref, bd_ref)
    if latent:
        q = _rope(q, c_ref, se_ref, so_ref)
        k = _rope(k, c_ref, se_ref, so_ref)
    else:
        nckv_ref[...] = ckv.reshape(nckv_ref.shape)
        nkpe_ref[...] = kpe[:, 0:B_ROPE].reshape(nkpe_ref.shape)
    q_ref[...] = _bf(q * (B_QK ** -0.5 * LOG2E))
    k_ref[...] = _bf(k)
    v_ref[...] = _bf(v)


def _proj_b(h1, w_b, gql, gkvl, wuq, wuk, wuv, gq, gk, bd, rope_tabs, l, row0, rows, latent,
            cache_out=None):
    n = B_HEADS * B_HEAD_PAD
    nv = B_HEADS * B_V
    b0 = row0 // TM
    tab_blocks = rope_tabs[0].shape[0] // TM if latent else 1
    in_specs = [pl.BlockSpec((TM, D_MODEL), lambda i: (i + b0, 0))] + [
        _layer(a, l) for a in (w_b, gql, gkvl, wuq, wuk, wuv, gq, gk)] + [_whole(bd)]
    args = [h1, w_b, gql, gkvl, wuq, wuk, wuv, gq, gk, bd]
    row = lambda w: pl.BlockSpec((TM, w), lambda i: (i, 0))
    out_shape = [jax.ShapeDtypeStruct((rows, n), BF16), jax.ShapeDtypeStruct((rows, n), BF16),
                 jax.ShapeDtypeStruct((rows, nv), BF16)]
    out_specs = [row(n), row(n), row(nv)]
    aliases = {}
    if latent:
        in_specs += [pl.BlockSpec((TM, n), lambda i: (i % tab_blocks, 0))] * 3
        args += list(rope_tabs)
    else:
        for buf in cache_out:
            aliases[len(args)] = len(out_shape)
            in_specs.append(pl.BlockSpec(memory_space=pl.ANY))
            args.append(buf)
            out_shape.append(jax.ShapeDtypeStruct(buf.shape, buf.dtype))
            out_specs.append(_cache_out_spec(buf, l))
    return pl.pallas_call(
        functools.partial(_proj_b_kernel, latent=latent),
        grid=(rows // TM,),
        in_specs=in_specs, out_specs=out_specs, out_shape=out_shape,
        input_output_aliases=aliases,
        compiler_params=_cparams(("parallel",)),
        name="proj_b_lat" if latent else "proj_b_ctx",
    )(*args)


def _cache_b_kernel(ckv_ref, kpe_ref, wuk_ref, wuv_ref, gk_ref, bd_ref, k_ref, v_ref):
    k, v = _mla_keys(ckv_ref[...], kpe_ref[...], wuk_ref, wuv_ref, gk_ref, bd_ref)
    k_ref[...] = _bf(k)
    v_ref[...] = _bf(v)


def _cache_b(ckv, kpe_pad, wuk, wuv, gk, bd):
    db, L, past, _ = ckv.shape
    n = B_HEADS * B_HEAD_PAD
    nv = B_HEADS * B_V
    blk = lambda w: pl.BlockSpec((None, None, past, w), lambda l, b: (b, l, 0, 0))
    wl = lambda a: pl.BlockSpec((None,) + a.shape[1:], lambda l, b: (l, 0, 0))
    return pl.pallas_call(
        _cache_b_kernel,
        grid=(L, db),
        in_specs=[blk(B_KVL), blk(LANES), wl(wuk), wl(wuv), wl(gk),
                  pl.BlockSpec(bd.shape, lambda l, b: (0, 0))],
        out_specs=[blk(n), blk(nv)],
        out_shape=[jax.ShapeDtypeStruct((db, L, past, n), BF16),
                   jax.ShapeDtypeStruct((db, L, past, nv), BF16)],
        compiler_params=_cparams(("parallel", "parallel")),
        name="cache_b_expand",
    )(ckv, kpe_pad, wuk, wuv, gk, bd)


def _attn_b_kernel(*refs, latent):
    if latent:
        q_ref, k_ref, v_ref, kc_ref, vc_ref, o_ref = refs
    else:
        q_ref, k_ref, v_ref, o_ref = refs
    lane = lax.broadcasted_iota(I32, (q_ref.shape[0], 2 * B_V), 1)
    for hp in range(B_HEADS // 2):
        vsl = slice(2 * B_V * hp, 2 * B_V * (hp + 1))
        vs = [v_ref[:, vsl]] + ([vc_ref[:, vsl]] if latent else [])
        qs, kss = [], []
        for h in (2 * hp, 2 * hp + 1):
            sl = slice(B_HEAD_PAD * h, B_HEAD_PAD * (h + 1))
            qs.append(q_ref[:, sl])
            kss.append([k_ref[:, sl]] + ([kc_ref[:, sl]] if latent else []))
        outs = _attend(qs, kss, vs)
        o_ref[:, vsl] = _bf(jnp.where(lane < B_V, outs[0], outs[1]))


def _attn_b(q, k, v, cache_k, cache_v, layer, seq, latent):
    rows, n = q.shape
    nv = v.shape[1]
    nb = rows // seq
    kern = functools.partial(_attn_b_kernel, latent=latent)
    if latent:
        nq = seq // TQ
        past = cache_k.shape[2]
        grid = (nb, nq)
        in_specs = [pl.BlockSpec((TQ, n), lambda b, j: (b * nq + j, 0)),
                    pl.BlockSpec((seq, n), lambda b, j: (b, 0)),
                    pl.BlockSpec((seq, nv), lambda b, j: (b, 0)),
                    pl.BlockSpec((None, None, past, n), lambda b, j: (b, layer, 0, 0)),
                    pl.BlockSpec((None, None, past, nv), lambda b, j: (b, layer, 0, 0))]
        args = (q, k, v, cache_k, cache_v)
        out_spec = pl.BlockSpec((TQ, nv), lambda b, j: (b * nq + j, 0))
        sem = ("parallel", "parallel")
    else:
        grid = (nb,)
        in_specs = [pl.BlockSpec((seq, n), lambda b: (b, 0)),
                    pl.BlockSpec((seq, n), lambda b: (b, 0)),
                    pl.BlockSpec((seq, nv), lambda b: (b, 0))]
        args = (q, k, v)
        out_spec = pl.BlockSpec((seq, nv), lambda b: (b, 0))
        sem = ("parallel",)
    return pl.pallas_call(
        kern, grid=grid, in_specs=in_specs, out_specs=out_spec,
        out_shape=jax.ShapeDtypeStruct((rows, nv), BF16),
        compiler_params=_cparams(sem),
        name="attn_b_lat" if latent else "attn_b_ctx",
    )(*args)


CONV_CHUNK = 64


def _mix_cd_kernel(h_ref, w_ref, dw_ref, dwb_ref, lng_ref, lnb_ref, bdd_ref, dsc_ref, pm_ref, pcnt_ref,
                   oc_ref, od_ref, gpad, dpad, gsh, dsh, *, seq):
    p = _dot_nt(h_ref[...], w_ref[...])
    glu = p[:, 0:C_CH] * jax.nn.sigmoid(p[:, C_CH:2 * C_CH])
    gpad[0:C_PAD, :] = jnp.zeros((C_PAD, C_CH), F32)
    gpad[C_PAD + seq:2 * C_PAD + seq, :] = jnp.zeros((C_PAD, C_CH), F32)
    gpad[C_PAD:C_PAD + seq, :] = glu
    half = C_W // 2
    span = seq + 2 * C_PAD - SUB
    for r in range(1, SUB):
        gsh[(r - 1) * span:r * span, :] = gpad[r:r + span, :]
    for c0 in range(0, seq, CONV_CHUNK):
        acc = jnp.zeros((CONV_CHUNK, C_CH), F32) + dwb_ref[...]
        for j in range(C_W):
            s = c0 + C_PAD - half + j
            r = s % SUB
            if r == 0:
                tap = gpad[s:s + CONV_CHUNK, :]
            else:
                tap = gsh[(r - 1) * span + s - r:(r - 1) * span + s - r + CONV_CHUNK, :]
            acc = acc + tap * dw_ref[j:j + 1, :]
        mu = jnp.mean(acc, axis=-1, keepdims=True)
        xc = acc - mu
        y = xc * lax.rsqrt(jnp.mean(xc * xc, axis=-1, keepdims=True) + EPS)
        y = y * lng_ref[...] + lnb_ref[...]
        oc_ref[c0:c0 + CONV_CHUNK, :] = _bf(y * jax.nn.sigmoid(y))
    d = p[:, 2 * C_CH:2 * C_CH + D_CH]
    dpad[0:D_PAD, :] = jnp.zeros((D_PAD, D_CH), F32)
    dpad[D_PAD + seq:2 * D_PAD + seq, :] = jnp.zeros((D_PAD, D_CH), F32)
    dpad[D_PAD:D_PAD + seq, :] = d
    dspan = seq + 2 * D_PAD - SUB
    for r in range(1, SUB):
        dsh[(r - 1) * dspan:r * dspan, :] = dpad[r:r + dspan, :]
    for c0 in range(0, seq, CONV_CHUNK):
        acc = jnp.zeros((CONV_CHUNK, D_CH), F32)
        for j in range(2 * D_PAD):
            s = c0 + j
            r = s % SUB
            if r == 0:
                tap = dpad[s:s + CONV_CHUNK, :]
            else:
                tap = dsh[(r - 1) * dspan + s - r:(r - 1) * dspan + s - r + CONV_CHUNK, :]
            acc = acc + tap * pm_ref[j:j + 1, :]
        pooled = acc / pcnt_ref[c0:c0 + CONV_CHUNK, :]
        diff = pooled - dpad[c0 + D_PAD:c0 + D_PAD + CONV_CHUNK, :]
        od_ref[c0:c0 + CONV_CHUNK, :] = _bf(_dot(_bf(diff), bdd_ref[...]) * dsc_ref[...])


def _mix_cd(h1, w_cd, dw, dwb, lng, lnb, bdd, dsc, pmask, l, row0, rows, seq):
    b0 = row0 // seq
    pcnt = _pool_counts(seq)
    return pl.pallas_call(
        functools.partial(_mix_cd_kernel, seq=seq),
        grid=(rows // seq,),
        in_specs=[pl.BlockSpec((seq, D_MODEL), lambda b: (b + b0, 0))] + [
            _layer(a, l) for a in (w_cd, dw, dwb, lng, lnb, bdd, dsc)] + [_whole(pmask), _whole(pcnt)],
        out_specs=[pl.BlockSpec((seq, C_CH), lambda b: (b, 0)),
                   pl.BlockSpec((seq, D_CH), lambda b: (b, 0))],
        out_shape=[jax.ShapeDtypeStruct((rows, C_CH), BF16),
                   jax.ShapeDtypeStruct((rows, D_CH), BF16)],
        scratch_shapes=[pltpu.VMEM((seq + 2 * C_PAD, C_CH), F32),
                        pltpu.VMEM((seq + 2 * D_PAD, D_CH), F32),
                        pltpu.VMEM(((SUB - 1) * (seq + 2 * C_PAD - SUB), C_CH), F32),
                        pltpu.VMEM(((SUB - 1) * (seq + 2 * D_PAD - SUB), D_CH), F32)],
        compiler_params=_cparams(("parallel",)),
        name="mix_cd_%d" % seq,
    )(h1, w_cd, dw, dwb, lng, lnb, bdd, dsc, pmask, pcnt)


def _merge_kernel(*refs, moe, split_x, n_ctx_tiles):
    is_ctx = pl.program_id(0) < n_ctx_tiles
    if split_x:
        x_in = jnp.where(is_ctx, refs[0][...], refs[1][...])
        refs = refs[2:]
    else:
        x_in = refs[0][...]
        refs = refs[1:]
    (h_ref, oac_ref, oal_ref, obc_ref, obl_ref, occ_ref, ocl_ref, odc_ref, odl_ref,
     mod_ref, n2g_ref, wg_ref, wa_ref, wb_ref, wc_ref, wd_ref, wo_ref) = refs[:17]
    if moe:
        rhi_ref, rlo_ref, x1_ref, h2t_ref, ridx_ref, rw_ref, sel_ref = refs[17:]
    else:
        x1_ref, h2_ref = refs[17:]
    h = h_ref[...]
    acc = None
    for i, (c_ref, l_ref, w_ref) in enumerate(((oac_ref, oal_ref, wa_ref), (obc_ref, obl_ref, wb_ref),
                                               (occ_ref, ocl_ref, wc_ref), (odc_ref, odl_ref, wd_ref))):
        gate = jax.nn.sigmoid(_dot_nt(h, wg_ref[i * D_MODEL:(i + 1) * D_MODEL, :]))
        o = jnp.where(is_ctx, c_ref[...], l_ref[...])
        t = gate * _dot(o, w_ref[...])
        acc = t if acc is None else acc + t
    y = _dot(_bf(acc), wo_ref[...])
    x1 = x_in + mod_ref[:, 2 * D_MODEL:3 * D_MODEL] * y
    x1_ref[...] = x1
    h2 = (_rms(x1, n2g_ref[...]) * (1.0 + mod_ref[:, 4 * D_MODEL:5 * D_MODEL])
          + mod_ref[:, 3 * D_MODEL:4 * D_MODEL])
    if not moe:
        h2_ref[...] = _bf(h2)
        return
    tm = h2.shape[0]
    for j in range(SUB):
        h2t_ref[pl.ds(j, tm, stride=SUB), :] = h2[:, LANES * j:LANES * (j + 1)]
    hi = _bf(h2)
    lo = _bf(h2 - hi.astype(F32))
    logits = _dot(hi, rhi_ref[...]) + _dot(lo, rhi_ref[...]) + _dot(hi, rlo_ref[...])
    lane = lax.broadcasted_iota(I32, logits.shape, 1)
    lanef = lane.astype(F32)
    neg = jnp.float32(-jnp.inf)
    lg = jnp.where(lane < N_EXP, logits, neg)
    m0 = jnp.max(lg, axis=-1, keepdims=True)
    i0 = jnp.min(jnp.where(lg == m0, lanef, float(N_EXP_PAD)), axis=-1, keepdims=True)
    sel0 = lanef == i0
    lg1 = jnp.where(sel0, neg, lg)
    m1 = jnp.max(lg1, axis=-1, keepdims=True)
    i1 = jnp.min(jnp.where(lg1 == m1, lanef, float(N_EXP_PAD)), axis=-1, keepdims=True)
    sel1 = lanef == i1
    e = jnp.exp(m1 - m0)
    w0 = 1.0 / (1.0 + e)
    w1 = e / (1.0 + e)
    ridx_ref[...] = jnp.where(lane == 0, i0, jnp.where(lane == 1, i1, 0.0)).astype(I32)
    rw_ref[...] = jnp.where(lane == 0, w0, jnp.where(lane == 1, w1, 0.0))
    sel_ref[...] = jnp.where(sel0 | sel1, 1.0, 0.0).astype(BF16)


def _merge(x, h1, branches, mod4, n2g, wg, wa, wb, wc, wd, wo, router, l, j, t_ctx, dec_seq):
    T = h1.shape[0]
    moe = router is not None
    split_x = isinstance(x, tuple)
    tm = TM_MERGE
    nc = t_ctx // tm
    row = functools.partial(_mod_row, tm=tm, t_ctx=t_ctx, dec_seq=dec_seq)
    rowspec = lambda w: pl.BlockSpec((tm, w), lambda i: (i, 0))
    if split_x:
        in_specs = _ctx_lat_specs(tm, D_MODEL, nc) + [rowspec(D_MODEL)]
        args = [x[0], x[1], h1]
    else:
        in_specs = [rowspec(D_MODEL), rowspec(D_MODEL)]
        args = [x, h1]
    for oc, ol in branches:
        in_specs += _ctx_lat_specs(tm, oc.shape[1], nc)
        args += [oc, ol]
    in_specs += [pl.BlockSpec((None, None, 1, 6 * D_MODEL), lambda i: (l, row(i), 0, 0))]
    in_specs += [_layer(a, l) for a in (n2g, wg, wa, wb, wc, wd, wo)]
    args += [mod4, n2g, wg, wa, wb, wc, wd, wo]
    out_shape = [jax.ShapeDtypeStruct((T, D_MODEL), F32)]
    out_specs = [rowspec(D_MODEL)]
    if moe:
        in_specs += [_layer(router[0], j), _layer(router[1], j)]
        args += list(router)
        out_shape += [jax.ShapeDtypeStruct((T * SUB, LANES), F32),
                      jax.ShapeDtypeStruct((T, N_EXP_PAD), I32),
                      jax.ShapeDtypeStruct((T, N_EXP_PAD), F32),
                      jax.ShapeDtypeStruct((T, N_EXP_PAD), BF16)]
        out_specs += [pl.BlockSpec((tm * SUB, LANES), lambda i: (i, 0)),
                      rowspec(N_EXP_PAD), rowspec(N_EXP_PAD), rowspec(N_EXP_PAD)]
    else:
        out_shape += [jax.ShapeDtypeStruct((T, D_MODEL), BF16)]
        out_specs += [rowspec(D_MODEL)]
    return pl.pallas_call(
        functools.partial(_merge_kernel, moe=moe, split_x=split_x, n_ctx_tiles=nc),
        grid=(T // tm,),
        in_specs=in_specs, out_specs=out_specs, out_shape=out_shape,
        compiler_params=_cparams(("parallel",)),
        name="merge_moe" if moe else "merge",
    )(*args)


def _swiglu_step(x, w1_ref, w3_ref, w2_ref, acc_ref, f):
    a = _dot(x, _bf(w1_ref[...]))
    b = _dot(x, _bf(w3_ref[...]))
    t = _dot(_bf(a * jax.nn.sigmoid(a) * b), _bf(w2_ref[...]))

    @pl.when(f == 0)
    def _():
        acc_ref[...] = t

    @pl.when(f > 0)
    def _():
        acc_ref[...] += t


def _pre_norm1(x, mod_ref, g_ref):
    return _bf(_rms(x, g_ref[...]) * (1.0 + mod_ref[:, D_MODEL:2 * D_MODEL]) + mod_ref[:, 0:D_MODEL])


def _ffn_dense_kernel(x_ref, w1_ref, w3_ref, w2_ref, x1_ref, mod_ref, modn_ref, gn_ref,
                      o_ref, hn_ref, acc_ref):
    f = pl.program_id(1)
    _swiglu_step(x_ref[...], w1_ref, w3_ref, w2_ref, acc_ref, f)

    @pl.when(f == pl.num_programs(1) - 1)
    def _():
        x2 = x1_ref[...] + mod_ref[:, 5 * D_MODEL:6 * D_MODEL] * acc_ref[...]
        o_ref[...] = x2
        hn_ref[...] = _pre_norm1(x2, modn_ref, gn_ref)


def _ffn_dense(h2, w1, w3, w2, x1, mod4, n1g, l, j, t_ctx, dec_seq):
    T = h2.shape[0]
    nf = w1.shape[2] // TF_DENSE
    row = functools.partial(_mod_row, tm=TG, t_ctx=t_ctx, dec_seq=dec_seq)
    rows = pl.BlockSpec((TG, D_MODEL), lambda i, f: (i, 0))
    return pl.pallas_call(
        _ffn_dense_kernel,
        grid=(T // TG, nf),
        in_specs=[rows,
                  pl.BlockSpec((None, D_MODEL, TF_DENSE), lambda i, f: (j, 0, f)),
                  pl.BlockSpec((None, D_MODEL, TF_DENSE), lambda i, f: (j, 0, f)),
                  pl.BlockSpec((None, TF_DENSE, D_MODEL), lambda i, f: (j, f, 0)),
                  rows,
                  pl.BlockSpec((None, None, 1, 6 * D_MODEL), lambda i, f: (l, row(i), 0, 0)),
                  pl.BlockSpec((None, None, 1, 6 * D_MODEL), lambda i, f: (l + 1, row(i), 0, 0)),
                  _layer(n1g, l + 1)],
        out_specs=[rows, rows],
        out_shape=[jax.ShapeDtypeStruct((T, D_MODEL), F32), jax.ShapeDtypeStruct((T, D_MODEL), BF16)],
        scratch_shapes=[pltpu.VMEM((TG, D_MODEL), F32)],
        compiler_params=_cparams(("parallel", "arbitrary")),
        name="ffn_dense",
    )(h2, w1, w3, w2, x1, mod4, mod4, n1g)


def _ffn_experts_kernel(te_ref, na_ref, x_ref, w1_ref, w3_ref, w2_ref, o_ref, xb_ref, acc_ref):
    i = pl.program_id(0)
    f = pl.program_id(1)
    active = i < na_ref[0]

    @pl.when(jnp.logical_and(active, f == 0))
    def _():
        for j in range(SUB):
            xb_ref[:, LANES * j:LANES * (j + 1)] = _bf(x_ref[pl.ds(j, TG, stride=SUB), :])

    @pl.when(active)
    def _():
        _swiglu_step(xb_ref[...], w1_ref, w3_ref, w2_ref, acc_ref, f)

    @pl.when(jnp.logical_and(active, f == pl.num_programs(1) - 1))
    def _():
        for j in range(SUB):
            o_ref[pl.ds(j, TG, stride=SUB), :] = acc_ref[:, LANES * j:LANES * (j + 1)]

    @pl.when(jnp.logical_and(jnp.logical_not(active), f == 0))
    def _():
        o_ref[...] = jnp.zeros(o_ref.shape, F32)


def _ffn_experts(xs, w1, w3, w2, tile_e, n_active, j):
    rows = xs.shape[0] // SUB
    nf = w1.shape[3] // TF_MOE

    def fidx(i, f, na):
        return jnp.where(i < na[0], f, nf - 1)

    def xidx(i, na):
        return jnp.minimum(i, na[0] - 1)

    return pl.pallas_call(
        _ffn_experts_kernel,
        grid_spec=pltpu.PrefetchScalarGridSpec(
            num_scalar_prefetch=2, grid=(rows // TG, nf),
            in_specs=[pl.BlockSpec((TG * SUB, LANES), lambda i, f, te, na: (xidx(i, na), 0)),
                      pl.BlockSpec((None, None, D_MODEL, TF_MOE),
                                   lambda i, f, te, na: (j, te[i], 0, fidx(i, f, na))),
                      pl.BlockSpec((None, None, D_MODEL, TF_MOE),
                                   lambda i, f, te, na: (j, te[i], 0, fidx(i, f, na))),
                      pl.BlockSpec((None, None, TF_MOE, D_MODEL),
                                   lambda i, f, te, na: (j, te[i], fidx(i, f, na), 0))],
            out_specs=pl.BlockSpec((TG * SUB, LANES), lambda i, f, te, na: (i, 0)),
            scratch_shapes=[pltpu.VMEM((TG, D_MODEL), BF16), pltpu.VMEM((TG, D_MODEL), F32)]),
        out_shape=jax.ShapeDtypeStruct((rows * SUB, LANES), F32),
        compiler_params=_cparams(("parallel", "arbitrary")),
        name="ffn_experts",
    )(tile_e, n_active, xs, w1, w3, w2)


def _rank_kernel(sel_ref, tri_ref, rank_ref, cnt_ref, carry):
    i = pl.program_id(0)

    @pl.when(i == 0)
    def _():
        carry[...] = jnp.zeros(carry.shape, F32)

    s = sel_ref[...]
    rank_ref[...] = _dot(tri_ref[...], s) + carry[...]
    carry[...] += jnp.sum(s.astype(F32), axis=0, keepdims=True)
    cnt_ref[...] = jnp.broadcast_to(carry[...], cnt_ref.shape)


def _rank(sel, tri):
    T = sel.shape[0]
    tr = tri.shape[0]
    return pl.pallas_call(
        _rank_kernel,
        grid=(T // tr,),
        in_specs=[pl.BlockSpec((tr, N_EXP_PAD), lambda i: (i, 0)),
                  pl.BlockSpec((tr, tr), lambda i: (0, 0))],
        out_specs=[pl.BlockSpec((tr, N_EXP_PAD), lambda i: (i, 0)),
                   pl.BlockSpec((8, N_EXP_PAD), lambda i: (0, 0))],
        out_shape=[jax.ShapeDtypeStruct((T, N_EXP_PAD), F32),
                   jax.ShapeDtypeStruct((8, N_EXP_PAD), F32)],
        scratch_shapes=[pltpu.VMEM((1, N_EXP_PAD), F32)],
        compiler_params=_cparams(("arbitrary",)),
        name="route_rank",
    )(sel, tri)


TILE_ROWS = TR * SUB


def _token_tile(ref, tok):
    return ref.at[pl.ds(pl.multiple_of(tok * SUB, SUB), SUB)]


def _dispatch_kernel(pos0_ref, pos1_ref, ends_ref, padded_ref, h_ref, xs_hbm, stage, zbuf, sem, zsem):
    i = pl.program_id(0)

    @pl.when(i == 0)
    def _():
        zbuf[...] = jnp.zeros(zbuf.shape, F32)

        def fill_tile(first_slot):
            start = pl.multiple_of(first_slot * SUB, TG * SUB)
            return pltpu.make_async_copy(zbuf, xs_hbm.at[pl.ds(start, TG * SUB)], zsem.at[0])

        n_slots = xs_hbm.shape[0] // SUB
        for wait in (False, True):
            for e in range(N_EXP):
                for cond, first in ((padded_ref[e] > 0, ends_ref[e] - TG),
                                    (ends_ref[N_EXP - 1] + e * TG < n_slots, ends_ref[N_EXP - 1] + e * TG)):
                    @pl.when(cond)
                    def _():
                        fill = fill_tile(first)
                        if wait:
                            fill.wait()
                        else:
                            fill.start()

    slot = i % 2
    base = pl.multiple_of(slot * TILE_ROWS, TILE_ROWS)
    stage[pl.ds(base, TILE_ROWS), :] = h_ref[...]

    def issue(r, c):
        t = i * TR + r
        src = _token_tile(stage, slot * TR + r)
        pltpu.make_async_copy(src, _token_tile(xs_hbm, pos0_ref[t]), sem.at[slot]).start()
        pltpu.make_async_copy(src, _token_tile(xs_hbm, pos1_ref[t]), sem.at[slot]).start(priority=1)
        return c
    lax.fori_loop(0, TR, issue, 0, unroll=8)

    def wait_slot(s):
        b = pl.multiple_of(s * TILE_ROWS, TILE_ROWS)
        for _ in range(2):
            pltpu.make_async_copy(stage.at[pl.ds(b, TILE_ROWS)], xs_hbm.at[pl.ds(0, TILE_ROWS)],
                                  sem.at[s]).wait()

    @pl.when(i > 0)
    def _():
        wait_slot(1 - slot)

    @pl.when(i == pl.num_programs(0) - 1)
    def _():
        wait_slot(slot)


def _dispatch(pos0, pos1, ends, padded, h2t, n_rows):
    T = h2t.shape[0] // SUB
    assert n_rows - 2 * T <= N_EXP * TG
    return pl.pallas_call(
        _dispatch_kernel,
        grid_spec=pltpu.PrefetchScalarGridSpec(
            num_scalar_prefetch=4, grid=(T // TR,),
            in_specs=[pl.BlockSpec((TILE_ROWS, LANES), lambda i, *_: (i, 0))],
            out_specs=pl.BlockSpec(memory_space=pl.ANY),
            scratch_shapes=[pltpu.VMEM((2 * TILE_ROWS, LANES), F32),
                            pltpu.VMEM((TG * SUB, LANES), F32),
                            pltpu.SemaphoreType.DMA((2,)),
                            pltpu.SemaphoreType.DMA((1,))]),
        out_shape=jax.ShapeDtypeStruct((n_rows * SUB, LANES), F32),
        compiler_params=_cparams(("arbitrary",)),
        name="moe_dispatch",
    )(pos0, pos1, ends, padded, h2t)


def _combine_kernel(pos0_ref, pos1_ref, y_hbm, x1_ref, rw_ref, mod_ref, *rest, emit_next, n_ctx_tiles):
    if emit_next:
        modn_ref, gn_ref, o_ref, hn_ref, buf0, buf1, sem0, sem1 = rest
    else:
        oc_ref, ol_ref, buf0, buf1, sem0, sem1 = rest
    i = pl.program_id(0)
    nt = pl.num_programs(0)

    def issue(tile, slot):
        def body(r, c):
            t = tile * TR + r
            pltpu.make_async_copy(_token_tile(y_hbm, pos0_ref[t]), _token_tile(buf0, slot * TR + r),
                                  sem0.at[slot]).start()
            pltpu.make_async_copy(_token_tile(y_hbm, pos1_ref[t]), _token_tile(buf1, slot * TR + r),
                                  sem1.at[slot]).start(priority=1)
            return c
        lax.fori_loop(0, TR, body, 0, unroll=8)

    @pl.when(i == 0)
    def _():
        issue(0, 0)

    @pl.when(i + 1 < nt)
    def _():
        issue(i + 1, (i + 1) % 2)

    slot = i % 2
    base = pl.multiple_of(slot * TILE_ROWS, TILE_ROWS)
    for buf, sem in ((buf0, sem0), (buf1, sem1)):
        pltpu.make_async_copy(y_hbm.at[pl.ds(0, TILE_ROWS)], buf.at[pl.ds(base, TILE_ROWS)],
                              sem.at[slot]).wait()
    rw = rw_ref[...]
    w0 = rw[:, 0:1]
    w1 = rw[:, 1:2]

    def write(o_ref):
        for j in range(SUB):
            sl = slice(LANES * j, LANES * (j + 1))
            f = (w0 * buf0[pl.ds(base + j, TR, stride=SUB), :]
                 + w1 * buf1[pl.ds(base + j, TR, stride=SUB), :])
            o_ref[:, sl] = x1_ref[:, sl] + mod_ref[:, 5 * D_MODEL + LANES * j:5 * D_MODEL + LANES * (j + 1)] * f

    if emit_next:
        write(o_ref)
        hn_ref[...] = _pre_norm1(o_ref[...], modn_ref, gn_ref)
    else:
        @pl.when(i < n_ctx_tiles)
        def _():
            write(oc_ref)

        @pl.when(i >= n_ctx_tiles)
        def _():
            write(ol_ref)


def _combine(pos0, pos1, ys, x1, rw, mod4, n1g, l, t_ctx, dec_seq, emit_next):
    T = x1.shape[0]
    nc = t_ctx // TR
    row = functools.partial(_mod_row, tm=TR, t_ctx=t_ctx, dec_seq=dec_seq)
    rows = pl.BlockSpec((TR, D_MODEL), lambda i, p0, p1: (i, 0))
    in_specs = [pl.BlockSpec(memory_space=pl.ANY), rows,
                pl.BlockSpec((TR, N_EXP_PAD), lambda i, p0, p1: (i, 0)),
                pl.BlockSpec((None, None, 1, 6 * D_MODEL), lambda i, p0, p1: (l, row(i), 0, 0))]
    args = [pos0, pos1, ys, x1, rw, mod4]
    if emit_next:
        in_specs += [pl.BlockSpec((None, None, 1, 6 * D_MODEL), lambda i, p0, p1: (l + 1, row(i), 0, 0)),
                     _layer(n1g, l + 1)]
        args += [mod4, n1g]
        out_specs = [rows, rows]
        out_shape = [jax.ShapeDtypeStruct((T, D_MODEL), F32), jax.ShapeDtypeStruct((T, D_MODEL), BF16)]
    else:
        out_specs = [pl.BlockSpec((TR, D_MODEL), lambda i, p0, p1: (jnp.minimum(i, nc - 1), 0)),
                     pl.BlockSpec((TR, D_MODEL), lambda i, p0, p1: (jnp.maximum(i - nc, 0), 0))]
        out_shape = [jax.ShapeDtypeStruct((t_ctx, D_MODEL), F32),
                     jax.ShapeDtypeStruct((T - t_ctx, D_MODEL), F32)]
    return pl.pallas_call(
        functools.partial(_combine_kernel, emit_next=emit_next, n_ctx_tiles=nc),
        grid_spec=pltpu.PrefetchScalarGridSpec(
            num_scalar_prefetch=2, grid=(T // TR,),
            in_specs=in_specs, out_specs=out_specs,
            scratch_shapes=[pltpu.VMEM((2 * TILE_ROWS, LANES), F32), pltpu.VMEM((2 * TILE_ROWS, LANES), F32),
                            pltpu.SemaphoreType.DMA((2,)), pltpu.SemaphoreType.DMA((2,))]),
        out_shape=out_shape,
        compiler_params=_cparams(("arbitrary",)),
        name="moe_combine",
    )(*args)


def _block_ones(n, seg):
    idx = np.arange(n) // seg
    return jnp.asarray((idx[:, None] == idx[None, :]).astype(np.float32), dtype=BF16)


def _rope_tables(seq, head_w, rot_dim, n_heads):
    rows = seq // GRID_W
    nf = rot_dim // 4
    freqs = ROPE_THETA ** (-np.arange(nf, dtype=np.float64) / nf)
    row = np.repeat(np.arange(rows, dtype=np.float64), GRID_W)
    col = np.tile(np.arange(GRID_W, dtype=np.float64), rows)
    ang = np.concatenate([row[:, None] * freqs, col[:, None] * freqs], axis=-1)
    ang = np.repeat(ang, 2, axis=-1)
    even = (np.arange(rot_dim) % 2 == 0)[None, :]
    c = np.ones((seq, head_w))
    se = np.zeros((seq, head_w))
    so = np.zeros((seq, head_w))
    c[:, :rot_dim] = np.cos(ang)
    se[:, :rot_dim] = np.where(even, -np.sin(ang), 0.0)
    so[:, :rot_dim] = np.where(even, 0.0, np.sin(ang))
    return tuple(jnp.asarray(np.tile(t, (1, n_heads)), dtype=F32) for t in (c, se, so))


def _pool_mask():
    m = np.zeros((2 * D_PAD, D_CH), np.float32)
    for gi, w in enumerate(POOL):
        left = w // 2
        right = w - 1 - left
        for off in range(-left, right + 1):
            m[off + D_PAD, gi * D_G:(gi + 1) * D_G] = 1.0
    return jnp.asarray(m)


def _pool_counts(seq):
    t = np.arange(seq)
    m = np.zeros((seq, D_CH), np.float32)
    for gi, w in enumerate(POOL):
        left = w // 2
        right = w - 1 - left
        m[:, gi * D_G:(gi + 1) * D_G] = (np.minimum(t + right + 1, seq) - np.maximum(t - left, 0))[:, None]
    return jnp.asarray(m)


def _b_heads(nope, rope):
    ref = nope if nope is not None else rope
    lead = ref.shape[:-1]
    z = lambda w: jnp.zeros(lead + (w,), ref.dtype)
    parts = [rope if rope is not None else z(B_ROPE), nope if nope is not None else z(B_NOPE),
             z(B_HEAD_PAD - B_QK)]
    out = jnp.concatenate(parts, axis=-1)
    return out.reshape(lead[:-1] + (lead[-1] * B_HEAD_PAD,))


def kernel(x_prompt, x_sample, cache_a_k, cache_a_v, cache_b_ckv, cache_b_kpe, c, c_ctx, ada_w, ada_b, norm1_g, norm2_g, w_in, a_q_norm, a_k_norm, a_lambda, a_sub_norm, a_w_o, b_q_lora_norm, b_kv_lora_norm, b_w_uq, b_w_ukv, b_q_norm, b_k_norm, b_w_o, c_dw, c_dw_b, c_ln_g, c_ln_b, c_w_o, d_w_group, d_scale, d_w_o, w_out, ffn_w1, ffn_w3, ffn_w2, moe_router, moe_w1, moe_w3, moe_w2):
    nb, seq, _ = x_prompt.shape
    db, dseq, _ = x_sample.shape
    L = w_in.shape[0]
    past = cache_a_k.shape[2]
    t_ctx = nb * seq
    t_lat = db * dseq
    T = t_ctx + t_lat
    na = 2 * A_HEADS * A_QK
    assert t_ctx % TG == 0 and dseq % TG == 0 and seq % TR == 0 and db + 1 <= 16

    x = (x_prompt.reshape(t_ctx, D_MODEL), x_sample.reshape(t_lat, D_MODEL))
    cv = jnp.concatenate([c_ctx[None, :], c, jnp.zeros((15 - db, D_MODEL), F32)], axis=0)
    mod4 = _ada_all(cv, ada_w, ada_b).reshape(L, 16, 1, 6 * D_MODEL)

    bd_a = _block_ones(na, A_QK)
    bd_b = _block_ones(B_HEADS * B_HEAD_PAD, B_HEAD_PAD)
    rope_a = _rope_tables(dseq, A_QK, A_QK, 2 * A_HEADS)
    rope_b = _rope_tables(dseq, B_HEAD_PAD, B_ROPE, B_HEADS)
    pmask = _pool_mask()
    tri = jnp.asarray(np.tril(np.ones((TG, TG), np.float32), -1), dtype=BF16)

    row1 = lambda a: a.reshape(L, 1, -1)
    w_a, w_b, w_cd, w_g = _repack(jnp.transpose(w_in, (0, 2, 1)))
    gq_a = row1(jnp.tile(a_q_norm, (1, 2 * A_HEADS)))
    gk_a = row1(jnp.tile(a_k_norm, (1, 2 * A_HEADS)))
    uq = b_w_uq.reshape(L, B_QL, B_HEADS, B_QK)
    wuq = _b_heads(uq[..., :B_NOPE], uq[..., B_NOPE:]).astype(BF16)
    ukv = b_w_ukv.reshape(L, B_KVL, B_HEADS, B_NOPE + B_V)
    wuk = _b_heads(ukv[..., :B_NOPE], None).astype(BF16)
    wuv = ukv[..., B_NOPE:].reshape(L, B_KVL, B_HEADS * B_V).astype(BF16)
    gq_b = row1(jnp.tile(_b_heads(b_q_norm[:, None, :B_NOPE], b_q_norm[:, None, B_NOPE:]), (1, B_HEADS)))
    gk_b = row1(jnp.tile(_b_heads(b_k_norm[:, None, :B_NOPE], b_k_norm[:, None, B_NOPE:]), (1, B_HEADS)))
    bdd = jnp.zeros((L, D_CH, D_CH), F32)
    for gi in range(len(POOL)):
        bdd = bdd.at[:, gi * D_G:(gi + 1) * D_G, gi * D_G:(gi + 1) * D_G].set(d_w_group[:, gi])
    bdd = bdd.astype(BF16)
    a_wo, b_wo, c_wo, d_wo, wo = (w.astype(BF16) for w in (a_w_o, b_w_o, c_w_o, d_w_o, w_out))
    ffn1, ffn3, ffn2 = (w.astype(BF16) for w in (ffn_w1, ffn_w3, ffn_w2))
    r_pad = jnp.pad(moe_router, ((0, 0), (0, 0), (0, N_EXP_PAD - N_EXP)))
    r_hi = r_pad.astype(BF16)
    r_lo = (r_pad - r_hi.astype(F32)).astype(BF16)
    n1g, n2g = row1(norm1_g), row1(norm2_g)
    g_sub = row1(a_sub_norm)
    gql, gkvl = row1(b_q_lora_norm), row1(b_kv_lora_norm)
    dwb, lng, lnb, dsc = row1(c_dw_b), row1(c_ln_g), row1(c_ln_b), row1(d_scale)

    ck_a = cache_a_k.reshape(db, L, past, na)
    cv_a = cache_a_v.reshape(db, L, past, A_HEADS * A_V)
    kpe_pad = jnp.pad(cache_b_kpe, ((0, 0), (0, 0), (0, 0), (0, LANES - B_ROPE)))
    ck_b, cv_b = _cache_b(cache_b_ckv, kpe_pad, wuk, wuv, gk_b, bd_b)

    n_rows = 2 * T + N_EXP * TG
    nt_g = n_rows // TG

    new_ak = jnp.zeros((nb, L, seq, na), F32)
    new_av = jnp.zeros((nb, L, seq, A_HEADS * A_V), F32)
    new_ckv = jnp.zeros((nb, L, seq, B_KVL), F32)
    new_kpe = jnp.zeros((nb, L, seq, B_ROPE), F32)
    h1 = _prep(x[0], x[1], mod4, n1g, 0, dseq)
    for l in range(L):
        last = l == L - 1

        q1c, q2c, kc, vc, new_ak, new_av = _proj_a(h1, w_a, gq_a, gk_a, bd_a, None, l, 0, t_ctx, False,
                                                   cache_out=(new_ak, new_av))
        q1l, q2l, kl, vl = _proj_a(h1, w_a, gq_a, gk_a, bd_a, rope_a, l, t_ctx, t_lat, True)
        oa = (_attn_a(q1c, q2c, kc, vc, None, None, a_lambda, g_sub, l, seq, False),
              _attn_a(q1l, q2l, kl, vl, ck_a, cv_a, a_lambda, g_sub, l, dseq, True))

        bargs = (w_b, gql, gkvl, wuq, wuk, wuv, gq_b, gk_b, bd_b)
        qc, kc, vc, new_ckv, new_kpe = _proj_b(h1, *bargs, None, l, 0, t_ctx, False,
                                               cache_out=(new_ckv, new_kpe))
        ql, kl, vl = _proj_b(h1, *bargs, rope_b, l, t_ctx, t_lat, True)
        ob = (_attn_b(qc, kc, vc, None, None, l, seq, False),
              _attn_b(ql, kl, vl, ck_b, cv_b, l, dseq, True))

        cdargs = (w_cd, c_dw, dwb, lng, lnb, bdd, dsc, pmask)
        occ, odc = _mix_cd(h1, *cdargs, l, 0, t_ctx, seq)
        ocl, odl = _mix_cd(h1, *cdargs, l, t_ctx, t_lat, dseq)

        j = l // 2
        moe = l % 2 == 1
        router = (r_hi, r_lo) if moe else None
        outs = _merge(x, h1, (oa, ob, (occ, ocl), (odc, odl)), mod4, n2g, w_g, a_wo, b_wo,
                      c_wo, d_wo, wo, router, l, j, t_ctx, dseq)
        if not moe:
            x1, h2 = outs
            assert not last, "a dense layer is always followed by another layer"
            x, h1 = _ffn_dense(h2, ffn1, ffn3, ffn2, x1, mod4, n1g, l, j, t_ctx, dseq)
        else:
            x1, h2t, ridx, rw, sel = outs
            rank, cnt = _rank(sel, tri)
            counts = cnt[0, :N_EXP].astype(I32)
            padded = ((counts + TG - 1) // TG) * TG
            ends = jnp.cumsum(padded)
            offs = ends - padded
            pos_all = offs[None, :] + rank[:, :N_EXP].astype(I32)
            lane8 = jnp.arange(N_EXP, dtype=I32)[None, :]
            pos0 = jnp.sum(jnp.where(lane8 == ridx[:, 0:1], pos_all, 0), axis=1)
            pos1 = jnp.sum(jnp.where(lane8 == ridx[:, 1:2], pos_all, 0), axis=1)
            tile_start = jnp.arange(nt_g, dtype=I32) * TG
            tile_e = jnp.minimum(jnp.sum(tile_start[:, None] >= ends[None, :], axis=1), N_EXP - 1).astype(I32)
            n_active = (ends[-1] // TG).astype(I32).reshape(1)
            last_e = tile_e[jnp.maximum(n_active[0] - 1, 0)]
            tile_e = jnp.where(tile_start < ends[-1], tile_e, last_e)
            xs = _dispatch(pos0, pos1, ends.astype(I32), padded, h2t, n_rows)
            ys = _ffn_experts(xs, moe_w1, moe_w3, moe_w2, tile_e, n_active, j)
            outs = _combine(pos0, pos1, ys, x1, rw, mod4, n1g, l, t_ctx, dseq, not last)
            if last:
                y_ctx, y_lat = outs
            else:
                x, h1 = outs

    assert L % 2 == 0, "the last layer is a routed layer, whose combine step emits the two outputs"
    y_prompt = y_ctx.reshape(nb, seq, D_MODEL)
    y_sample = y_lat.reshape(db, dseq, D_MODEL)
    new_a_k = new_ak.reshape(nb, L, seq, 2 * A_HEADS, A_QK)
    new_a_v = new_av.reshape(nb, L, seq, A_HEADS, A_V)
    return (y_prompt, y_sample, new_a_k, new_a_v, new_ckv, new_kpe)
```
